```python
import math
import jax
import jax.numpy as jnp
from jax import lax
import numpy as np

D_MODEL = 1024
BATCH = 8
SEQ = 4096
DEPTH = 4

GRID_W = 64
CTX_LEN = 256

HY_W = 256
N_HEADS = 8
N_KV_HEADS = 2
N_GROUPS = N_HEADS // N_KV_HEADS
HEAD_DIM = 64
ATT_W = N_HEADS * HEAD_DIM
KV_W = N_KV_HEADS * HEAD_DIM
RG_W = 256
RG_BLOCKS = 4
RG_BW = RG_W // RG_BLOCKS
MIX_W = HY_W + ATT_W + RG_W

HY_END = 3 * HY_W
Q_END = HY_END + ATT_W
K_END = Q_END + KV_W
V_END = K_END + KV_W
RGX_END = V_END + RG_W
PROJ_W = RGX_END + RG_W

HY_ORDER = 2
HY_SHORT = 3
HY_BANDS = 8
HY_EMB = 1 + 2 * HY_BANDS
HY_FILTER_W = 64
HY_DECAY_TARGET = 1e-2
HY_SHORT_DECAY_PCT = 0.3
HY_LONG_DECAY_PCT = 1.5
HY_MAX_DECAY = math.log(HY_DECAY_TARGET) / HY_SHORT_DECAY_PCT
HY_MIN_DECAY = math.log(HY_DECAY_TARGET) / HY_LONG_DECAY_PCT

ROPE_THETA = 10000.0
Q_BLOCK = 128
QK_EPS = 1e-6

RG_CONV = 4
RG_C = 8.0

N_EXPERTS = 64
TOP_K = 8
EXPERT_FF = 256
SHARED_FF = 256
ROUTED_SCALE = 2.5
MOE_CHUNK = 1024

LN_EPS = 1e-6
DN_ALPHA = (2 * DEPTH) ** 0.25
DN_BETA = (8 * DEPTH) ** -0.25

kernel_name = 'hybrid_dit_hyena_gqa_rglru_moe'


def layer_norm(u, g, b):
    u32 = u.astype(jnp.float32)
    mu = jnp.mean(u32, axis=-1, keepdims=True)
    var = jnp.mean(jnp.square(u32 - mu), axis=-1, keepdims=True)
    return ((u32 - mu) * lax.rsqrt(var + LN_EPS) * g + b).astype(u.dtype)


def rms_norm(u, g):
    u32 = u.astype(jnp.float32)
    return (u32 * lax.rsqrt(jnp.mean(jnp.square(u32), axis=-1, keepdims=True) + QK_EPS) * g).astype(u.dtype)


def depthwise_conv(u, w, b):
    width = w.shape[0]
    L = u.shape[1]
    left = (width - 1) // 2
    up = jnp.pad(u, ((0, 0), (left, width - 1 - left), (0, 0)))
    return sum(up[:, j:j + L] * w[j] for j in range(width)) + b


def split_proj(p):
    return (p[..., :HY_END], p[..., HY_END:Q_END], p[..., Q_END:K_END],
            p[..., K_END:V_END], p[..., V_END:RGX_END], p[..., RGX_END:])


def hyena_filter_spectra(L, w1, b1, freq, w2, b2, w3):
    t = jnp.linspace(0.0, 1.0, L, dtype=jnp.float32)[:, None]
    w = 2.0 * math.pi * jnp.arange(L, dtype=jnp.float32)[:, None] / L
    bands = jnp.linspace(1e-4, HY_BANDS - 1, HY_BANDS, dtype=jnp.float32)
    z = jnp.concatenate([t, jnp.cos(bands * w), -jnp.sin(bands * w)], axis=-1)
    f = jnp.sin(freq * (z @ w1 + b1))
    f = jnp.sin(freq * (f @ w2 + b2))
    f = (f @ w3).astype(jnp.float32).reshape(L, 2, HY_ORDER, HY_W)
    deltas = jnp.abs(jnp.linspace(HY_MIN_DECAY, HY_MAX_DECAY, HY_W, dtype=jnp.float32))
    f = f * jnp.exp(-t * deltas)[:, None, None, :]
    fwd, bwd = f[:, 0], f[:, 1]
    k = jnp.concatenate([fwd, jnp.zeros_like(fwd[:1]), bwd[:0:-1]], axis=0)
    return jnp.fft.rfft(k, axis=0)


def fft_long_conv(u, k_f, skip):
    L = u.shape[1]
    u32 = u.astype(jnp.float32)
    y = jnp.fft.irfft(jnp.fft.rfft(u32, n=2 * L, axis=1) * k_f, n=2 * L, axis=1)[:, :L]
    return (y + u32 * skip.astype(jnp.float32)).astype(u.dtype)


def hyena_mixer(u, short_w, short_b, w1, b1, freq, w2, b2, w3, skip):
    L = u.shape[1]
    x1, x2, z = jnp.split(depthwise_conv(u, short_w, short_b), 3, axis=-1)
    k_f = hyena_filter_spectra(L, w1, b1, freq, w2, b2, w3)
    for n, gate in enumerate((x1, x2)):
        z = gate * fft_long_conv(z, k_f[:, n], skip[n])
    return z


def axial_rope(n_tokens):
    n_rows = n_tokens // GRID_W
    row = jnp.repeat(jnp.arange(n_rows, dtype=jnp.float32), GRID_W)
    col = jnp.tile(jnp.arange(GRID_W, dtype=jnp.float32), n_rows)
    axis_dim = HEAD_DIM // 2
    inv_freq = ROPE_THETA ** (-jnp.arange(0, axis_dim, 2, dtype=jnp.float32) / axis_dim)
    ang = jnp.concatenate([row[:, None] * inv_freq, col[:, None] * inv_freq], axis=-1)
    return jnp.cos(ang), jnp.sin(ang)


def apply_rope(u, cos, sin):
    u32 = u.astype(jnp.float32).reshape(*u.shape[:-1], HEAD_DIM // 2, 2)
    u1, u2 = u32[..., 0], u32[..., 1]
    out = jnp.stack([u1 * cos - u2 * sin, u1 * sin + u2 * cos], axis=-1)
    return out.reshape(u.shape).astype(u.dtype)


def gqa_q(q, gain):
    B, L, _ = q.shape
    q = rms_norm(q.reshape(B, L, N_KV_HEADS, N_GROUPS, HEAD_DIM), gain)
    return q.transpose(0, 2, 3, 1, 4)


def gqa_kv(k, gain=None):
    B, L, _ = k.shape
    k = k.reshape(B, L, N_KV_HEADS, HEAD_DIM)
    if gain is not None:
        k = rms_norm(k, gain)
    return k.transpose(0, 2, 1, 3)


def sdpa(q, k, v):
    s = jnp.einsum('bkgqd,bksd->bkgqs', q, k, preferred_element_type=jnp.float32) * HEAD_DIM ** -0.5
    p = jax.nn.softmax(s, axis=-1).astype(v.dtype)
    return jnp.einsum('bkgqs,bksd->bkgqd', p, v)


def attention_mixer(q_l, k_l, v_l, q_c, k_c, v_c, q_gain, k_gain, cos, sin):
    B, T, _ = q_l.shape
    kc = gqa_kv(k_c, k_gain)
    vc = gqa_kv(v_c)
    ql = apply_rope(gqa_q(q_l, q_gain), cos, sin)
    k_all = jnp.concatenate([apply_rope(gqa_kv(k_l, k_gain), cos, sin), kc], axis=2)
    v_all = jnp.concatenate([gqa_kv(v_l), vc], axis=2)
    n_blk = T // Q_BLOCK
    q_blocks = ql.reshape(B, N_KV_HEADS, N_GROUPS, n_blk, Q_BLOCK, HEAD_DIM).transpose(3, 0, 1, 2, 4, 5)
    o = lax.map(lambda qb: sdpa(qb, k_all, v_all), q_blocks)
    y_l = o.transpose(1, 0, 4, 2, 3, 5).reshape(B, T, ATT_W)
    if q_c is None:
        return y_l, None
    o_c = sdpa(gqa_q(q_c, q_gain), kc, vc)
    y_c = o_c.transpose(0, 3, 1, 2, 4).reshape(B, q_c.shape[1], ATT_W)
    return y_l, y_c


def _linear_combine(e1, e2):
    a1, b1 = e1
    a2, b2 = e2
    return a1 * a2, a2 * b1 + b2


def rg_lru_scan(u, lam, w_a, b_a, w_x, b_x, h0, reverse):
    B, L, _ = u.shape
    ub = u.reshape(B, L, RG_BLOCKS, RG_BW)
    r = jax.nn.sigmoid(jnp.einsum('blhi,hij->blhj', ub, w_a).reshape(B, L, RG_W) + b_a).astype(jnp.float32)
    i = jax.nn.sigmoid(jnp.einsum('blhi,hij->blhj', ub, w_x).reshape(B, L, RG_W) + b_x)
    log_a = -RG_C * r * jax.nn.softplus(-lam.astype(jnp.float32))
    a = jnp.exp(log_a)
    b = jnp.sqrt(-jnp.expm1(2.0 * log_a)) * (i * u).astype(jnp.float32)
    first = L - 1 if reverse else 0
    b = b.at[:, first].add(a[:, first] * h0)
    _, h = lax.associative_scan(_linear_combine, (a, b), axis=1, reverse=reverse)
    return h


def rglru_mixer(x_l, g_l, x_c, g_c, conv_w, conv_b, lam, w_a, b_a, w_x, b_x):
    u_l = depthwise_conv(x_l, conv_w, conv_b)
    u_c = depthwise_conv(x_c, conv_w, conv_b)
    zeros = jnp.zeros((x_c.shape[0], RG_W), jnp.float32)
    h_l_sum = 0.0
    h_c_sum = 0.0
    for d, rev in enumerate((False, True)):
        params = (lam[d], w_a[d], b_a[d], w_x[d], b_x[d])
        h_c = rg_lru_scan(u_c, *params, zeros, rev)
        h_end = h_c[:, 0] if rev else h_c[:, -1]
        h_l_sum = h_l_sum + rg_lru_scan(u_l, *params, h_end, rev)
        if g_c is not None:
            h_c_sum = h_c_sum + h_c
    y_l = h_l_sum.astype(x_l.dtype) * jax.nn.gelu(g_l)
    y_c = None if g_c is None else h_c_sum.astype(x_c.dtype) * jax.nn.gelu(g_c)
    return y_l, y_c


def moe_ffn(u, w_router, b_router, w_gate, w_up, w_down, ws_gate, ws_up, ws_down):
    n, d = u.shape
    scores = jax.nn.sigmoid(jnp.einsum('nd,de->ne', u, w_router, preferred_element_type=jnp.float32))
    _, idx = lax.top_k(scores + b_router.astype(jnp.float32), TOP_K)
    sel = jnp.take_along_axis(scores, idx, axis=-1)
    wts = sel / jnp.sum(sel, axis=-1, keepdims=True) * ROUTED_SCALE
    gates = jnp.zeros((n, N_EXPERTS), jnp.float32).at[jnp.arange(n)[:, None], idx].add(wts).astype(u.dtype)
    pad = (-n) % MOE_CHUNK
    u_blk = jnp.pad(u, ((0, pad), (0, 0))).reshape(-1, MOE_CHUNK, d)
    g_blk = jnp.pad(gates, ((0, pad), (0, 0))).reshape(-1, MOE_CHUNK, N_EXPERTS)

    def routed(args):
        ub, gb = args
        hid = jax.nn.silu(jnp.einsum('nd,edf->nef', ub, w_gate)) * jnp.einsum('nd,edf->nef', ub, w_up)
        return jnp.einsum('nef,efd->nd', hid * gb[..., None], w_down)

    y = lax.map(routed, (u_blk, g_blk)).reshape(-1, d)[:n]
    shared = (jax.nn.silu(u @ ws_gate) * (u @ ws_up)) @ ws_down
    return y + shared


def setup_inputs(seed: int = 0) -> dict:
    key = jax.random.key(seed)
    ks = iter(jax.random.split(key, 48))

    def nrm(shape, scale):
        return jax.random.normal(next(ks), shape, jnp.float32) * scale

    D = D_MODEL
    u = jax.random.uniform(next(ks), (DEPTH, 2, RG_W), jnp.float32, 0.9, 0.999)
    s = u ** (1.0 / RG_C)
    rg_lambda = jnp.log(s) - jnp.log1p(-s)
    return {
        'x': nrm((BATCH, SEQ, D), 1.0),
        'c': nrm((BATCH, D), 1.0),
        'ctx': nrm((BATCH, CTX_LEN, D), 1.0),
        'c_ctx': nrm((D,), 1.0),
        'w_mod': nrm((DEPTH, D, 6 * D), 0.5 * D ** -0.5),
        'b_mod': nrm((DEPTH, 6 * D), 0.02),
        'ln1_g': 1.0 + nrm((DEPTH, D), 0.02),
        'ln1_b': nrm((DEPTH, D), 0.02),
        'ln2_g': 1.0 + nrm((DEPTH, D), 0.02),
        'ln2_b': nrm((DEPTH, D), 0.02),
        'w_in': nrm((DEPTH, D, PROJ_W), D ** -0.5),
        'w_out': nrm((DEPTH, MIX_W, D), DN_BETA * MIX_W ** -0.5),
        'hy_short_w': nrm((DEPTH, HY_SHORT, 3 * HY_W), HY_SHORT ** -0.5),
        'hy_short_b': nrm((DEPTH, 3 * HY_W), 0.02),
        'hy_f_w1': nrm((DEPTH, HY_EMB, HY_FILTER_W), HY_EMB ** -0.5),
        'hy_f_b1': nrm((DEPTH, HY_FILTER_W), 0.02),
        'hy_f_freq': 1.0 + nrm((DEPTH, HY_FILTER_W), 0.1),
        'hy_f_w2': nrm((DEPTH, HY_FILTER_W, HY_FILTER_W), HY_FILTER_W ** -0.5),
        'hy_f_b2': nrm((DEPTH, HY_FILTER_W), 0.02),
        'hy_f_w3': nrm((DEPTH, HY_FILTER_W, 2 * HY_ORDER * HY_W), 0.04 * HY_FILTER_W ** -0.5),
        'hy_skip': nrm((DEPTH, HY_ORDER, HY_W), 1.0),
        'q_norm': 1.0 + nrm((DEPTH, HEAD_DIM), 0.02),
        'k_norm': 1.0 + nrm((DEPTH, HEAD_DIM), 0.02),
        'rg_conv_w': nrm((DEPTH, RG_CONV, RG_W), RG_CONV ** -0.5),
        'rg_conv_b': nrm((DEPTH, RG_W), 0.02),
        'rg_lambda': rg_lambda,
        'rg_w_a': nrm((DEPTH, 2, RG_BLOCKS, RG_BW, RG_BW), RG_BW ** -0.5),
        'rg_b_a': nrm((DEPTH, 2, RG_W), 0.02),
        'rg_w_x': nrm((DEPTH, 2, RG_BLOCKS, RG_BW, RG_BW), RG_BW ** -0.5),
        'rg_b_x': nrm((DEPTH, 2, RG_W), 0.02),
        'w_router': nrm((DEPTH, D, N_EXPERTS), D ** -0.5),
        'b_router': nrm((DEPTH, N_EXPERTS), 0.01),
        'w_gate': nrm((DEPTH, N_EXPERTS, D, EXPERT_FF), D ** -0.5),
        'w_up': nrm((DEPTH, N_EXPERTS, D, EXPERT_FF), D ** -0.5),
        'w_down': nrm((DEPTH, N_EXPERTS, EXPERT_FF, D), DN_BETA * EXPERT_FF ** -0.5),
        'ws_gate': nrm((DEPTH, D, SHARED_FF), D ** -0.5),
        'ws_up': nrm((DEPTH, D, SHARED_FF), D ** -0.5),
        'ws_down': nrm((DEPTH, SHARED_FF, D), DN_BETA * SHARED_FF ** -0.5),
    }


def reference(x, c, ctx, c_ctx, w_mod, b_mod, ln1_g, ln1_b, ln2_g, ln2_b, w_in, w_out,
              hy_short_w, hy_short_b, hy_f_w1, hy_f_b1, hy_f_freq, hy_f_w2, hy_f_b2, hy_f_w3, hy_skip,
              q_norm, k_norm, rg_conv_w, rg_conv_b, rg_lambda, rg_w_a, rg_b_a, rg_w_x, rg_b_x,
              w_router, b_router, w_gate, w_up, w_down, ws_gate, ws_up, ws_down):
    B, T, D = x.shape
    Lc = ctx.shape[1]
    cos, sin = axial_rope(T)
    s_c = jax.nn.silu(c)
    s_cc = jax.nn.silu(c_ctx)
    h, hc = x, ctx
    for l in range(DEPTH):
        last = l == DEPTH - 1
        sh1, sc1, g1, sh2, sc2, g2 = jnp.split((s_c @ w_mod[l] + b_mod[l])[:, None, :], 6, axis=-1)
        n_mod_c = 2 if last else 6
        mod_c = jnp.split(s_cc @ w_mod[l, :, :n_mod_c * D] + b_mod[l, :n_mod_c * D], n_mod_c)

        a_l = h * (1.0 + sc1) + sh1
        a_c = hc * (1.0 + mod_c[1]) + mod_c[0]
        hy_l, q_l, k_l, v_l, rgx_l, rgg_l = split_proj(a_l @ w_in[l])
        if last:
            k_c, v_c, rgx_c = jnp.split(a_c @ w_in[l, :, Q_END:RGX_END], [KV_W, 2 * KV_W], axis=-1)
            q_c = None
            rgg_c = None
        else:
            hy_c, q_c, k_c, v_c, rgx_c, rgg_c = split_proj(a_c @ w_in[l])
        hy_p = (hy_short_w[l], hy_short_b[l], hy_f_w1[l], hy_f_b1[l], hy_f_freq[l],
                hy_f_w2[l], hy_f_b2[l], hy_f_w3[l], hy_skip[l])
        att_l, att_c = attention_mixer(q_l, k_l, v_l, q_c, k_c, v_c, q_norm[l], k_norm[l], cos, sin)
        rg_l, rg_c = rglru_mixer(rgx_l, rgg_l, rgx_c, rgg_c, rg_conv_w[l], rg_conv_b[l],
                                 rg_lambda[l], rg_w_a[l], rg_b_a[l], rg_w_x[l], rg_b_x[l])
        y_l = jnp.concatenate([hyena_mixer(hy_l, *hy_p), att_l, rg_l], axis=-1) @ w_out[l]
        h = layer_norm(DN_ALPHA * h + g1 * y_l, ln1_g[l], ln1_b[l])

        moe_p = (w_router[l], b_router[l], w_gate[l], w_up[l], w_down[l], ws_gate[l], ws_up[l], ws_down[l])
        m_l = (h * (1.0 + sc2) + sh2).reshape(B * T, D)
        if last:
            f_l = moe_ffn(m_l, *moe_p)
        else:
            y_c = jnp.concatenate([hyena_mixer(hy_c, *hy_p), att_c, rg_c], axis=-1) @ w_out[l]
            hc = layer_norm(DN_ALPHA * hc + mod_c[2] * y_c, ln1_g[l], ln1_b[l])
            m_c = (hc * (1.0 + mod_c[4]) + mod_c[3]).reshape(B * Lc, D)
            f = moe_ffn(jnp.concatenate([m_l, m_c], axis=0), *moe_p)
            f_l = f[:B * T]
            hc = layer_norm(DN_ALPHA * hc + mod_c[5] * f[B * T:].reshape(B, Lc, D), ln2_g[l], ln2_b[l])
        h = layer_norm(DN_ALPHA * h + g2 * f_l.reshape(B, T, D), ln2_g[l], ln2_b[l])
    return h
```

```python
import functools
import math

import numpy as np
import jax
import jax.numpy as jnp
from jax import lax
from jax.experimental import pallas as pl
from jax.experimental.pallas import tpu as pltpu

f32 = jnp.float32
bf16 = jnp.bfloat16

D_MODEL = 1024
GRID_W = 64
HY_W = 256
N_HEADS = 8
N_KV_HEADS = 2
N_GROUPS = N_HEADS // N_KV_HEADS
HEAD_DIM = 64
ATT_W = N_HEADS * HEAD_DIM
KV_W = N_KV_HEADS * HEAD_DIM
RG_W = 256
RG_BLOCKS = 4
HY_END = 3 * HY_W
Q_END = HY_END + ATT_W
K_END = Q_END + KV_W
V_END = K_END + KV_W
RGX_END = V_END + RG_W
PROJ_W = RGX_END + RG_W
HY_BANDS = 8
HY_EMB = 1 + 2 * HY_BANDS
HY_MAX_DECAY = math.log(1e-2) / 0.3
HY_MIN_DECAY = math.log(1e-2) / 1.5
ROPE_THETA = 10000.0
QK_EPS = 1e-6
RG_C = 8.0
N_EXPERTS = 64
TOP_K = 8
EXPERT_FF = 256
ROUTED_SCALE = 2.5
LN_EPS = 1e-6

LANE = 128
HEAD_PAD = LANE
QP_W = N_HEADS * HEAD_PAD
KVP_W = N_KV_HEADS * HEAD_PAD
PROJ_PAD_W = HY_END + QP_W + 2 * KVP_W + 2 * RG_W
VMEM_LIMIT = 52 * 1024 * 1024

FFT_N2 = 128
FFT_PASSES = 3
TOKEN_TILE = 512
ATT_Q_TILE = 256
SCAN_TILE = 256


def _cparams(sem):
    return pltpu.CompilerParams(dimension_semantics=sem, vmem_limit_bytes=VMEM_LIMIT)


def _dot(a, b):
    return jnp.dot(a, b, preferred_element_type=f32)


def _split(x):
    hi = x.astype(bf16)
    lo = (x - hi.astype(f32)).astype(bf16)
    return hi, lo


def _np_split(x):
    x = jnp.asarray(np.asarray(x, np.float32))
    return _split(x)


def _dot3(a, w_hi, w_lo):
    a_hi, a_lo = _split(a)
    return _dot(a_hi, w_hi) + _dot(a_lo, w_hi) + _dot(a_hi, w_lo)


def _stack_dft(mat):
    re_hi, re_lo = _np_split(mat.real)
    im_hi, im_lo = _np_split(mat.imag)
    return jnp.concatenate([re_hi, im_hi, re_lo, im_lo], axis=0)


def _apply_stack(stack, x, passes):
    m2 = stack.shape[0] // 2
    x_hi, x_lo = _split(x)
    if passes == 1:
        return _dot(stack[:m2], x_hi)
    r = _dot(stack, x_hi)
    out = r[:m2] + r[m2:]
    if passes >= 3:
        out = out + _dot(stack[:m2], x_lo)
    return out


def _mod_kernel(a_ref, w_ref, b_ref, o_ref):
    a = a_ref[...]
    a = a * jax.nn.sigmoid(a)
    w_hi, w_lo = _split(w_ref[...])
    o_ref[...] = _dot3(a, w_hi, w_lo) + b_ref[...]


def _modulation(cond, w_mod, b_mod):
    depth, d, n = w_mod.shape
    rows = cond.shape[0]
    tn = 512
    return pl.pallas_call(
        _mod_kernel,
        out_shape=jax.ShapeDtypeStruct((depth, rows, n), f32),
        grid=(depth, n // tn),
        in_specs=[
            pl.BlockSpec((rows, d), lambda l, j: (0, 0)),
            pl.BlockSpec((None, d, tn), lambda l, j: (l, 0, j)),
            pl.BlockSpec((None, 1, tn), lambda l, j: (l, 0, j)),
        ],
        out_specs=pl.BlockSpec((None, rows, tn), lambda l, j: (l, 0, j)),
        compiler_params=_cparams(("parallel", "parallel")),
        name="modulation",
    )(cond, w_mod, b_mod.reshape(depth, 1, n))


def _inproj_kernel(h_ref, sh_ref, sc_ref, w_ref, ohy_ref, oq_ref, okv_ref, org_ref):
    a = h_ref[...] * (1.0 + sc_ref[...]) + sh_ref[...]
    p = _dot(a.astype(bf16), w_ref[...])
    c0, c1, c2 = HY_END, HY_END + QP_W, HY_END + QP_W + 2 * KVP_W
    ohy_ref[...] = p[:, :c0]
    oq_ref[...] = p[:, c0:c1]
    okv_ref[...] = p[:, c1:c2]
    org_ref[...] = p[:, c2:]


def _mod_spec(seg, j):
    return pl.BlockSpec((None, 1, D_MODEL), lambda i: (seg(i), 0, j))


def _inproj(h, mods3, w_pad, seg, tm):
    n = h.shape[0]
    widths = (HY_END, QP_W, 2 * KVP_W, 2 * RG_W)
    return pl.pallas_call(
        _inproj_kernel,
        out_shape=[jax.ShapeDtypeStruct((n, w), f32) for w in widths],
        grid=(n // tm,),
        in_specs=[
            pl.BlockSpec((tm, D_MODEL), lambda i: (i, 0)),
            _mod_spec(seg, 0),
            _mod_spec(seg, 1),
            pl.BlockSpec((D_MODEL, PROJ_PAD_W), lambda i: (0, 0)),
        ],
        out_specs=[pl.BlockSpec((tm, w), lambda i: (i, 0)) for w in widths],
        compiler_params=_cparams(("parallel",)),
        name="inproj",
    )(h, mods3, mods3, w_pad)


def _dwconv_kernel(x_ref, w_ref, b_ref, o_ref, *, width, left):
    x = x_ref[...]
    n = x.shape[0]
    row = lax.broadcasted_iota(jnp.int32, x.shape, 0)
    acc = jnp.zeros_like(x) + b_ref[...]
    for j in range(width):
        off = j - left
        if off == 0:
            xs = x
        else:
            xs = pltpu.roll(x, (-off) % n, axis=0)
            valid = jnp.logical_and(row + off >= 0, row + off < n)
            xs = jnp.where(valid, xs, 0.0)
        acc = acc + xs * w_ref[j:j + 1, :]
    o_ref[...] = acc


def _dwconv(x, w, b, seq_len, row_block0, n_seq, col_block0, n_col_blocks):
    width = w.shape[0]
    wp = jnp.zeros((8, w.shape[1]), f32).at[:width].set(w)
    ct = 256
    return pl.pallas_call(
        functools.partial(_dwconv_kernel, width=width, left=(width - 1) // 2),
        out_shape=jax.ShapeDtypeStruct((n_seq * seq_len, n_col_blocks * ct), f32),
        grid=(n_seq, n_col_blocks),
        in_specs=[
            pl.BlockSpec((seq_len, ct), lambda s, c: (row_block0 + s, col_block0 + c)),
            pl.BlockSpec((8, ct), lambda s, c: (0, c)),
            pl.BlockSpec((1, ct), lambda s, c: (0, c)),
        ],
        out_specs=pl.BlockSpec((seq_len, ct), lambda s, c: (s, c)),
        compiler_params=_cparams(("parallel", "parallel")),
        name="dwconv",
    )(x, wp, b.reshape(1, -1))


def _filter_kernel(z_ref, w1h_ref, w1l_ref, b1_ref, fr_ref, w2h_ref, w2l_ref, b2_ref,
                   w3h_ref, w3l_ref, win_ref, o_ref):
    fr = fr_ref[...]
    f = jnp.sin(fr * (_dot3(z_ref[...], w1h_ref[...], w1l_ref[...]) + b1_ref[...]))
    f = jnp.sin(fr * (_dot3(f, w2h_ref[...], w2l_ref[...]) + b2_ref[...]))
    f = _dot3(f, w3h_ref[...], w3l_ref[...])
    win = win_ref[...]
    o_ref[...] = f * jnp.concatenate([win] * 4, axis=1)


def _filter_features(L):
    t = np.linspace(0.0, 1.0, L, dtype=np.float32)[:, None].astype(np.float64)
    w = np.float32(2.0 * math.pi) * np.arange(L, dtype=np.float32)[:, None] / np.float32(L)
    bands = np.linspace(1e-4, HY_BANDS - 1, HY_BANDS, dtype=np.float32)
    arg = (bands * w).astype(np.float64)
    z = np.concatenate([t, np.cos(arg), -np.sin(arg)], axis=-1)
    zp = np.zeros((L, LANE), np.float32)
    zp[:, :HY_EMB] = z
    deltas = np.abs(np.linspace(HY_MIN_DECAY, HY_MAX_DECAY, HY_W, dtype=np.float32)).astype(np.float64)
    win = np.exp(-t * deltas).astype(np.float32)
    return jnp.asarray(zp), jnp.asarray(win)


def _pad2(w, rows, cols):
    return jnp.zeros((rows, cols), f32).at[:w.shape[0], :w.shape[1]].set(w)


def _hyena_filters(L, w1, b1, freq, w2, b2, w3):
    zp, win = _filter_features(L)
    w1h, w1l = _split(_pad2(w1, LANE, LANE))
    w2h, w2l = _split(_pad2(w2, LANE, LANE))
    w3h, w3l = _split(_pad2(w3, LANE, 4 * HY_W))
    b1p = _pad2(b1[None, :], 1, LANE)
    b2p = _pad2(b2[None, :], 1, LANE)
    frp = _pad2(freq[None, :], 1, LANE)
    tl = min(L, 512)
    full = lambda shape: pl.BlockSpec(shape, lambda i: (0, 0))
    return pl.pallas_call(
        _filter_kernel,
        out_shape=jax.ShapeDtypeStruct((L, 4 * HY_W), f32),
        grid=(L // tl,),
        in_specs=[
            pl.BlockSpec((tl, LANE), lambda i: (i, 0)),
            full((LANE, LANE)), full((LANE, LANE)), full((1, LANE)), full((1, LANE)),
            full((LANE, LANE)), full((LANE, LANE)), full((1, LANE)),
            full((LANE, 4 * HY_W)), full((LANE, 4 * HY_W)),
            pl.BlockSpec((tl, HY_W), lambda i: (i, 0)),
        ],
        out_specs=pl.BlockSpec((tl, 4 * HY_W), lambda i: (i, 0)),
        compiler_params=_cparams(("parallel",)),
        name="hyena_filter",
    )(zp, w1h, w1l, b1p, frp, w2h, w2l, b2p, w3h, w3l, win)


def _circular_filter(f):
    half = 2 * HY_W
    fwd, bwd = f[:, :half], f[:, half:]
    return jnp.concatenate([fwd, jnp.zeros((1, half), f32), bwd[:0:-1]], axis=0)


def _dft_consts(n):
    n1 = n // FFT_N2
    k = np.arange(n1)
    f_n1 = np.exp(-2j * np.pi * np.outer(k, k) / n1)
    k2 = np.arange(FFT_N2)
    f_n2 = np.exp(-2j * np.pi * np.outer(k2, k2) / FFT_N2)
    tw = np.exp(-2j * np.pi * np.outer(k, k2) / n).reshape(n, 1)
    tw_re = jnp.asarray(np.broadcast_to(tw.real, (n, LANE)).astype(np.float32))
    tw_im = jnp.asarray(np.broadcast_to(tw.imag, (n, LANE)).astype(np.float32))
    return n1, f_n1, f_n2, tw_re, tw_im


def _fa_kernel(u_ref, fs_ref, are_ref, aim_ref, *, grp, passes, real_only):
    fs = fs_ref[...]
    n1 = fs.shape[0] // 4
    jb, c = u_ref.shape[2], u_ref.shape[3]
    for g in range(jb // grp):
        def gather(bi):
            return jnp.concatenate([u_ref[bi, :, g * grp + jj, :] for jj in range(grp)], axis=1)
        p = _apply_stack(fs, gather(0), passes)
        if real_only:
            re, im = p[:n1], p[n1:]
        else:
            q = _apply_stack(fs, gather(1), passes)
            re, im = p[:n1] - q[n1:], p[n1:] + q[:n1]
        for jj in range(grp):
            are_ref[:, g * grp + jj, :] = re[:, jj * c:(jj + 1) * c]
            aim_ref[:, g * grp + jj, :] = im[:, jj * c:(jj + 1) * c]


def _fft_step_a(u4, fs, col_block, c, real_only, passes):
    s, n1_in = u4.shape[0], u4.shape[1]
    n1 = fs.shape[0] // 4
    per = 1 if real_only else 2
    jb = 32
    grp = max(1, 1024 // c)
    shape = jax.ShapeDtypeStruct((s // per, n1, FFT_N2, c), f32)
    return pl.pallas_call(
        functools.partial(_fa_kernel, grp=grp, passes=passes, real_only=real_only),
        out_shape=[shape, shape],
        grid=(s // per, FFT_N2 // jb),
        in_specs=[
            pl.BlockSpec((per, n1_in, jb, c), lambda p, j: (p, 0, j, col_block)),
            pl.BlockSpec(fs.shape, lambda p, j: (0, 0)),
        ],
        out_specs=[pl.BlockSpec((None, n1, jb, c), lambda p, j: (p, 0, j, 0))] * 2,
        compiler_params=_cparams(("parallel", "parallel")),
        name="fft_step_a",
    )(u4, fs)


def _mid_kernel(are_ref, aim_ref, twr_ref, twi_ref, f2_ref, *rest, kb, passes, fwd_only, inv_n):
    if fwd_only:
        ore_ref, oim_ref = rest
    else:
        kr_ref, ki_ref, ore_ref, oim_ref = rest
    f2 = f2_ref[...]
    c = are_ref.shape[1]
    for kk in range(kb):
        rows = pl.ds(kk * FFT_N2, FFT_N2)
        ar, ai = are_ref[rows, :], aim_ref[rows, :]
        tr = jnp.concatenate([twr_ref[rows, :]] * (c // LANE), axis=1)
        ti = jnp.concatenate([twi_ref[rows, :]] * (c // LANE), axis=1)
        xr = ar * tr - ai * ti
        xi = ar * ti + ai * tr
        p = _apply_stack(f2, xr, passes)
        q = _apply_stack(f2, xi, passes)
        sr = p[:FFT_N2] - q[FFT_N2:]
        si = p[FFT_N2:] + q[:FFT_N2]
        if fwd_only:
            ore_ref[rows, :] = sr
            oim_ref[rows, :] = si
        else:
            kr, ki = kr_ref[rows, :], ki_ref[rows, :]
            yr = sr * kr - si * ki
            yi = sr * ki + si * kr
            p2 = _apply_stack(f2, yr, passes)
            q2 = _apply_stack(f2, yi, passes)
            br = p2[:FFT_N2] + q2[FFT_N2:]
            bi = q2[:FFT_N2] - p2[FFT_N2:]
            ore_ref[rows, :] = (br * tr + bi * ti) * inv_n
            oim_ref[rows, :] = (bi * tr - br * ti) * inv_n


def _fft_mid(a_re, a_im, tw_re, tw_im, f2s, kf=None, order=0, passes=FFT_PASSES):
    p, n, c = a_re.shape
    kb = min(4, n // FFT_N2)
    rb = kb * FFT_N2
    fwd_only = kf is None
    blk = pl.BlockSpec((None, rb, c), lambda i, k: (i, k, 0))
    in_specs = [blk, blk,
                pl.BlockSpec((rb, LANE), lambda i, k: (k, 0)),
                pl.BlockSpec((rb, LANE), lambda i, k: (k, 0)),
                pl.BlockSpec(f2s.shape, lambda i, k: (0, 0))]
    args = [a_re, a_im, tw_re, tw_im, f2s]
    if not fwd_only:
        in_specs += [pl.BlockSpec((rb, c), lambda i, k: (k, order))] * 2
        args += list(kf)
    shape = jax.ShapeDtypeStruct((p, n, c), f32)
    return pl.pallas_call(
        functools.partial(_mid_kernel, kb=kb, passes=passes, fwd_only=fwd_only, inv_n=1.0 / n),
        out_shape=[shape, shape],
        grid=(p, n // rb),
        in_specs=in_specs,
        out_specs=[blk, blk],
        compiler_params=_cparams(("parallel", "parallel")),
        name="fft_mid",
    )(*args)


def _fai_kernel(bre_ref, bim_ref, gs_ref, u_ref, gate_ref, skip_ref, o_ref, *, grp, passes):
    gs = gs_ref[...]
    n1h = gs.shape[0] // 4
    jb, c = u_ref.shape[2], u_ref.shape[3]
    skip = skip_ref[...]
    for g in range(jb // grp):
        def gather(ref):
            return jnp.concatenate([ref[:, g * grp + jj, :] for jj in range(grp)], axis=1)
        p = _apply_stack(gs, gather(bre_ref), passes)
        q = _apply_stack(gs, gather(bim_ref), passes)
        ya = p[:n1h] - q[n1h:]
        yb = p[n1h:] + q[:n1h]
        for jj in range(grp):
            j = g * grp + jj
            for bi, y in ((0, ya), (1, yb)):
                u = u_ref[bi, :, j, :]
                o_ref[bi, :, j, :] = gate_ref[bi, :, j, :] * (y[:, jj * c:(jj + 1) * c] + u * skip)


def _fft_step_a_inv(b_re, b_im, gs, u4, u_col, gate4, gate_col, skip, passes):
    p, n1, _, c = b_re.shape
    n1h = n1 // 2
    jb = 32
    grp = max(1, 1024 // c)
    bspec = pl.BlockSpec((None, n1, jb, c), lambda i, j: (i, 0, j, 0))
    return pl.pallas_call(
        functools.partial(_fai_kernel, grp=grp, passes=passes),
        out_shape=jax.ShapeDtypeStruct((2 * p, n1h, FFT_N2, c), f32),
        grid=(p, FFT_N2 // jb),
        in_specs=[
            bspec, bspec,
            pl.BlockSpec(gs.shape, lambda i, j: (0, 0)),
            pl.BlockSpec((2, n1h, jb, c), lambda i, j: (i, 0, j, u_col)),
            pl.BlockSpec((2, n1h, jb, c), lambda i, j: (i, 0, j, gate_col)),
            pl.BlockSpec((1, c), lambda i, j: (0, 0)),
        ],
        out_specs=pl.BlockSpec((2, n1h, jb, c), lambda i, j: (i, 0, j, 0)),
        compiler_params=_cparams(("parallel", "parallel")),
        name="fft_step_a_inv",
    )(b_re, b_im, gs, u4, gate4, skip)


def _hyena_long(xz, filt, skip, n_seq, L, passes=FFT_PASSES):
    n = 2 * L
    n1, f_n1, f_n2, tw_re, tw_im = _dft_consts(n)
    n1h = n1 // 2
    c = HY_W
    fs_full = _stack_dft(f_n1)
    fs_half = _stack_dft(f_n1[:, :n1h])
    gs = _stack_dft(np.conj(f_n1)[:n1h, :])
    f2s = _stack_dft(f_n2)
    kc4 = _circular_filter(filt).reshape(1, n1, FFT_N2, 2 * c)
    k_re, k_im = _fft_step_a(kc4, fs_full, 0, 2 * c, True, passes)
    kf = _fft_mid(k_re.reshape(1, n, 2 * c), k_im.reshape(1, n, 2 * c), tw_re, tw_im, f2s, passes=passes)
    kf = (kf[0].reshape(n, 2 * c), kf[1].reshape(n, 2 * c))
    xz4 = xz.reshape(n_seq, n1h, FFT_N2, 3 * c)
    z4, z_col = xz4, 2
    for order in range(2):
        a_re, a_im = _fft_step_a(z4, fs_half, z_col, c, False, passes)
        p = n_seq // 2
        b_re, b_im = _fft_mid(a_re.reshape(p, n, c), a_im.reshape(p, n, c), tw_re, tw_im, f2s,
                              kf=kf, order=order, passes=passes)
        z4 = _fft_step_a_inv(b_re.reshape(p, n1, FFT_N2, c), b_im.reshape(p, n1, FFT_N2, c), gs,
                             z4, z_col, xz4, order, skip[order:order + 1], passes)
        z_col = 0
    return z4.reshape(n_seq * L, c)


def _dense_spec_kernel(k_ref, fs_ref, ore_ref, oim_ref, *, passes):
    n = k_ref.shape[0]
    p = _apply_stack(fs_ref[...], k_ref[...], passes)
    ore_ref[...] = p[:n]
    oim_ref[...] = p[n:]


def _dense_conv_kernel(xz_ref, fs_ref, gs_ref, kr_ref, ki_ref, skip_ref, o_ref, *, passes):
    L = xz_ref.shape[1]
    n = 2 * L
    c = HY_W
    fs, gs = fs_ref[...], gs_ref[...]
    za, zb = xz_ref[0, :, 2 * c:], xz_ref[1, :, 2 * c:]
    for order in range(2):
        p = _apply_stack(fs, za, passes)
        q = _apply_stack(fs, zb, passes)
        sr, si = p[:n] - q[n:], p[n:] + q[:n]
        kr, ki = kr_ref[:, order * c:(order + 1) * c], ki_ref[:, order * c:(order + 1) * c]
        yr, yi = sr * kr - si * ki, sr * ki + si * kr
        p2 = _apply_stack(gs, yr, passes)
        q2 = _apply_stack(gs, yi, passes)
        ya = (p2[:L] - q2[L:]) * (1.0 / n)
        yb = (p2[L:] + q2[:L]) * (1.0 / n)
        skip = skip_ref[order:order + 1, :]
        za = xz_ref[0, :, order * c:(order + 1) * c] * (ya + za * skip)
        zb = xz_ref[1, :, order * c:(order + 1) * c] * (yb + zb * skip)
    o_ref[0] = za
    o_ref[1] = zb


def _hyena_short_seq(xz, filt, skip, n_seq, L, passes=FFT_PASSES):
    n = 2 * L
    c = HY_W
    k = np.arange(n)
    f_n = np.exp(-2j * np.pi * np.outer(k, k) / n)
    fs_full = _stack_dft(f_n)
    fs_half = _stack_dft(f_n[:, :L])
    gs = _stack_dft(np.conj(f_n)[:L, :])
    kc = _circular_filter(filt)
    shape = jax.ShapeDtypeStruct((n, 2 * c), f32)
    k_re, k_im = pl.pallas_call(
        functools.partial(_dense_spec_kernel, passes=passes),
        out_shape=[shape, shape],
        compiler_params=_cparams(None),
        name="dense_filter_spectrum",
    )(kc, fs_full)
    skip_p = jnp.zeros((8, c), f32).at[:2].set(skip)
    full = lambda a: pl.BlockSpec(a.shape, lambda i: (0,) * a.ndim)
    out = pl.pallas_call(
        functools.partial(_dense_conv_kernel, passes=passes),
        out_shape=jax.ShapeDtypeStruct((n_seq, L, c), f32),
        grid=(n_seq // 2,),
        in_specs=[pl.BlockSpec((2, L, 3 * c), lambda i: (i, 0, 0)),
                  full(fs_half), full(gs), full(k_re), full(k_im), full(skip_p)],
        out_specs=pl.BlockSpec((2, L, c), lambda i: (i, 0, 0)),
        compiler_params=_cparams(("parallel",)),
        name="dense_long_conv",
    )(xz.reshape(n_seq, L, 3 * c), fs_half, gs, k_re, k_im, skip_p)
    return out.reshape(n_seq * L, c)


def _rope_tables(T, tm):
    n_rows = T // GRID_W
    row = np.repeat(np.arange(n_rows, dtype=np.float32), GRID_W)
    col = np.tile(np.arange(GRID_W, dtype=np.float32), n_rows)
    axis_dim = HEAD_DIM // 2
    inv_freq = np.float32(ROPE_THETA) ** (-np.arange(0, axis_dim, 2, dtype=np.float32) / np.float32(axis_dim))
    ang = np.concatenate([row[:, None] * inv_freq, col[:, None] * inv_freq], axis=-1).astype(np.float64)
    cos = np.zeros((T + tm, HEAD_PAD), np.float32)
    sin = np.zeros((T + tm, HEAD_PAD), np.float32)
    cos[:T, 0:HEAD_DIM:2] = np.cos(ang)
    cos[:T, 1:HEAD_DIM:2] = np.cos(ang)
    sin[:T, 0:HEAD_DIM:2] = -np.sin(ang)
    sin[:T, 1:HEAD_DIM:2] = np.sin(ang)
    cos[T:, :HEAD_DIM] = 1.0
    return jnp.asarray(cos), jnp.asarray(sin)


def _qkprep_kernel(q_ref, kv_ref, cos_ref, sin_ref, qg_ref, kg_ref, avg_ref, oq_ref, okv_ref):
    cos, sin = cos_ref[...], sin_ref[...]
    avg = avg_ref[...]
    lane = lax.broadcasted_iota(jnp.int32, cos.shape, 1)
    even = (lane % 2) == 0

    def norm_rope(x, gain, scale):
        sq_hi, sq_lo = _split(x * x)
        ms = _dot(sq_hi, avg) + _dot(sq_lo, avg)
        xn = x * lax.rsqrt(ms + QK_EPS) * gain
        swapped = jnp.where(even, pltpu.roll(xn, LANE - 1, axis=1), pltpu.roll(xn, 1, axis=1))
        return ((xn * cos + swapped * sin) * scale).astype(bf16)

    for h in range(N_HEADS):
        sl = slice(h * HEAD_PAD, (h + 1) * HEAD_PAD)
        oq_ref[:, sl] = norm_rope(q_ref[:, sl], qg_ref[...], HEAD_DIM ** -0.5)
    for g in range(N_KV_HEADS):
        sl = slice(g * HEAD_PAD, (g + 1) * HEAD_PAD)
        okv_ref[:, sl] = norm_rope(kv_ref[:, sl], kg_ref[...], 1.0)
    okv_ref[:, KVP_W:] = kv_ref[:, KVP_W:].astype(bf16)


def _qkprep(q, kv, cos, sin, q_gain, k_gain, pos_block, tm):
    n = q.shape[0]
    pad = lambda g: jnp.zeros((1, HEAD_PAD), f32).at[0, :HEAD_DIM].set(g)
    avg = jnp.full((HEAD_PAD, HEAD_PAD), 1.0 / HEAD_DIM, bf16)
    one = lambda shape: pl.BlockSpec(shape, lambda i: (0, 0))
    return pl.pallas_call(
        _qkprep_kernel,
        out_shape=[jax.ShapeDtypeStruct((n, QP_W), bf16), jax.ShapeDtypeStruct((n, 2 * KVP_W), bf16)],
        grid=(n // tm,),
        in_specs=[
            pl.BlockSpec((tm, QP_W), lambda i: (i, 0)),
            pl.BlockSpec((tm, 2 * KVP_W), lambda i: (i, 0)),
            pl.BlockSpec((tm, HEAD_PAD), lambda i: (pos_block(i), 0)),
            pl.BlockSpec((tm, HEAD_PAD), lambda i: (pos_block(i), 0)),
            one((1, HEAD_PAD)), one((1, HEAD_PAD)), one((HEAD_PAD, HEAD_PAD)),
        ],
        out_specs=[pl.BlockSpec((tm, QP_W), lambda i: (i, 0)),
                   pl.BlockSpec((tm, 2 * KVP_W), lambda i: (i, 0))],
        compiler_params=_cparams(("parallel",)),
        name="qk_prep",
    )(q, kv, cos, sin, pad(q_gain), pad(k_gain), avg)


def _attn_kernel(q_ref, kvl_ref, kvc_ref, o_ref, *, nq_lat):
    i = pl.program_id(1)
    dims = (((1,), (1,)), ((), ()))

    def heads(use_lat):
        for h in range(N_HEADS):
            g = h // N_GROUPS
            ks = slice(g * HEAD_PAD, (g + 1) * HEAD_PAD)
            vs = slice(KVP_W + g * HEAD_PAD, KVP_W + (g + 1) * HEAD_PAD)
            q = q_ref[:, h * HEAD_PAD:(h + 1) * HEAD_PAD]
            sc = lax.dot_general(q, kvc_ref[:, ks], dims, preferred_element_type=f32)
            m = jnp.max(sc, axis=-1, keepdims=True)
            if use_lat:
                sl = lax.dot_general(q, kvl_ref[:, ks], dims, preferred_element_type=f32)
                m = jnp.maximum(m, jnp.max(sl, axis=-1, keepdims=True))
                p_l = jnp.exp(sl - m)
                denom = jnp.sum(p_l, axis=-1, keepdims=True)
                acc = _dot(p_l.astype(bf16), kvl_ref[:, vs])
            p_c = jnp.exp(sc - m)
            if use_lat:
                denom = denom + jnp.sum(p_c, axis=-1, keepdims=True)
                acc = acc + _dot(p_c.astype(bf16), kvc_ref[:, vs])
            else:
                denom = jnp.sum(p_c, axis=-1, keepdims=True)
                acc = _dot(p_c.astype(bf16), kvc_ref[:, vs])
            o_ref[:, h * HEAD_PAD:(h + 1) * HEAD_PAD] = (acc / denom).astype(bf16)

    @pl.when(i < nq_lat)
    def _():
        heads(True)

    @pl.when(i == nq_lat)
    def _():
        heads(False)


def _attention(qp, kvp, B, T, Lc):
    n = qp.shape[0]
    tq = Lc
    nq = T // tq
    q_idx = lambda b, i: (jnp.where(i < nq, b * nq + i, B * nq + b), 0)
    return pl.pallas_call(
        functools.partial(_attn_kernel, nq_lat=nq),
        out_shape=jax.ShapeDtypeStruct((n, QP_W), bf16),
        grid=(B, nq + 1),
        in_specs=[
            pl.BlockSpec((tq, QP_W), q_idx),
            pl.BlockSpec((T, 2 * KVP_W), lambda b, i: (b, 0)),
            pl.BlockSpec((Lc, 2 * KVP_W), lambda b, i: (B * nq + b, 0)),
        ],
        out_specs=pl.BlockSpec((tq, QP_W), q_idx),
        compiler_params=_cparams(("parallel", "arbitrary")),
        name="attention",
    )(qp, kvp, kvp)


def _scan_kernel(u_ref, wh_ref, wl_ref, bias_ref, lam_ref, h0_ref, *rest, reverse, final):
    if final:
        hprev_ref, g_ref, o_ref, hend_ref, carry_ref = rest
    else:
        o_ref, hend_ref, carry_ref = rest
    t = pl.program_id(1)

    @pl.when(t == 0)
    def _():
        carry_ref[...] = h0_ref[...]

    u = u_ref[...]
    tt = u.shape[0]
    gates = _dot3(u, wh_ref[...], wl_ref[...]) + bias_ref[...]
    r = jax.nn.sigmoid(gates[:, :RG_W])
    ig = jax.nn.sigmoid(gates[:, RG_W:])
    lam = lam_ref[...]
    softplus = jnp.maximum(-lam, 0.0) + jnp.log1p(jnp.exp(-jnp.abs(lam)))
    log_a = -RG_C * r * softplus
    a = jnp.exp(log_a)
    b = jnp.sqrt(-jnp.tanh(log_a) * (a * a + 1.0)) * (ig * u)
    row = lax.broadcasted_iota(jnp.int32, a.shape, 0)
    s = 1
    while s < tt:
        if reverse:
            a_s, b_s = pltpu.roll(a, tt - s, axis=0), pltpu.roll(b, tt - s, axis=0)
            valid = row < tt - s
        else:
            a_s, b_s = pltpu.roll(a, s, axis=0), pltpu.roll(b, s, axis=0)
            valid = row >= s
        b = a * jnp.where(valid, b_s, 0.0) + b
        a = a * jnp.where(valid, a_s, 1.0)
        s *= 2
    h = a * carry_ref[...] + b
    last = h[0:1, :] if reverse else h[tt - 1:tt, :]
    carry_ref[...] = last
    hend_ref[...] = last
    if final:
        o_ref[...] = ((hprev_ref[...] + h) * jax.nn.gelu(g_ref[...], approximate=True)).astype(o_ref.dtype)
    else:
        o_ref[...] = h


def _rg_scan(u, n_seq, L, w_hi, w_lo, bias, lam, h0, reverse, hprev=None, gate=None, gate_row0=0):
    tt = min(SCAN_TILE, L)
    nt = L // tt
    final = hprev is not None
    tidx = (lambda t: nt - 1 - t) if reverse else (lambda t: t)
    row = lambda b, t: (b * nt + tidx(t), 0)
    one = lambda shape: pl.BlockSpec(shape, lambda b, t: (0, 0))
    in_specs = [pl.BlockSpec((tt, RG_W), row), one(w_hi.shape), one(w_lo.shape), one((1, 2 * RG_W)),
                one((1, RG_W)), pl.BlockSpec((None, 1, RG_W), lambda b, t: (b, 0, 0))]
    args = [u, w_hi, w_lo, bias, lam, h0]
    if final:
        in_specs += [pl.BlockSpec((tt, RG_W), row),
                     pl.BlockSpec((tt, RG_W), lambda b, t: (gate_row0 // tt + b * nt + tidx(t), 1))]
        args += [hprev, gate]
    return pl.pallas_call(
        functools.partial(_scan_kernel, reverse=reverse, final=final),
        out_shape=[jax.ShapeDtypeStruct((n_seq * L, RG_W), bf16 if final else f32),
                   jax.ShapeDtypeStruct((n_seq, 1, RG_W), f32)],
        grid=(n_seq, nt),
        in_specs=in_specs,
        out_specs=[pl.BlockSpec((tt, RG_W), row), pl.BlockSpec((None, 1, RG_W), lambda b, t: (b, 0, 0))],
        scratch_shapes=[pltpu.VMEM((1, RG_W), f32)],
        compiler_params=_cparams(("parallel", "arbitrary")),
        name="rg_scan",
    )(*args)


def _block_diag(w):
    bw = w.shape[-1]
    out = jnp.zeros((RG_W, RG_W), f32)
    for h in range(RG_BLOCKS):
        out = out.at[h * bw:(h + 1) * bw, h * bw:(h + 1) * bw].set(w[h])
    return out


def _rglru(rg, B, T, Lc, conv_w, conv_b, lam, w_a, b_a, w_x, b_x):
    bt = B * T
    u_l = _dwconv(rg, conv_w, conv_b, T, 0, B, 0, 1)
    u_c = _dwconv(rg, conv_w, conv_b, Lc, bt // Lc, B, 0, 1)
    zeros = jnp.zeros((B, 1, RG_W), f32)
    prev_l = prev_c = None
    for d, rev in enumerate((False, True)):
        w_hi, w_lo = _split(jnp.concatenate([_block_diag(w_a[d]), _block_diag(w_x[d])], axis=1))
        bias = jnp.concatenate([b_a[d], b_x[d]])[None, :]
        lam_d = lam[d][None, :]
        last = d == 1
        kw_c = dict(hprev=prev_c, gate=rg, gate_row0=bt) if last else {}
        kw_l = dict(hprev=prev_l, gate=rg, gate_row0=0) if last else {}
        prev_c, h_end = _rg_scan(u_c, B, Lc, w_hi, w_lo, bias, lam_d, zeros, rev, **kw_c)
        prev_l, _ = _rg_scan(u_l, B, T, w_hi, w_lo, bias, lam_d, h_end, rev, **kw_l)
    return jnp.concatenate([prev_l, prev_c], axis=0)


def _layer_norm(v, g, b):
    mu = jnp.mean(v, axis=-1, keepdims=True)
    d = v - mu
    var = jnp.mean(d * d, axis=-1, keepdims=True)
    return d * lax.rsqrt(var + LN_EPS) * g + b


def _outproj_kernel(hy_ref, att_ref, rg_ref, h_ref, g1_ref, sh2_ref, sc2_ref, lng_ref, lnb_ref,
                    why_ref, watt_ref, wrg_ref, wrh_ref, wrl_ref, br_ref,
                    oh_ref, om_ref, og_ref, *, alpha):
    y = (_dot(hy_ref[...].astype(bf16), why_ref[...]) + _dot(att_ref[...], watt_ref[...])
         + _dot(rg_ref[...], wrg_ref[...]))
    hn = _layer_norm(alpha * h_ref[...] + g1_ref[...] * y, lng_ref[...], lnb_ref[...])
    oh_ref[...] = hn
    m = hn * (1.0 + sc2_ref[...]) + sh2_ref[...]
    om_ref[...] = m.astype(bf16)
    scores = jax.nn.sigmoid(_dot3(m, wrh_ref[...], wrl_ref[...]))
    sel = scores + br_ref[...]
    lane = lax.broadcasted_iota(jnp.int32, sel.shape, 1)
    picked = jnp.zeros_like(scores)
    for _ in range(TOP_K):
        mx = jnp.max(sel, axis=-1, keepdims=True)
        first = jnp.min(jnp.where(sel == mx, lane, LANE), axis=-1, keepdims=True)
        hit = lane == first
        picked = jnp.where(hit, scores, picked)
        sel = jnp.where(hit, -jnp.inf, sel)
    og_ref[...] = picked / jnp.sum(picked, axis=-1, keepdims=True) * ROUTED_SCALE


def _outproj(hy, att, rg, h, mods3, ln_g, ln_b, w_hy, w_att, w_rg, wr_hi, wr_lo, b_router, seg, tm, alpha):
    n = h.shape[0]
    row = lambda w: pl.BlockSpec((tm, w), lambda i: (i, 0))
    one = lambda a: pl.BlockSpec(a.shape, lambda i: (0,) * a.ndim)
    return pl.pallas_call(
        functools.partial(_outproj_kernel, alpha=alpha),
        out_shape=[jax.ShapeDtypeStruct((n, D_MODEL), f32), jax.ShapeDtypeStruct((n, D_MODEL), bf16),
                   jax.ShapeDtypeStruct((n, LANE), f32)],
        grid=(n // tm,),
        in_specs=[row(HY_W), row(QP_W), row(RG_W), row(D_MODEL),
                  _mod_spec(seg, 2), _mod_spec(seg, 3), _mod_spec(seg, 4),
                  one(ln_g), one(ln_b), one(w_hy), one(w_att), one(w_rg), one(wr_hi), one(wr_lo),
                  one(b_router)],
        out_specs=[row(D_MODEL), row(D_MODEL), row(LANE)],
        compiler_params=_cparams(("parallel",)),
        name="outproj_router",
    )(hy, att, rg, h, mods3, mods3, mods3, ln_g, ln_b, w_hy, w_att, w_rg, wr_hi, wr_lo, b_router)


def _moe_kernel(x_ref, gate_ref, wgu_ref, wd_ref, sgu_ref, sd_ref, h_ref, g2_ref, lng_ref, lnb_ref,
                o_ref, acc_ref, *, eb, alpha):
    j = pl.program_id(1)
    x = x_ref[...]

    def ffn(wgu, wd, scale):
        hgu = _dot(x, wgu)
        hid = jax.nn.silu(hgu[:, :EXPERT_FF]) * hgu[:, EXPERT_FF:]
        if scale is not None:
            hid = hid * scale
        return _dot(hid.astype(bf16), wd)

    @pl.when(j == 0)
    def _():
        acc_ref[...] = ffn(sgu_ref[...], sd_ref[...], None)

    gates = gate_ref[...]
    lane = lax.broadcasted_iota(jnp.int32, gates.shape, 1)
    for e in range(eb):
        col = jnp.sum(jnp.where(lane == j * eb + e, gates, 0.0), axis=-1, keepdims=True)
        acc_ref[...] += ffn(wgu_ref[e], wd_ref[e], col)

    @pl.when(j == pl.num_programs(1) - 1)
    def _():
        o_ref[...] = _layer_norm(alpha * h_ref[...] + g2_ref[...] * acc_ref[...], lng_ref[...], lnb_ref[...])


def _moe(m, gates, wgu, wd, sgu, sd, h, mods3, ln_g, ln_b, seg, tm, alpha):
    n = h.shape[0]
    eb = 4
    one = lambda a: pl.BlockSpec(a.shape, lambda i, j: (0,) * a.ndim)
    return pl.pallas_call(
        functools.partial(_moe_kernel, eb=eb, alpha=alpha),
        out_shape=jax.ShapeDtypeStruct((n, D_MODEL), f32),
        grid=(n // tm, N_EXPERTS // eb),
        in_specs=[
            pl.BlockSpec((tm, D_MODEL), lambda i, j: (i, 0)),
            pl.BlockSpec((tm, LANE), lambda i, j: (i, 0)),
            pl.BlockSpec((eb, D_MODEL, 2 * EXPERT_FF), lambda i, j: (j, 0, 0)),
            pl.BlockSpec((eb, EXPERT_FF, D_MODEL), lambda i, j: (j, 0, 0)),
            one(sgu), one(sd),
            pl.BlockSpec((tm, D_MODEL), lambda i, j: (i, 0)),
            pl.BlockSpec((None, 1, D_MODEL), lambda i, j: (seg(i), 0, 5)),
            one(ln_g), one(ln_b),
        ],
        out_specs=pl.BlockSpec((tm, D_MODEL), lambda i, j: (i, 0)),
        scratch_shapes=[pltpu.VMEM((tm, D_MODEL), f32)],
        compiler_params=_cparams(("parallel", "arbitrary")),
        name="moe",
    )(m, gates, wgu, wd, sgu, sd, h, mods3, ln_g, ln_b)


def _pad_heads_cols(w, n_heads):
    lead = w.shape[:-1]
    w = w.reshape(*lead, n_heads, HEAD_DIM)
    w = jnp.concatenate([w, jnp.zeros_like(w)], axis=-1)
    return w.reshape(*lead, n_heads * HEAD_PAD)


def _pad_in_proj(w_in):
    hy, q, k, v, rgx, rgg = (w_in[..., :HY_END], w_in[..., HY_END:Q_END], w_in[..., Q_END:K_END],
                             w_in[..., K_END:V_END], w_in[..., V_END:RGX_END], w_in[..., RGX_END:])
    return jnp.concatenate([hy, _pad_heads_cols(q, N_HEADS), _pad_heads_cols(k, N_KV_HEADS),
                            _pad_heads_cols(v, N_KV_HEADS), rgx, rgg], axis=-1).astype(bf16)


def kernel(x, c, ctx, c_ctx, w_mod, b_mod, ln1_g, ln1_b, ln2_g, ln2_b, w_in, w_out,
           hy_short_w, hy_short_b, hy_f_w1, hy_f_b1, hy_f_freq, hy_f_w2, hy_f_b2, hy_f_w3, hy_skip,
           q_norm, k_norm, rg_conv_w, rg_conv_b, rg_lambda, rg_w_a, rg_b_a, rg_w_x, rg_b_x,
           w_router, b_router, w_gate, w_up, w_down, ws_gate, ws_up, ws_down):
    B, T, D = x.shape
    Lc = ctx.shape[1]
    depth = w_mod.shape[0]
    bt, bc = B * T, B * Lc
    n_tok = bt + bc
    tm = TOKEN_TILE
    assert D == D_MODEL and T % tm == 0 and bc % tm == 0 and B % 2 == 0
    assert T % ATT_Q_TILE == 0 and Lc == ATT_Q_TILE and T % GRID_W == 0
    alpha = (2 * depth) ** 0.25
    tiles_per_seq = T // tm
    seg = lambda i: jnp.minimum(i // tiles_per_seq, B)
    pos_block = lambda i: jnp.where(i < B * tiles_per_seq, i % tiles_per_seq, tiles_per_seq)
    moe_tm = 1024 if n_tok % 1024 == 0 and T % 1024 == 0 else tm
    moe_seg = lambda i: jnp.minimum(i // (T // moe_tm), B)

    rows = -(-(B + 1) // 8) * 8
    cond = jnp.zeros((rows, D), f32).at[:B].set(c).at[B].set(c_ctx)
    mods = _modulation(cond, w_mod, b_mod)

    cos, sin = _rope_tables(T, tm)
    w_in_pad = _pad_in_proj(w_in)
    w_out_hy = w_out[:, :HY_W].astype(bf16)
    w_out_att = _pad_heads_cols(w_out[:, HY_W:HY_W + ATT_W].transpose(0, 2, 1), N_HEADS).transpose(0, 2, 1).astype(bf16)
    w_out_rg = w_out[:, HY_W + ATT_W:].astype(bf16)
    wr_pad = jnp.zeros((depth, D, LANE), f32).at[:, :, :N_EXPERTS].set(w_router)
    br_pad = jnp.full((depth, 1, LANE), -jnp.inf, f32).at[:, 0, :N_EXPERTS].set(b_router)
    wgu = jnp.concatenate([w_gate, w_up], axis=-1).astype(bf16)
    wd = w_down.astype(bf16)
    sgu = jnp.concatenate([ws_gate, ws_up], axis=-1).astype(bf16)
    sd = ws_down.astype(bf16)

    h = jnp.concatenate([x.reshape(bt, D), ctx.reshape(bc, D)], axis=0)
    for l in range(depth):
        mods3 = mods[l].reshape(rows, 1, 6 * D)
        hy, q, kv, rg = _inproj(h, mods3, w_in_pad[l], seg, tm)

        xz_l = _dwconv(hy, hy_short_w[l], hy_short_b[l], T, 0, B, 0, 3)
        xz_c = _dwconv(hy, hy_short_w[l], hy_short_b[l], Lc, bt // Lc, B, 0, 3)
        fargs = (hy_f_w1[l], hy_f_b1[l], hy_f_freq[l], hy_f_w2[l], hy_f_b2[l], hy_f_w3[l])
        hy_l = _hyena_long(xz_l, _hyena_filters(T, *fargs), hy_skip[l], B, T)
        hy_c = _hyena_short_seq(xz_c, _hyena_filters(Lc, *fargs), hy_skip[l], B, Lc)
        hy_out = jnp.concatenate([hy_l, hy_c], axis=0)

        qp, kvp = _qkprep(q, kv, cos, sin, q_norm[l], k_norm[l], pos_block, tm)
        att = _attention(qp, kvp, B, T, Lc)

        rg_out = _rglru(rg, B, T, Lc, rg_conv_w[l], rg_conv_b[l], rg_lambda[l],
                        rg_w_a[l], rg_b_a[l], rg_w_x[l], rg_b_x[l])

        wr_hi, wr_lo = _split(wr_pad[l])
        h, m, gates = _outproj(hy_out, att, rg_out, h, mods3, ln1_g[l][None, :], ln1_b[l][None, :],
                               w_out_hy[l], w_out_att[l], w_out_rg[l], wr_hi, wr_lo, br_pad[l],
                               seg, tm, alpha)
        h = _moe(m, gates, wgu[l], wd[l], sgu[l], sd[l], h, mods3,
                 ln2_g[l][None, :], ln2_b[l][None, :], moe_seg, moe_tm, alpha)
    return h[:bt].reshape(B, T, D)
```

```python
import functools
import math

import numpy as np
import jax
import jax.numpy as jnp
from jax import lax
from jax.experimental import pallas as pl
from jax.experimental.pallas import tpu as pltpu

f32 = jnp.float32
bf16 = jnp.bfloat16

D_MODEL = 1024
GRID_W = 64
HY_W = 256
N_HEADS = 8
N_KV_HEADS = 2
N_GROUPS = N_HEADS // N_KV_HEADS
HEAD_DIM = 64
ATT_W = N_HEADS * HEAD_DIM
KV_W = N_KV_HEADS * HEAD_DIM
RG_W = 256
RG_BLOCKS = 4
HY_END = 3 * HY_W
Q_END = HY_END + ATT_W
K_END = Q_END + KV_W
V_END = K_END + KV_W
RGX_END = V_END + RG_W
PROJ_W = RGX_END + RG_W
HY_BANDS = 8
HY_EMB = 1 + 2 * HY_BANDS
HY_MAX_DECAY = math.log(1e-2) / 0.3
HY_MIN_DECAY = math.log(1e-2) / 1.5
ROPE_THETA = 10000.0
QK_EPS = 1e-6
RG_C = 8.0
N_EXPERTS = 64
TOP_K = 8
EXPERT_FF = 256
ROUTED_SCALE = 2.5
LN_EPS = 1e-6

LANE = 128
HEAD_PAD = LANE
QP_W = N_HEADS * HEAD_PAD
KVP_W = N_KV_HEADS * HEAD_PAD
PROJ_PAD_W = HY_END + QP_W + 2 * KVP_W + 2 * RG_W
VMEM_LIMIT = 52 * 1024 * 1024

FFT_N2 = 128
FFT_PASSES = 1
FILTER_PASSES = 3
TOKEN_TILE = 512
ATT_Q_TILE = 256
SCAN_TILE = 256
DISPATCH_TILE = 256
RUN_ALIGN = 16
RUN_PIECES = (256, 128, 64, 32, 16)
FFN_TILE = 512


def _cparams(sem):
    return pltpu.CompilerParams(dimension_semantics=sem, vmem_limit_bytes=VMEM_LIMIT)


def _dot(a, b):
    return jnp.dot(a, b, preferred_element_type=f32)


def _split(x):
    hi = x.astype(bf16)
    lo = (x - hi.astype(f32)).astype(bf16)
    return hi, lo


def _np_split(x):
    x = jnp.asarray(np.asarray(x, np.float32))
    return _split(x)


def _dot3(a, w_hi, w_lo):
    a_hi, a_lo = _split(a)
    return _dot(a_hi, w_hi) + _dot(a_lo, w_hi) + _dot(a_hi, w_lo)


def _stack_dft(mat):
    re_hi, re_lo = _np_split(mat.real)
    im_hi, im_lo = _np_split(mat.imag)
    return jnp.concatenate([re_hi, im_hi, re_lo, im_lo], axis=0)


def _apply_stack(stack, x, passes):
    m2 = stack.shape[0] // 2
    x_hi, x_lo = _split(x)
    if passes == 1:
        return _dot(stack[:m2], x_hi)
    r = _dot(stack, x_hi)
    out = r[:m2] + r[m2:]
    if passes >= 3:
        out = out + _dot(stack[:m2], x_lo)
    return out


def _mod_kernel(a_ref, w_ref, b_ref, o_ref):
    a = a_ref[...]
    a = a * jax.nn.sigmoid(a)
    w_hi, w_lo = _split(w_ref[...])
    o_ref[...] = _dot3(a, w_hi, w_lo) + b_ref[...]


def _modulation(cond, w_mod, b_mod):
    depth, d, n = w_mod.shape
    rows = cond.shape[0]
    tn = 512
    return pl.pallas_call(
        _mod_kernel,
        out_shape=jax.ShapeDtypeStruct((depth, rows, n), f32),
        grid=(depth, n // tn),
        in_specs=[
            pl.BlockSpec((rows, d), lambda l, j: (0, 0)),
            pl.BlockSpec((None, d, tn), lambda l, j: (l, 0, j)),
            pl.BlockSpec((None, 1, tn), lambda l, j: (l, 0, j)),
        ],
        out_specs=pl.BlockSpec((None, rows, tn), lambda l, j: (l, 0, j)),
        compiler_params=_cparams(("parallel", "parallel")),
        name="modulation",
    )(cond, w_mod, b_mod.reshape(depth, 1, n))


def _inproj_kernel(h_ref, sh_ref, sc_ref, w_ref, ohy_ref, oq_ref, okv_ref, org_ref):
    a = h_ref[...] * (1.0 + sc_ref[...]) + sh_ref[...]
    p = _dot(a.astype(bf16), w_ref[...])
    c0, c1, c2 = HY_END, HY_END + QP_W, HY_END + QP_W + 2 * KVP_W
    ohy_ref[...] = p[:, :c0]
    oq_ref[...] = p[:, c0:c1]
    okv_ref[...] = p[:, c1:c2]
    org_ref[...] = p[:, c2:]


def _mod_spec(seg, j):
    return pl.BlockSpec((None, 1, D_MODEL), lambda i: (seg(i), 0, j))


def _inproj(h, mods3, w_pad, seg, tm):
    n = h.shape[0]
    widths = (HY_END, QP_W, 2 * KVP_W, 2 * RG_W)
    return pl.pallas_call(
        _inproj_kernel,
        out_shape=[jax.ShapeDtypeStruct((n, w), f32) for w in widths],
        grid=(n // tm,),
        in_specs=[
            pl.BlockSpec((tm, D_MODEL), lambda i: (i, 0)),
            _mod_spec(seg, 0),
            _mod_spec(seg, 1),
            pl.BlockSpec((D_MODEL, PROJ_PAD_W), lambda i: (0, 0)),
        ],
        out_specs=[pl.BlockSpec((tm, w), lambda i: (i, 0)) for w in widths],
        compiler_params=_cparams(("parallel",)),
        name="inproj",
    )(h, mods3, mods3, w_pad)


def _dwconv_kernel(x_ref, w_ref, b_ref, o_ref, *, width, left):
    x = x_ref[...]
    n = x.shape[0]
    row = lax.broadcasted_iota(jnp.int32, x.shape, 0)
    acc = jnp.zeros_like(x) + b_ref[...]
    for j in range(width):
        off = j - left
        if off == 0:
            xs = x
        else:
            xs = pltpu.roll(x, (-off) % n, axis=0)
            valid = jnp.logical_and(row + off >= 0, row + off < n)
            xs = jnp.where(valid, xs, 0.0)
        acc = acc + xs * w_ref[j:j + 1, :]
    o_ref[...] = acc


def _dwconv(x, w, b, seq_len, row_block0, n_seq, col_block0, n_col_blocks):
    width = w.shape[0]
    wp = jnp.zeros((8, w.shape[1]), f32).at[:width].set(w)
    ct = 256
    return pl.pallas_call(
        functools.partial(_dwconv_kernel, width=width, left=(width - 1) // 2),
        out_shape=jax.ShapeDtypeStruct((n_seq * seq_len, n_col_blocks * ct), f32),
        grid=(n_seq, n_col_blocks),
        in_specs=[
            pl.BlockSpec((seq_len, ct), lambda s, c: (row_block0 + s, col_block0 + c)),
            pl.BlockSpec((8, ct), lambda s, c: (0, c)),
            pl.BlockSpec((1, ct), lambda s, c: (0, c)),
        ],
        out_specs=pl.BlockSpec((seq_len, ct), lambda s, c: (s, c)),
        compiler_params=_cparams(("parallel", "parallel")),
        name="dwconv",
    )(x, wp, b.reshape(1, -1))


def _filter_kernel(z_ref, w1h_ref, w1l_ref, b1_ref, fr_ref, w2h_ref, w2l_ref, b2_ref,
                   w3h_ref, w3l_ref, win_ref, o_ref):
    fr = fr_ref[...]
    f = jnp.sin(fr * (_dot3(z_ref[...], w1h_ref[...], w1l_ref[...]) + b1_ref[...]))
    f = jnp.sin(fr * (_dot3(f, w2h_ref[...], w2l_ref[...]) + b2_ref[...]))
    f = _dot3(f, w3h_ref[...], w3l_ref[...])
    win = win_ref[...]
    o_ref[...] = f * jnp.concatenate([win] * 4, axis=1)


def _filter_features(L):
    t = np.linspace(0.0, 1.0, L, dtype=np.float32)[:, None].astype(np.float64)
    w = np.float32(2.0 * math.pi) * np.arange(L, dtype=np.float32)[:, None] / np.float32(L)
    bands = np.linspace(1e-4, HY_BANDS - 1, HY_BANDS, dtype=np.float32)
    arg = (bands * w).astype(np.float64)
    z = np.concatenate([t, np.cos(arg), -np.sin(arg)], axis=-1)
    zp = np.zeros((L, LANE), np.float32)
    zp[:, :HY_EMB] = z
    deltas = np.abs(np.linspace(HY_MIN_DECAY, HY_MAX_DECAY, HY_W, dtype=np.float32)).astype(np.float64)
    win = np.exp(-t * deltas).astype(np.float32)
    return jnp.asarray(zp), jnp.asarray(win)


def _pad2(w, rows, cols):
    return jnp.zeros((rows, cols), f32).at[:w.shape[0], :w.shape[1]].set(w)


def _hyena_filters(L, w1, b1, freq, w2, b2, w3):
    zp, win = _filter_features(L)
    w1h, w1l = _split(_pad2(w1, LANE, LANE))
    w2h, w2l = _split(_pad2(w2, LANE, LANE))
    w3h, w3l = _split(_pad2(w3, LANE, 4 * HY_W))
    b1p = _pad2(b1[None, :], 1, LANE)
    b2p = _pad2(b2[None, :], 1, LANE)
    frp = _pad2(freq[None, :], 1, LANE)
    tl = min(L, 512)
    full = lambda shape: pl.BlockSpec(shape, lambda i: (0, 0))
    return pl.pallas_call(
        _filter_kernel,
        out_shape=jax.ShapeDtypeStruct((L, 4 * HY_W), f32),
        grid=(L // tl,),
        in_specs=[
            pl.BlockSpec((tl, LANE), lambda i: (i, 0)),
            full((LANE, LANE)), full((LANE, LANE)), full((1, LANE)), full((1, LANE)),
            full((LANE, LANE)), full((LANE, LANE)), full((1, LANE)),
            full((LANE, 4 * HY_W)), full((LANE, 4 * HY_W)),
            pl.BlockSpec((tl, HY_W), lambda i: (i, 0)),
        ],
        out_specs=pl.BlockSpec((tl, 4 * HY_W), lambda i: (i, 0)),
        compiler_params=_cparams(("parallel",)),
        name="hyena_filter",
    )(zp, w1h, w1l, b1p, frp, w2h, w2l, b2p, w3h, w3l, win)


def _circular_filter(f):
    half = 2 * HY_W
    fwd, bwd = f[:, :half], f[:, half:]
    return jnp.concatenate([fwd, jnp.zeros((1, half), f32), bwd[:0:-1]], axis=0)


def _dft_consts(n):
    n1 = n // FFT_N2
    k = np.arange(n1)
    f_n1 = np.exp(-2j * np.pi * np.outer(k, k) / n1)
    k2 = np.arange(FFT_N2)
    f_n2 = np.exp(-2j * np.pi * np.outer(k2, k2) / FFT_N2)
    tw = np.exp(-2j * np.pi * np.outer(k, k2) / n).reshape(n, 1)
    tw_re = jnp.asarray(np.broadcast_to(tw.real, (n, LANE)).astype(np.float32))
    tw_im = jnp.asarray(np.broadcast_to(tw.imag, (n, LANE)).astype(np.float32))
    return n1, f_n1, f_n2, tw_re, tw_im


def _fa_kernel(u_ref, fs_ref, are_ref, aim_ref, *, grp, passes, real_only):
    fs = fs_ref[...]
    n1 = fs.shape[0] // 4
    jb, c = u_ref.shape[2], u_ref.shape[3]
    for g in range(jb // grp):
        def gather(bi):
            return jnp.concatenate([u_ref[bi, :, g * grp + jj, :] for jj in range(grp)], axis=1)
        p = _apply_stack(fs, gather(0), passes)
        if real_only:
            re, im = p[:n1], p[n1:]
        else:
            q = _apply_stack(fs, gather(1), passes)
            re, im = p[:n1] - q[n1:], p[n1:] + q[:n1]
        for jj in range(grp):
            are_ref[:, g * grp + jj, :] = re[:, jj * c:(jj + 1) * c]
            aim_ref[:, g * grp + jj, :] = im[:, jj * c:(jj + 1) * c]


def _fft_step_a(u4, fs, col_block, c, real_only, passes):
    s, n1_in = u4.shape[0], u4.shape[1]
    n1 = fs.shape[0] // 4
    per = 1 if real_only else 2
    jb = 32
    grp = max(1, 1024 // c)
    shape = jax.ShapeDtypeStruct((s // per, n1, FFT_N2, c), f32)
    return pl.pallas_call(
        functools.partial(_fa_kernel, grp=grp, passes=passes, real_only=real_only),
        out_shape=[shape, shape],
        grid=(s // per, FFT_N2 // jb),
        in_specs=[
            pl.BlockSpec((per, n1_in, jb, c), lambda p, j: (p, 0, j, col_block)),
            pl.BlockSpec(fs.shape, lambda p, j: (0, 0)),
        ],
        out_specs=[pl.BlockSpec((None, n1, jb, c), lambda p, j: (p, 0, j, 0))] * 2,
        compiler_params=_cparams(("parallel", "parallel")),
        name="fft_step_a",
    )(u4, fs)


def _mid_kernel(are_ref, aim_ref, twr_ref, twi_ref, f2_ref, *rest, kb, passes, fwd_only, inv_n):
    if fwd_only:
        ore_ref, oim_ref = rest
    else:
        kr_ref, ki_ref, ore_ref, oim_ref = rest
    f2 = f2_ref[...]
    c = are_ref.shape[1]
    for kk in range(kb):
        rows = pl.ds(kk * FFT_N2, FFT_N2)
        ar, ai = are_ref[rows, :], aim_ref[rows, :]
        tr = jnp.concatenate([twr_ref[rows, :]] * (c // LANE), axis=1)
        ti = jnp.concatenate([twi_ref[rows, :]] * (c // LANE), axis=1)
        xr = ar * tr - ai * ti
        xi = ar * ti + ai * tr
        p = _apply_stack(f2, xr, passes)
        q = _apply_stack(f2, xi, passes)
        sr = p[:FFT_N2] - q[FFT_N2:]
        si = p[FFT_N2:] + q[:FFT_N2]
        if fwd_only:
            ore_ref[rows, :] = sr
            oim_ref[rows, :] = si
        else:
            kr, ki = kr_ref[rows, :], ki_ref[rows, :]
            yr = sr * kr - si * ki
            yi = sr * ki + si * kr
            p2 = _apply_stack(f2, yr, passes)
            q2 = _apply_stack(f2, yi, passes)
            br = p2[:FFT_N2] + q2[FFT_N2:]
            bi = q2[:FFT_N2] - p2[FFT_N2:]
            ore_ref[rows, :] = (br * tr + bi * ti) * inv_n
            oim_ref[rows, :] = (bi * tr - br * ti) * inv_n


def _fft_mid(a_re, a_im, tw_re, tw_im, f2s, kf=None, order=0, passes=FFT_PASSES):
    p, n, c = a_re.shape
    kb = min(4, n // FFT_N2)
    rb = kb * FFT_N2
    fwd_only = kf is None
    blk = pl.BlockSpec((None, rb, c), lambda i, k: (i, k, 0))
    in_specs = [blk, blk,
                pl.BlockSpec((rb, LANE), lambda i, k: (k, 0)),
                pl.BlockSpec((rb, LANE), lambda i, k: (k, 0)),
                pl.BlockSpec(f2s.shape, lambda i, k: (0, 0))]
    args = [a_re, a_im, tw_re, tw_im, f2s]
    if not fwd_only:
        in_specs += [pl.BlockSpec((rb, c), lambda i, k: (k, order))] * 2
        args += list(kf)
    shape = jax.ShapeDtypeStruct((p, n, c), f32)
    return pl.pallas_call(
        functools.partial(_mid_kernel, kb=kb, passes=passes, fwd_only=fwd_only, inv_n=1.0 / n),
        out_shape=[shape, shape],
        grid=(p, n // rb),
        in_specs=in_specs,
        out_specs=[blk, blk],
        compiler_params=_cparams(("parallel", "parallel")),
        name="fft_mid",
    )(*args)


def _fai_kernel(bre_ref, bim_ref, gs_ref, u_ref, gate_ref, skip_ref, o_ref, *, grp, passes):
    gs = gs_ref[...]
    n1h = gs.shape[0] // 4
    jb, c = u_ref.shape[2], u_ref.shape[3]
    skip = skip_ref[...]
    for g in range(jb // grp):
        def gather(ref):
            return jnp.concatenate([ref[:, g * grp + jj, :] for jj in range(grp)], axis=1)
        p = _apply_stack(gs, gather(bre_ref), passes)
        q = _apply_stack(gs, gather(bim_ref), passes)
        ya = p[:n1h] - q[n1h:]
        yb = p[n1h:] + q[:n1h]
        for jj in range(grp):
            j = g * grp + jj
            for bi, y in ((0, ya), (1, yb)):
                u = u_ref[bi, :, j, :]
                o_ref[bi, :, j, :] = gate_ref[bi, :, j, :] * (y[:, jj * c:(jj + 1) * c] + u * skip)


def _fft_step_a_inv(b_re, b_im, gs, u4, u_col, gate4, gate_col, skip, passes):
    p, n1, _, c = b_re.shape
    n1h = n1 // 2
    jb = 32
    grp = max(1, 1024 // c)
    bspec = pl.BlockSpec((None, n1, jb, c), lambda i, j: (i, 0, j, 0))
    return pl.pallas_call(
        functools.partial(_fai_kernel, grp=grp, passes=passes),
        out_shape=jax.ShapeDtypeStruct((2 * p, n1h, FFT_N2, c), f32),
        grid=(p, FFT_N2 // jb),
        in_specs=[
            bspec, bspec,
            pl.BlockSpec(gs.shape, lambda i, j: (0, 0)),
            pl.BlockSpec((2, n1h, jb, c), lambda i, j: (i, 0, j, u_col)),
            pl.BlockSpec((2, n1h, jb, c), lambda i, j: (i, 0, j, gate_col)),
            pl.BlockSpec((1, c), lambda i, j: (0, 0)),
        ],
        out_specs=pl.BlockSpec((2, n1h, jb, c), lambda i, j: (i, 0, j, 0)),
        compiler_params=_cparams(("parallel", "parallel")),
        name="fft_step_a_inv",
    )(b_re, b_im, gs, u4, gate4, skip)


def _hyena_long(xz, filt, skip, n_seq, L, passes=FFT_PASSES):
    n = 2 * L
    n1, f_n1, f_n2, tw_re, tw_im = _dft_consts(n)
    n1h = n1 // 2
    c = HY_W
    fs_full = _stack_dft(f_n1)
    fs_half = _stack_dft(f_n1[:, :n1h])
    gs = _stack_dft(np.conj(f_n1)[:n1h, :])
    f2s = _stack_dft(f_n2)
    kc4 = _circular_filter(filt).reshape(1, n1, FFT_N2, 2 * c)
    k_re, k_im = _fft_step_a(kc4, fs_full, 0, 2 * c, True, FILTER_PASSES)
    kf = _fft_mid(k_re.reshape(1, n, 2 * c), k_im.reshape(1, n, 2 * c), tw_re, tw_im, f2s,
                  passes=FILTER_PASSES)
    kf = (kf[0].reshape(n, 2 * c), kf[1].reshape(n, 2 * c))
    xz4 = xz.reshape(n_seq, n1h, FFT_N2, 3 * c)
    z4, z_col = xz4, 2
    for order in range(2):
        a_re, a_im = _fft_step_a(z4, fs_half, z_col, c, False, passes)
        p = n_seq // 2
        b_re, b_im = _fft_mid(a_re.reshape(p, n, c), a_im.reshape(p, n, c), tw_re, tw_im, f2s,
                              kf=kf, order=order, passes=passes)
        z4 = _fft_step_a_inv(b_re.reshape(p, n1, FFT_N2, c), b_im.reshape(p, n1, FFT_N2, c), gs,
                             z4, z_col, xz4, order, skip[order:order + 1], passes)
        z_col = 0
    return z4.reshape(n_seq * L, c)


def _dense_spec_kernel(k_ref, fs_ref, ore_ref, oim_ref, *, passes):
    n = k_ref.shape[0]
    p = _apply_stack(fs_ref[...], k_ref[...], passes)
    ore_ref[...] = p[:n]
    oim_ref[...] = p[n:]


def _dense_conv_kernel(xz_ref, fs_ref, gs_ref, kr_ref, ki_ref, skip_ref, o_ref, *, passes):
    L = xz_ref.shape[1]
    n = 2 * L
    c = HY_W
    fs, gs = fs_ref[...], gs_ref[...]
    za, zb = xz_ref[0, :, 2 * c:], xz_ref[1, :, 2 * c:]
    for order in range(2):
        p = _apply_stack(fs, za, passes)
        q = _apply_stack(fs, zb, passes)
        sr, si = p[:n] - q[n:], p[n:] + q[:n]
        kr, ki = kr_ref[:, order * c:(order + 1) * c], ki_ref[:, order * c:(order + 1) * c]
        yr, yi = sr * kr - si * ki, sr * ki + si * kr
        p2 = _apply_stack(gs, yr, passes)
        q2 = _apply_stack(gs, yi, passes)
        ya = (p2[:L] - q2[L:]) * (1.0 / n)
        yb = (p2[L:] + q2[:L]) * (1.0 / n)
        skip = skip_ref[order:order + 1, :]
        za = xz_ref[0, :, order * c:(order + 1) * c] * (ya + za * skip)
        zb = xz_ref[1, :, order * c:(order + 1) * c] * (yb + zb * skip)
    o_ref[0] = za
    o_ref[1] = zb


def _hyena_short_seq(xz, filt, skip, n_seq, L, passes=FFT_PASSES):
    n = 2 * L
    c = HY_W
    k = np.arange(n)
    f_n = np.exp(-2j * np.pi * np.outer(k, k) / n)
    fs_full = _stack_dft(f_n)
    fs_half = _stack_dft(f_n[:, :L])
    gs = _stack_dft(np.conj(f_n)[:L, :])
    kc = _circular_filter(filt)
    shape = jax.ShapeDtypeStruct((n, 2 * c), f32)
    k_re, k_im = pl.pallas_call(
        functools.partial(_dense_spec_kernel, passes=FILTER_PASSES),
        out_shape=[shape, shape],
        compiler_params=_cparams(None),
        name="dense_filter_spectrum",
    )(kc, fs_full)
    skip_p = jnp.zeros((8, c), f32).at[:2].set(skip)
    full = lambda a: pl.BlockSpec(a.shape, lambda i: (0,) * a.ndim)
    out = pl.pallas_call(
        functools.partial(_dense_conv_kernel, passes=passes),
        out_shape=jax.ShapeDtypeStruct((n_seq, L, c), f32),
        grid=(n_seq // 2,),
        in_specs=[pl.BlockSpec((2, L, 3 * c), lambda i: (i, 0, 0)),
                  full(fs_half), full(gs), full(k_re), full(k_im), full(skip_p)],
        out_specs=pl.BlockSpec((2, L, c), lambda i: (i, 0, 0)),
        compiler_params=_cparams(("parallel",)),
        name="dense_long_conv",
    )(xz.reshape(n_seq, L, 3 * c), fs_half, gs, k_re, k_im, skip_p)
    return out.reshape(n_seq * L, c)


def _rope_tables(T, tm):
    n_rows = T // GRID_W
    row = np.repeat(np.arange(n_rows, dtype=np.float32), GRID_W)
    col = np.tile(np.arange(GRID_W, dtype=np.float32), n_rows)
    axis_dim = HEAD_DIM // 2
    inv_freq = np.float32(ROPE_THETA) ** (-np.arange(0, axis_dim, 2, dtype=np.float32) / np.float32(axis_dim))
    ang = np.concatenate([row[:, None] * inv_freq, col[:, None] * inv_freq], axis=-1).astype(np.float64)
    cos = np.zeros((T + tm, HEAD_PAD), np.float32)
    sin = np.zeros((T + tm, HEAD_PAD), np.float32)
    cos[:T, 0:HEAD_DIM:2] = np.cos(ang)
    cos[:T, 1:HEAD_DIM:2] = np.cos(ang)
    sin[:T, 0:HEAD_DIM:2] = -np.sin(ang)
    sin[:T, 1:HEAD_DIM:2] = np.sin(ang)
    cos[T:, :HEAD_DIM] = 1.0
    return jnp.asarray(cos), jnp.asarray(sin)


def _qkprep_kernel(q_ref, kv_ref, cos_ref, sin_ref, qg_ref, kg_ref, avg_ref, oq_ref, okv_ref):
    cos, sin = cos_ref[...], sin_ref[...]
    avg = avg_ref[...]
    lane = lax.broadcasted_iota(jnp.int32, cos.shape, 1)
    even = (lane % 2) == 0

    def norm_rope(x, gain, scale):
        sq_hi, sq_lo = _split(x * x)
        ms = _dot(sq_hi, avg) + _dot(sq_lo, avg)
        xn = x * lax.rsqrt(ms + QK_EPS) * gain
        swapped = jnp.where(even, pltpu.roll(xn, LANE - 1, axis=1), pltpu.roll(xn, 1, axis=1))
        return ((xn * cos + swapped * sin) * scale).astype(bf16)

    for h in range(N_HEADS):
        sl = slice(h * HEAD_PAD, (h + 1) * HEAD_PAD)
        oq_ref[:, sl] = norm_rope(q_ref[:, sl], qg_ref[...], HEAD_DIM ** -0.5)
    for g in range(N_KV_HEADS):
        sl = slice(g * HEAD_PAD, (g + 1) * HEAD_PAD)
        okv_ref[:, sl] = norm_rope(kv_ref[:, sl], kg_ref[...], 1.0)
    okv_ref[:, KVP_W:] = kv_ref[:, KVP_W:].astype(bf16)


def _qkprep(q, kv, cos, sin, q_gain, k_gain, pos_block, tm):
    n = q.shape[0]
    pad = lambda g: jnp.zeros((1, HEAD_PAD), f32).at[0, :HEAD_DIM].set(g)
    avg = jnp.full((HEAD_PAD, HEAD_PAD), 1.0 / HEAD_DIM, bf16)
    one = lambda shape: pl.BlockSpec(shape, lambda i: (0, 0))
    return pl.pallas_call(
        _qkprep_kernel,
        out_shape=[jax.ShapeDtypeStruct((n, QP_W), bf16), jax.ShapeDtypeStruct((n, 2 * KVP_W), bf16)],
        grid=(n // tm,),
        in_specs=[
            pl.BlockSpec((tm, QP_W), lambda i: (i, 0)),
            pl.BlockSpec((tm, 2 * KVP_W), lambda i: (i, 0)),
            pl.BlockSpec((tm, HEAD_PAD), lambda i: (pos_block(i), 0)),
            pl.BlockSpec((tm, HEAD_PAD), lambda i: (pos_block(i), 0)),
            one((1, HEAD_PAD)), one((1, HEAD_PAD)), one((HEAD_PAD, HEAD_PAD)),
        ],
        out_specs=[pl.BlockSpec((tm, QP_W), lambda i: (i, 0)),
                   pl.BlockSpec((tm, 2 * KVP_W), lambda i: (i, 0))],
        compiler_params=_cparams(("parallel",)),
        name="qk_prep",
    )(q, kv, cos, sin, pad(q_gain), pad(k_gain), avg)


def _attn_kernel(q_ref, kvl_ref, kvc_ref, o_ref, *, nq_lat):
    i = pl.program_id(1)
    dims = (((1,), (1,)), ((), ()))

    def scores(h, use_lat):
        g = h // N_GROUPS
        ks = slice(g * HEAD_PAD, (g + 1) * HEAD_PAD)
        q = q_ref[:, h * HEAD_PAD:(h + 1) * HEAD_PAD]
        sc = lax.dot_general(q, kvc_ref[:, ks], dims, preferred_element_type=f32)
        sl = lax.dot_general(q, kvl_ref[:, ks], dims, preferred_element_type=f32) if use_lat else None
        return sl, sc

    def heads(use_lat):
        nxt = scores(0, use_lat)
        for h in range(N_HEADS):
            sl, sc = nxt
            if h + 1 < N_HEADS:
                nxt = scores(h + 1, use_lat)
            g = h // N_GROUPS
            vs = slice(KVP_W + g * HEAD_PAD, KVP_W + (g + 1) * HEAD_PAD)
            m = jnp.max(sc, axis=-1, keepdims=True)
            if use_lat:
                m = jnp.maximum(m, jnp.max(sl, axis=-1, keepdims=True))
                p_l = jnp.exp(sl - m)
                denom = jnp.sum(p_l, axis=-1, keepdims=True)
                acc = _dot(p_l.astype(bf16), kvl_ref[:, vs])
            p_c = jnp.exp(sc - m)
            if use_lat:
                denom = denom + jnp.sum(p_c, axis=-1, keepdims=True)
                acc = acc + _dot(p_c.astype(bf16), kvc_ref[:, vs])
            else:
                denom = jnp.sum(p_c, axis=-1, keepdims=True)
                acc = _dot(p_c.astype(bf16), kvc_ref[:, vs])
            o_ref[:, h * HEAD_PAD:(h + 1) * HEAD_PAD] = (acc / denom).astype(bf16)

    @pl.when(i < nq_lat)
    def _():
        heads(True)

    @pl.when(i == nq_lat)
    def _():
        heads(False)


def _attention(qp, kvp, B, T, Lc):
    n = qp.shape[0]
    tq = Lc
    nq = T // tq
    q_idx = lambda b, i: (jnp.where(i < nq, b * nq + i, B * nq + b), 0)
    return pl.pallas_call(
        functools.partial(_attn_kernel, nq_lat=nq),
        out_shape=jax.ShapeDtypeStruct((n, QP_W), bf16),
        grid=(B, nq + 1),
        in_specs=[
            pl.BlockSpec((tq, QP_W), q_idx),
            pl.BlockSpec((T, 2 * KVP_W), lambda b, i: (b, 0)),
            pl.BlockSpec((Lc, 2 * KVP_W), lambda b, i: (B * nq + b, 0)),
        ],
        out_specs=pl.BlockSpec((tq, QP_W), q_idx),
        compiler_params=_cparams(("parallel", "arbitrary")),
        name="attention",
    )(qp, kvp, kvp)


def _scan_kernel(u_ref, wh_ref, wl_ref, bias_ref, lam_ref, h0_ref, *rest, reverse, final):
    if final:
        hprev_ref, g_ref, o_ref, hend_ref, carry_ref = rest
    else:
        o_ref, hend_ref, carry_ref = rest
    t = pl.program_id(1)

    @pl.when(t == 0)
    def _():
        carry_ref[...] = h0_ref[...]

    u = u_ref[...]
    tt = u.shape[0]
    gates = _dot3(u, wh_ref[...], wl_ref[...]) + bias_ref[...]
    r = jax.nn.sigmoid(gates[:, :RG_W])
    ig = jax.nn.sigmoid(gates[:, RG_W:])
    lam = lam_ref[...]
    softplus = jnp.maximum(-lam, 0.0) + jnp.log1p(jnp.exp(-jnp.abs(lam)))
    log_a = -RG_C * r * softplus
    a = jnp.exp(log_a)
    b = jnp.sqrt(-jnp.tanh(log_a) * (a * a + 1.0)) * (ig * u)
    row = lax.broadcasted_iota(jnp.int32, a.shape, 0)
    s = 1
    while s < tt:
        if reverse:
            a_s, b_s = pltpu.roll(a, tt - s, axis=0), pltpu.roll(b, tt - s, axis=0)
            valid = row < tt - s
        else:
            a_s, b_s = pltpu.roll(a, s, axis=0), pltpu.roll(b, s, axis=0)
            valid = row >= s
        b = a * jnp.where(valid, b_s, 0.0) + b
        a = a * jnp.where(valid, a_s, 1.0)
        s *= 2
    h = a * carry_ref[...] + b
    last = h[0:1, :] if reverse else h[tt - 1:tt, :]
    carry_ref[...] = last
    hend_ref[...] = last
    if final:
        o_ref[...] = ((hprev_ref[...] + h) * jax.nn.gelu(g_ref[...], approximate=True)).astype(o_ref.dtype)
    else:
        o_ref[...] = h


def _rg_scan(u, n_seq, L, w_hi, w_lo, bias, lam, h0, reverse, hprev=None, gate=None, gate_row0=0):
    tt = min(SCAN_TILE, L)
    nt = L // tt
    final = hprev is not None
    tidx = (lambda t: nt - 1 - t) if reverse else (lambda t: t)
    row = lambda b, t: (b * nt + tidx(t), 0)
    one = lambda shape: pl.BlockSpec(shape, lambda b, t: (0, 0))
    in_specs = [pl.BlockSpec((tt, RG_W), row), one(w_hi.shape), one(w_lo.shape), one((1, 2 * RG_W)),
                one((1, RG_W)), pl.BlockSpec((None, 1, RG_W), lambda b, t: (b, 0, 0))]
    args = [u, w_hi, w_lo, bias, lam, h0]
    if final:
        in_specs += [pl.BlockSpec((tt, RG_W), row),
                     pl.BlockSpec((tt, RG_W), lambda b, t: (gate_row0 // tt + b * nt + tidx(t), 1))]
        args += [hprev, gate]
    return pl.pallas_call(
        functools.partial(_scan_kernel, reverse=reverse, final=final),
        out_shape=[jax.ShapeDtypeStruct((n_seq * L, RG_W), bf16 if final else f32),
                   jax.ShapeDtypeStruct((n_seq, 1, RG_W), f32)],
        grid=(n_seq, nt),
        in_specs=in_specs,
        out_specs=[pl.BlockSpec((tt, RG_W), row), pl.BlockSpec((None, 1, RG_W), lambda b, t: (b, 0, 0))],
        scratch_shapes=[pltpu.VMEM((1, RG_W), f32)],
        compiler_params=_cparams(("parallel", "arbitrary")),
        name="rg_scan",
    )(*args)


def _block_diag(w):
    bw = w.shape[-1]
    out = jnp.zeros((RG_W, RG_W), f32)
    for h in range(RG_BLOCKS):
        out = out.at[h * bw:(h + 1) * bw, h * bw:(h + 1) * bw].set(w[h])
    return out


def _rglru(rg, B, T, Lc, conv_w, conv_b, lam, w_a, b_a, w_x, b_x):
    bt = B * T
    u_l = _dwconv(rg, conv_w, conv_b, T, 0, B, 0, 1)
    u_c = _dwconv(rg, conv_w, conv_b, Lc, bt // Lc, B, 0, 1)
    zeros = jnp.zeros((B, 1, RG_W), f32)
    prev_l = prev_c = None
    for d, rev in enumerate((False, True)):
        w_hi, w_lo = _split(jnp.concatenate([_block_diag(w_a[d]), _block_diag(w_x[d])], axis=1))
        bias = jnp.concatenate([b_a[d], b_x[d]])[None, :]
        lam_d = lam[d][None, :]
        last = d == 1
        kw_c = dict(hprev=prev_c, gate=rg, gate_row0=bt) if last else {}
        kw_l = dict(hprev=prev_l, gate=rg, gate_row0=0) if last else {}
        prev_c, h_end = _rg_scan(u_c, B, Lc, w_hi, w_lo, bias, lam_d, zeros, rev, **kw_c)
        prev_l, _ = _rg_scan(u_l, B, T, w_hi, w_lo, bias, lam_d, h_end, rev, **kw_l)
    return jnp.concatenate([prev_l, prev_c], axis=0)


def _layer_norm(v, g, b):
    mu = jnp.mean(v, axis=-1, keepdims=True)
    d = v - mu
    var = jnp.mean(d * d, axis=-1, keepdims=True)
    return d * lax.rsqrt(var + LN_EPS) * g + b


def _outproj_kernel(hy_ref, att_ref, rg_ref, h_ref, g1_ref, sh2_ref, sc2_ref, lng_ref, lnb_ref,
                    why_ref, watt_ref, wrg_ref, wrh_ref, wrl_ref, br_ref,
                    oh_ref, om_ref, og_ref, oi_ref, oc_ref, *, alpha):
    y = (_dot(hy_ref[...].astype(bf16), why_ref[...]) + _dot(att_ref[...], watt_ref[...])
         + _dot(rg_ref[...], wrg_ref[...]))
    hn = _layer_norm(alpha * h_ref[...] + g1_ref[...] * y, lng_ref[...], lnb_ref[...])
    oh_ref[...] = hn
    m = hn * (1.0 + sc2_ref[...]) + sh2_ref[...]
    om_ref[...] = m.astype(bf16)
    scores = jax.nn.sigmoid(_dot3(m, wrh_ref[...], wrl_ref[...]))
    sel = scores + br_ref[...]
    lane = lax.broadcasted_iota(jnp.int32, sel.shape, 1)
    picked = jnp.zeros_like(scores)
    chosen = jnp.zeros(sel.shape, jnp.int32)
    count = jnp.zeros((1, LANE), f32)
    for k in range(TOP_K):
        mx = jnp.max(sel, axis=-1, keepdims=True)
        first = jnp.min(jnp.where(sel == mx, lane, LANE), axis=-1, keepdims=True)
        hit = lane == first
        picked = jnp.where(hit, scores, picked)
        sel = jnp.where(hit, -jnp.inf, sel)
        chosen = jnp.where(lane == k, first, chosen)
        count = count + jnp.sum(jnp.where(hit, 1.0, 0.0), axis=0, keepdims=True)
    og_ref[...] = picked / jnp.sum(picked, axis=-1, keepdims=True) * ROUTED_SCALE
    oi_ref[...] = chosen
    oc_ref[...] = count


def _outproj(hy, att, rg, h, mods3, ln_g, ln_b, w_hy, w_att, w_rg, wr_hi, wr_lo, b_router, seg, tm, alpha):
    n = h.shape[0]
    row = lambda w: pl.BlockSpec((tm, w), lambda i: (i, 0))
    one = lambda a: pl.BlockSpec(a.shape, lambda i: (0,) * a.ndim)
    return pl.pallas_call(
        functools.partial(_outproj_kernel, alpha=alpha),
        out_shape=[jax.ShapeDtypeStruct((n, D_MODEL), f32), jax.ShapeDtypeStruct((n, D_MODEL), bf16),
                   jax.ShapeDtypeStruct((n, LANE), f32), jax.ShapeDtypeStruct((n, LANE), jnp.int32),
                   jax.ShapeDtypeStruct((n // tm, 1, LANE), f32)],
        grid=(n // tm,),
        in_specs=[row(HY_W), row(QP_W), row(RG_W), row(D_MODEL),
                  _mod_spec(seg, 2), _mod_spec(seg, 3), _mod_spec(seg, 4),
                  one(ln_g), one(ln_b), one(w_hy), one(w_att), one(w_rg), one(wr_hi), one(wr_lo),
                  one(b_router)],
        out_specs=[row(D_MODEL), row(D_MODEL), row(LANE), row(LANE),
                   pl.BlockSpec((None, 1, LANE), lambda i: (i, 0, 0))],
        compiler_params=_cparams(("parallel",)),
        name="outproj_router",
    )(hy, att, rg, h, mods3, mods3, mods3, ln_g, ln_b, w_hy, w_att, w_rg, wr_hi, wr_lo, b_router)


def _xs_rows(tile):
    return -(-(TOP_K * tile + N_EXPERTS * (RUN_ALIGN - 1)) // 256) * 256


def _perm_matrix(idx, runstart, gates):
    t = idx.shape[0]
    lane = lax.broadcasted_iota(jnp.int32, (t, LANE), 1)
    hits = [lane == idx[:, k:k + 1] for k in range(TOP_K)]
    sel = jnp.zeros((t, LANE), f32)
    for hit in hits:
        sel = jnp.where(hit, 1.0, sel)
    earlier = (lax.broadcasted_iota(jnp.int32, (t, t), 1) < lax.broadcasted_iota(jnp.int32, (t, t), 0))
    rank = _dot(jnp.where(earlier, 1.0, 0.0).astype(bf16), sel.astype(bf16))
    posmat = rank + runstart
    slot = lax.broadcasted_iota(jnp.int32, (t, _xs_rows(t)), 1)
    acc = jnp.zeros(slot.shape, f32)
    for hit in hits:
        pos = jnp.sum(jnp.where(hit, posmat, 0.0), axis=-1, keepdims=True).astype(jnp.int32)
        val = 1.0 if gates is None else jnp.sum(jnp.where(hit, gates, 0.0), axis=-1, keepdims=True)
        acc = jnp.where(slot == pos, val, acc)
    return acc.astype(bf16)


def _run_pieces(i, rs_ref, len_ref, dst_ref, fn):
    def body(e, carry):
        base = i * N_EXPERTS + e
        rs, ln, dst = rs_ref[base], len_ref[base], dst_ref[base]
        off = jnp.int32(0)
        for sz in RUN_PIECES:
            bit = ln & sz

            @pl.when(bit != 0)
            def _(off=off, sz=sz):
                fn(pl.multiple_of(rs + off, RUN_ALIGN), pl.multiple_of(dst + off, RUN_ALIGN), sz)

            off = off + bit
        return carry

    lax.fori_loop(0, N_EXPERTS, body, 0)


def _dispatch_kernel(rs_ref, len_ref, dst_ref, gap_ref, reg_ref, x_ref, idx_ref, rsf_ref, s_ref,
                     xs_ref, zero_ref, sem):
    i = pl.program_id(0)

    @pl.when(i == 0)
    def _():
        zero_ref[...] = jnp.zeros(zero_ref.shape, zero_ref.dtype)

        def fill(action):
            def body(e, carry):
                @pl.when(reg_ref[e] > 0)
                def _():
                    dst = pl.multiple_of(gap_ref[e], RUN_ALIGN)
                    action(pltpu.make_async_copy(zero_ref, s_ref.at[pl.ds(dst, FFN_TILE)], sem))
                return carry
            lax.fori_loop(0, N_EXPERTS, body, 0)

        fill(lambda cp: cp.start())
        fill(lambda cp: cp.wait())

    p_t = _perm_matrix(idx_ref[...], rsf_ref[...], None)
    xs_ref[...] = lax.dot_general(p_t, x_ref[...], (((0,), (0,)), ((), ())),
                                  preferred_element_type=f32).astype(bf16)

    def copy(src, dst, sz):
        return pltpu.make_async_copy(xs_ref.at[pl.ds(src, sz)], s_ref.at[pl.ds(dst, sz)], sem)

    _run_pieces(i, rs_ref, len_ref, dst_ref, lambda s, d, sz: copy(s, d, sz).start())
    _run_pieces(i, rs_ref, len_ref, dst_ref, lambda s, d, sz: copy(s, d, sz).wait())


def _ffn_kernel(te_ref, nu_ref, x_ref, wgu_ref, wd_ref, o_ref):
    @pl.when(pl.program_id(0) < nu_ref[0])
    def _():
        hgu = _dot(x_ref[...], wgu_ref[...])
        hid = jax.nn.silu(hgu[:, :EXPERT_FF]) * hgu[:, EXPERT_FF:]
        o_ref[...] = _dot(hid.astype(bf16), wd_ref[...]).astype(bf16)


def _combine_kernel(rs_ref, len_ref, dst_ref, z_ref, idx_ref, gate_ref, rsf_ref, x_ref, sgu_ref, sd_ref,
                    h_ref, g2_ref, lng_ref, lnb_ref, o_ref, zbuf_ref, sem, *, alpha):
    i = pl.program_id(0)

    @pl.when(i == 0)
    def _():
        zbuf_ref[...] = jnp.zeros(zbuf_ref.shape, zbuf_ref.dtype)

    def copy(src, dst, sz):
        return pltpu.make_async_copy(z_ref.at[pl.ds(dst, sz)], zbuf_ref.at[pl.ds(src, sz)], sem)

    _run_pieces(i, rs_ref, len_ref, dst_ref, lambda s, d, sz: copy(s, d, sz).start())
    hs = _dot(x_ref[...], sgu_ref[...])
    shared = _dot((jax.nn.silu(hs[:, :EXPERT_FF]) * hs[:, EXPERT_FF:]).astype(bf16), sd_ref[...])
    w_t = _perm_matrix(idx_ref[...], rsf_ref[...], gate_ref[...])
    _run_pieces(i, rs_ref, len_ref, dst_ref, lambda s, d, sz: copy(s, d, sz).wait())
    y = _dot(w_t, zbuf_ref[...]) + shared
    o_ref[...] = _layer_norm(alpha * h_ref[...] + g2_ref[...] * y, lng_ref[...], lnb_ref[...])


def _route_plan(counts, n_ffn_tiles):
    c = counts[:, 0, :N_EXPERTS].astype(jnp.int32)
    lens = (c + RUN_ALIGN - 1) // RUN_ALIGN * RUN_ALIGN
    runstart = jnp.cumsum(lens, axis=1) - lens
    region = (jnp.sum(lens, axis=0) + FFN_TILE - 1) // FFN_TILE * FFN_TILE
    region_end = jnp.cumsum(region)
    dest = (region_end - region)[None, :] + jnp.cumsum(lens, axis=0) - lens
    n_used = region_end[-1] // FFN_TILE
    tile_row = jnp.minimum(jnp.arange(n_ffn_tiles, dtype=jnp.int32), n_used - 1) * FFN_TILE
    tile_expert = jnp.minimum(jnp.sum(tile_row[:, None] >= region_end[None, :], axis=1), N_EXPERTS - 1)
    runstart_f = jnp.zeros((c.shape[0], 1, LANE), f32).at[:, 0, :N_EXPERTS].set(runstart.astype(f32))
    return dict(rs=runstart.reshape(-1), ln=lens.reshape(-1), dst=dest.reshape(-1),
                gap=region_end - FFN_TILE, region=region, n_used=n_used.reshape(1),
                tile_expert=tile_expert.astype(jnp.int32), runstart_f=runstart_f)


def _moe(m, gates, idx, counts, wgu, wd, sgu, sd, h, mods3, ln_g, ln_b, seg, alpha):
    n = h.shape[0]
    t = DISPATCH_TILE
    nt = n // t
    xs_rows = _xs_rows(t)
    rows_max = TOP_K * n + nt * N_EXPERTS * (RUN_ALIGN - 1) + N_EXPERTS * (FFN_TILE - RUN_ALIGN)
    n_ffn_tiles = -(-rows_max // FFN_TILE)
    plan = _route_plan(counts, n_ffn_tiles)
    tile = lambda w: pl.BlockSpec((t, w), lambda i, *_: (i, 0))
    rsf_spec = pl.BlockSpec((None, 1, LANE), lambda i, *_: (i, 0, 0))
    one = lambda a: pl.BlockSpec(a.shape, lambda i, *_: (0,) * a.ndim)
    any_spec = pl.BlockSpec(memory_space=pl.ANY)

    sorted_x = pl.pallas_call(
        _dispatch_kernel,
        out_shape=jax.ShapeDtypeStruct((n_ffn_tiles * FFN_TILE, D_MODEL), bf16),
        grid_spec=pltpu.PrefetchScalarGridSpec(
            num_scalar_prefetch=5, grid=(nt,),
            in_specs=[tile(D_MODEL), tile(LANE), rsf_spec],
            out_specs=any_spec,
            scratch_shapes=[pltpu.VMEM((xs_rows, D_MODEL), bf16), pltpu.VMEM((FFN_TILE, D_MODEL), bf16),
                            pltpu.SemaphoreType.DMA]),
        compiler_params=_cparams(("arbitrary",)),
        name="moe_dispatch",
    )(plan["rs"], plan["ln"], plan["dst"], plan["gap"], plan["region"], m, idx, plan["runstart_f"])

    used = lambda i, te, nu: jnp.minimum(i, nu[0] - 1)
    sorted_z = pl.pallas_call(
        _ffn_kernel,
        out_shape=jax.ShapeDtypeStruct((n_ffn_tiles * FFN_TILE, D_MODEL), bf16),
        grid_spec=pltpu.PrefetchScalarGridSpec(
            num_scalar_prefetch=2, grid=(n_ffn_tiles,),
            in_specs=[pl.BlockSpec((FFN_TILE, D_MODEL), lambda i, te, nu: (used(i, te, nu), 0)),
                      pl.BlockSpec((None, D_MODEL, 2 * EXPERT_FF), lambda i, te, nu: (te[i], 0, 0)),
                      pl.BlockSpec((None, EXPERT_FF, D_MODEL), lambda i, te, nu: (te[i], 0, 0))],
            out_specs=pl.BlockSpec((FFN_TILE, D_MODEL), lambda i, te, nu: (used(i, te, nu), 0))),
        compiler_params=_cparams(("arbitrary",)),
        name="moe_ffn",
    )(plan["tile_expert"], plan["n_used"], sorted_x, wgu, wd)

    return pl.pallas_call(
        functools.partial(_combine_kernel, alpha=alpha),
        out_shape=jax.ShapeDtypeStruct((n, D_MODEL), f32),
        grid_spec=pltpu.PrefetchScalarGridSpec(
            num_scalar_prefetch=3, grid=(nt,),
            in_specs=[any_spec, tile(LANE), tile(LANE), rsf_spec, tile(D_MODEL), one(sgu), one(sd),
                      tile(D_MODEL), pl.BlockSpec((None, 1, D_MODEL), lambda i, *_: (seg(i), 0, 5)),
                      one(ln_g), one(ln_b)],
            out_specs=tile(D_MODEL),
            scratch_shapes=[pltpu.VMEM((xs_rows, D_MODEL), bf16), pltpu.SemaphoreType.DMA]),
        compiler_params=_cparams(("arbitrary",)),
        name="moe_combine",
    )(plan["rs"], plan["ln"], plan["dst"], sorted_z, idx, gates, plan["runstart_f"], m, sgu, sd,
      h, mods3, ln_g, ln_b)


def _pad_heads_cols(w, n_heads):
    lead = w.shape[:-1]
    w = w.reshape(*lead, n_heads, HEAD_DIM)
    w = jnp.concatenate([w, jnp.zeros_like(w)], axis=-1)
    return w.reshape(*lead, n_heads * HEAD_PAD)


def _pad_in_proj(w_in):
    hy, q, k, v, rgx, rgg = (w_in[..., :HY_END], w_in[..., HY_END:Q_END], w_in[..., Q_END:K_END],
                             w_in[..., K_END:V_END], w_in[..., V_END:RGX_END], w_in[..., RGX_END:])
    return jnp.concatenate([hy, _pad_heads_cols(q, N_HEADS), _pad_heads_cols(k, N_KV_HEADS),
                            _pad_heads_cols(v, N_KV_HEADS), rgx, rgg], axis=-1).astype(bf16)


def kernel(x, c, ctx, c_ctx, w_mod, b_mod, ln1_g, ln1_b, ln2_g, ln2_b, w_in, w_out,
           hy_short_w, hy_short_b, hy_f_w1, hy_f_b1, hy_f_freq, hy_f_w2, hy_f_b2, hy_f_w3, hy_skip,
           q_norm, k_norm, rg_conv_w, rg_conv_b, rg_lambda, rg_w_a, rg_b_a, rg_w_x, rg_b_x,
           w_router, b_router, w_gate, w_up, w_down, ws_gate, ws_up, ws_down):
    B, T, D = x.shape
    Lc = ctx.shape[1]
    depth = w_mod.shape[0]
    bt, bc = B * T, B * Lc
    n_tok = bt + bc
    tm = TOKEN_TILE
    assert D == D_MODEL and T % tm == 0 and bc % tm == 0 and B % 2 == 0
    assert T % ATT_Q_TILE == 0 and Lc == ATT_Q_TILE and T % GRID_W == 0
    alpha = (2 * depth) ** 0.25
    tiles_per_seq = T // tm
    seg = lambda i: jnp.minimum(i // tiles_per_seq, B)
    pos_block = lambda i: jnp.where(i < B * tiles_per_seq, i % tiles_per_seq, tiles_per_seq)
    moe_tm = DISPATCH_TILE
    assert T % moe_tm == 0 and bc % moe_tm == 0
    moe_seg = lambda i: jnp.minimum(i // (T // moe_tm), B)

    rows = -(-(B + 1) // 8) * 8
    cond = jnp.zeros((rows, D), f32).at[:B].set(c).at[B].set(c_ctx)
    mods = _modulation(cond, w_mod, b_mod)

    cos, sin = _rope_tables(T, tm)
    w_in_pad = _pad_in_proj(w_in)
    w_out_hy = w_out[:, :HY_W].astype(bf16)
    w_out_att = _pad_heads_cols(w_out[:, HY_W:HY_W + ATT_W].transpose(0, 2, 1), N_HEADS).transpose(0, 2, 1).astype(bf16)
    w_out_rg = w_out[:, HY_W + ATT_W:].astype(bf16)
    wr_pad = jnp.zeros((depth, D, LANE), f32).at[:, :, :N_EXPERTS].set(w_router)
    br_pad = jnp.full((depth, 1, LANE), -jnp.inf, f32).at[:, 0, :N_EXPERTS].set(b_router)
    wgu = jnp.concatenate([w_gate, w_up], axis=-1).astype(bf16)
    wd = w_down.astype(bf16)
    sgu = jnp.concatenate([ws_gate, ws_up], axis=-1).astype(bf16)
    sd = ws_down.astype(bf16)

    h = jnp.concatenate([x.reshape(bt, D), ctx.reshape(bc, D)], axis=0)
    for l in range(depth):
        mods3 = mods[l].reshape(rows, 1, 6 * D)
        hy, q, kv, rg = _inproj(h, mods3, w_in_pad[l], seg, tm)

        xz_l = _dwconv(hy, hy_short_w[l], hy_short_b[l], T, 0, B, 0, 3)
        xz_c = _dwconv(hy, hy_short_w[l], hy_short_b[l], Lc, bt // Lc, B, 0, 3)
        fargs = (hy_f_w1[l], hy_f_b1[l], hy_f_freq[l], hy_f_w2[l], hy_f_b2[l], hy_f_w3[l])
        hy_l = _hyena_long(xz_l, _hyena_filters(T, *fargs), hy_skip[l], B, T)
        hy_c = _hyena_short_seq(xz_c, _hyena_filters(Lc, *fargs), hy_skip[l], B, Lc)
        hy_out = jnp.concatenate([hy_l, hy_c], axis=0)

        qp, kvp = _qkprep(q, kv, cos, sin, q_norm[l], k_norm[l], pos_block, tm)
        att = _attention(qp, kvp, B, T, Lc)

        rg_out = _rglru(rg, B, T, Lc, rg_conv_w[l], rg_conv_b[l], rg_lambda[l],
                        rg_w_a[l], rg_b_a[l], rg_w_x[l], rg_b_x[l])

        wr_hi, wr_lo = _split(wr_pad[l])
        h, m, gates, chosen, counts = _outproj(
            hy_out, att, rg_out, h, mods3, ln1_g[l][None, :], ln1_b[l][None, :],
            w_out_hy[l], w_out_att[l], w_out_rg[l], wr_hi, wr_lo, br_pad[l], moe_seg, moe_tm, alpha)
        h = _moe(m, gates, chosen, counts, wgu[l], wd[l], sgu[l], sd[l], h, mods3,
                 ln2_g[l][None, :], ln2_b[l][None, :], moe_seg, alpha)
    return h[:bt].reshape(B, T, D)
```

```python
import functools
import math

import numpy as np
import jax
import jax.numpy as jnp
from jax import lax
from jax.experimental import pallas as pl
from jax.experimental.pallas import tpu as pltpu

f32 = jnp.float32
bf16 = jnp.bfloat16

D_MODEL = 1024
GRID_W = 64
HY_W = 256
N_HEADS = 8
N_KV_HEADS = 2
N_GROUPS = N_HEADS // N_KV_HEADS
HEAD_DIM = 64
ATT_W = N_HEADS * HEAD_DIM
KV_W = N_KV_HEADS * HEAD_DIM
RG_W = 256
RG_BLOCKS = 4
HY_END = 3 * HY_W
Q_END = HY_END + ATT_W
K_END = Q_END + KV_W
V_END = K_END + KV_W
RGX_END = V_END + RG_W
PROJ_W = RGX_END + RG_W
HY_BANDS = 8
HY_EMB = 1 + 2 * HY_BANDS
HY_MAX_DECAY = math.log(1e-2) / 0.3
HY_MIN_DECAY = math.log(1e-2) / 1.5
ROPE_THETA = 10000.0
QK_EPS = 1e-6
RG_C = 8.0
N_EXPERTS = 64
TOP_K = 8
EXPERT_FF = 256
ROUTED_SCALE = 2.5
LN_EPS = 1e-6

LANE = 128
HEAD_PAD = LANE
QP_W = N_HEADS * HEAD_PAD
KVP_W = N_KV_HEADS * HEAD_PAD
PROJ_PAD_W = HY_END + QP_W + 2 * KVP_W + 2 * RG_W
VMEM_LIMIT = 52 * 1024 * 1024

FFT_N2 = 128
FFT_PASSES = 1
FILTER_PASSES = 3
TOKEN_TILE = 512
ATT_Q_TILE = 256
SCAN_TILE = 256
DISPATCH_TILE = 256
RUN_ALIGN = 16
RUN_CHUNK = 64
PIECE_SIZES = (RUN_CHUNK, 32, 16)
PIECE_LIST = 128
FFN_TILE = 1024


def _cparams(sem):
    return pltpu.CompilerParams(dimension_semantics=sem, vmem_limit_bytes=VMEM_LIMIT)


def _dot(a, b):
    return jnp.dot(a, b, preferred_element_type=f32)


def _split(x):
    hi = x.astype(bf16)
    lo = (x - hi.astype(f32)).astype(bf16)
    return hi, lo


def _np_split(x):
    x = jnp.asarray(np.asarray(x, np.float32))
    return _split(x)


def _dot3(a, w_hi, w_lo):
    a_hi, a_lo = _split(a)
    return _dot(a_hi, w_hi) + _dot(a_lo, w_hi) + _dot(a_hi, w_lo)


def _stack_dft(mat):
    re_hi, re_lo = _np_split(mat.real)
    im_hi, im_lo = _np_split(mat.imag)
    return jnp.concatenate([re_hi, im_hi, re_lo, im_lo], axis=0)


def _apply_stack(stack, x, passes):
    m2 = stack.shape[0] // 2
    x_hi, x_lo = _split(x)
    if passes == 1:
        return _dot(stack[:m2], x_hi)
    r = _dot(stack, x_hi)
    out = r[:m2] + r[m2:]
    if passes >= 3:
        out = out + _dot(stack[:m2], x_lo)
    return out


def _mod_kernel(a_ref, w_ref, b_ref, o_ref):
    a = a_ref[...]
    a = a * jax.nn.sigmoid(a)
    w_hi, w_lo = _split(w_ref[...])
    o_ref[...] = _dot3(a, w_hi, w_lo) + b_ref[...]


def _modulation(cond, w_mod, b_mod):
    depth, d, n = w_mod.shape
    rows = cond.shape[0]
    tn = 512
    return pl.pallas_call(
        _mod_kernel,
        out_shape=jax.ShapeDtypeStruct((depth, rows, n), f32),
        grid=(depth, n // tn),
        in_specs=[
            pl.BlockSpec((rows, d), lambda l, j: (0, 0)),
            pl.BlockSpec((None, d, tn), lambda l, j: (l, 0, j)),
            pl.BlockSpec((None, 1, tn), lambda l, j: (l, 0, j)),
        ],
        out_specs=pl.BlockSpec((None, rows, tn), lambda l, j: (l, 0, j)),
        compiler_params=_cparams(("parallel", "parallel")),
        name="modulation",
    )(cond, w_mod, b_mod.reshape(depth, 1, n))


def _inproj_kernel(h_ref, sh_ref, sc_ref, w_ref, ohy_ref, oq_ref, okv_ref, org_ref):
    a = h_ref[...] * (1.0 + sc_ref[...]) + sh_ref[...]
    p = _dot(a.astype(bf16), w_ref[...])
    c0, c1, c2 = HY_END, HY_END + QP_W, HY_END + QP_W + 2 * KVP_W
    ohy_ref[...] = p[:, :c0]
    oq_ref[...] = p[:, c0:c1]
    okv_ref[...] = p[:, c1:c2]
    org_ref[...] = p[:, c2:]


def _mod_spec(seg, j):
    return pl.BlockSpec((None, 1, D_MODEL), lambda i: (seg(i), 0, j))


def _inproj(h, mods3, w_pad, seg, tm):
    n = h.shape[0]
    widths = (HY_END, QP_W, 2 * KVP_W, 2 * RG_W)
    return pl.pallas_call(
        _inproj_kernel,
        out_shape=[jax.ShapeDtypeStruct((n, w), f32) for w in widths],
        grid=(n // tm,),
        in_specs=[
            pl.BlockSpec((tm, D_MODEL), lambda i: (i, 0)),
            _mod_spec(seg, 0),
            _mod_spec(seg, 1),
            pl.BlockSpec((D_MODEL, PROJ_PAD_W), lambda i: (0, 0)),
        ],
        out_specs=[pl.BlockSpec((tm, w), lambda i: (i, 0)) for w in widths],
        compiler_params=_cparams(("parallel",)),
        name="inproj",
    )(h, mods3, mods3, w_pad)


def _dwconv_kernel(x_ref, w_ref, b_ref, o_ref, *, width, left):
    x = x_ref[...]
    n = x.shape[0]
    row = lax.broadcasted_iota(jnp.int32, x.shape, 0)
    acc = jnp.zeros_like(x) + b_ref[...]
    for j in range(width):
        off = j - left
        if off == 0:
            xs = x
        else:
            xs = pltpu.roll(x, (-off) % n, axis=0)
            valid = jnp.logical_and(row + off >= 0, row + off < n)
            xs = jnp.where(valid, xs, 0.0)
        acc = acc + xs * w_ref[j:j + 1, :]
    o_ref[...] = acc


def _dwconv(x, w, b, seq_len, row_block0, n_seq, col_block0, n_col_blocks):
    width = w.shape[0]
    wp = jnp.zeros((8, w.shape[1]), f32).at[:width].set(w)
    ct = 256
    return pl.pallas_call(
        functools.partial(_dwconv_kernel, width=width, left=(width - 1) // 2),
        out_shape=jax.ShapeDtypeStruct((n_seq * seq_len, n_col_blocks * ct), f32),
        grid=(n_seq, n_col_blocks),
        in_specs=[
            pl.BlockSpec((seq_len, ct), lambda s, c: (row_block0 + s, col_block0 + c)),
            pl.BlockSpec((8, ct), lambda s, c: (0, c)),
            pl.BlockSpec((1, ct), lambda s, c: (0, c)),
        ],
        out_specs=pl.BlockSpec((seq_len, ct), lambda s, c: (s, c)),
        compiler_params=_cparams(("parallel", "parallel")),
        name="dwconv",
    )(x, wp, b.reshape(1, -1))


def _filter_kernel(z_ref, w1h_ref, w1l_ref, b1_ref, fr_ref, w2h_ref, w2l_ref, b2_ref,
                   w3h_ref, w3l_ref, win_ref, o_ref):
    fr = fr_ref[...]
    f = jnp.sin(fr * (_dot3(z_ref[...], w1h_ref[...], w1l_ref[...]) + b1_ref[...]))
    f = jnp.sin(fr * (_dot3(f, w2h_ref[...], w2l_ref[...]) + b2_ref[...]))
    f = _dot3(f, w3h_ref[...], w3l_ref[...])
    win = win_ref[...]
    o_ref[...] = f * jnp.concatenate([win] * 2, axis=1)


def _filter_features(L):
    t = np.linspace(0.0, 1.0, L, dtype=np.float32)[:, None].astype(np.float64)
    w = np.float32(2.0 * math.pi) * np.arange(L, dtype=np.float32)[:, None] / np.float32(L)
    bands = np.linspace(1e-4, HY_BANDS - 1, HY_BANDS, dtype=np.float32)
    arg = (bands * w).astype(np.float64)
    z = np.concatenate([t, np.cos(arg), -np.sin(arg)], axis=-1)
    deltas = np.abs(np.linspace(HY_MIN_DECAY, HY_MAX_DECAY, HY_W, dtype=np.float32)).astype(np.float64)
    win = np.exp(-t * deltas)
    zp = np.zeros((2 * L, LANE), np.float32)
    zp[:L, :HY_EMB] = z
    zp[L + 1:, :HY_EMB] = z[:0:-1]
    win2 = np.zeros((2 * L, HY_W), np.float32)
    win2[:L] = win
    win2[L + 1:] = win[:0:-1]
    return jnp.asarray(zp), jnp.asarray(win2)


def _pad2(w, rows, cols):
    return jnp.zeros((rows, cols), f32).at[:w.shape[0], :w.shape[1]].set(w)


def _hyena_filters(L, w1, b1, freq, w2, b2, w3):
    zp, win = _filter_features(L)
    w1h, w1l = _split(_pad2(w1, LANE, LANE))
    w2h, w2l = _split(_pad2(w2, LANE, LANE))
    w3h, w3l = _split(_pad2(w3, LANE, 4 * HY_W))
    b1p = _pad2(b1[None, :], 1, LANE)
    b2p = _pad2(b2[None, :], 1, LANE)
    frp = _pad2(freq[None, :], 1, LANE)
    tl = min(L, 512)
    half = L // tl
    full = lambda shape: pl.BlockSpec(shape, lambda i: (0, 0))
    w3_spec = pl.BlockSpec((LANE, 2 * HY_W), lambda i: (0, i // half))
    return pl.pallas_call(
        _filter_kernel,
        out_shape=jax.ShapeDtypeStruct((2 * L, 2 * HY_W), f32),
        grid=(2 * half,),
        in_specs=[
            pl.BlockSpec((tl, LANE), lambda i: (i, 0)),
            full((LANE, LANE)), full((LANE, LANE)), full((1, LANE)), full((1, LANE)),
            full((LANE, LANE)), full((LANE, LANE)), full((1, LANE)),
            w3_spec, w3_spec,
            pl.BlockSpec((tl, HY_W), lambda i: (i, 0)),
        ],
        out_specs=pl.BlockSpec((tl, 2 * HY_W), lambda i: (i, 0)),
        compiler_params=_cparams(("parallel",)),
        name="hyena_filter",
    )(zp, w1h, w1l, b1p, frp, w2h, w2l, b2p, w3h, w3l, win)


def _dft_consts(n):
    n1 = n // FFT_N2
    k = np.arange(n1)
    f_n1 = np.exp(-2j * np.pi * np.outer(k, k) / n1)
    k2 = np.arange(FFT_N2)
    f_n2 = np.exp(-2j * np.pi * np.outer(k2, k2) / FFT_N2)
    tw = np.exp(-2j * np.pi * np.outer(k, k2) / n).reshape(n, 1)
    tw_re = jnp.asarray(np.broadcast_to(tw.real, (n, LANE)).astype(np.float32))
    tw_im = jnp.asarray(np.broadcast_to(tw.imag, (n, LANE)).astype(np.float32))
    return n1, f_n1, f_n2, tw_re, tw_im


def _fa_kernel(u_ref, fs_ref, are_ref, aim_ref, *, grp, passes, real_only):
    fs = fs_ref[...]
    n1 = fs.shape[0] // 4
    jb, c = u_ref.shape[2], u_ref.shape[3]
    for g in range(jb // grp):
        def gather(bi):
            return jnp.concatenate([u_ref[bi, :, g * grp + jj, :] for jj in range(grp)], axis=1)
        p = _apply_stack(fs, gather(0), passes)
        if real_only:
            re, im = p[:n1], p[n1:]
        else:
            q = _apply_stack(fs, gather(1), passes)
            re, im = p[:n1] - q[n1:], p[n1:] + q[:n1]
        for jj in range(grp):
            are_ref[:, g * grp + jj, :] = re[:, jj * c:(jj + 1) * c]
            aim_ref[:, g * grp + jj, :] = im[:, jj * c:(jj + 1) * c]


def _fft_step_a(u4, fs, col_block, c, real_only, passes):
    s, n1_in = u4.shape[0], u4.shape[1]
    n1 = fs.shape[0] // 4
    per = 1 if real_only else 2
    jb = 32
    grp = max(1, 1024 // c)
    shape = jax.ShapeDtypeStruct((s // per, n1, FFT_N2, c), f32)
    return pl.pallas_call(
        functools.partial(_fa_kernel, grp=grp, passes=passes, real_only=real_only),
        out_shape=[shape, shape],
        grid=(s // per, FFT_N2 // jb),
        in_specs=[
            pl.BlockSpec((per, n1_in, jb, c), lambda p, j: (p, 0, j, col_block)),
            pl.BlockSpec(fs.shape, lambda p, j: (0, 0)),
        ],
        out_specs=[pl.BlockSpec((None, n1, jb, c), lambda p, j: (p, 0, j, 0))] * 2,
        compiler_params=_cparams(("parallel", "parallel")),
        name="fft_step_a",
    )(u4, fs)


def _mid_kernel(are_ref, aim_ref, twr_ref, twi_ref, f2_ref, *rest, kb, passes, fwd_only, inv_n):
    if fwd_only:
        ore_ref, oim_ref = rest
    else:
        kr_ref, ki_ref, ore_ref, oim_ref = rest
    f2 = f2_ref[...]
    c = are_ref.shape[1]
    for kk in range(kb):
        rows = pl.ds(kk * FFT_N2, FFT_N2)
        ar, ai = are_ref[rows, :], aim_ref[rows, :]
        tr = jnp.concatenate([twr_ref[rows, :]] * (c // LANE), axis=1)
        ti = jnp.concatenate([twi_ref[rows, :]] * (c // LANE), axis=1)
        xr = ar * tr - ai * ti
        xi = ar * ti + ai * tr
        p = _apply_stack(f2, xr, passes)
        q = _apply_stack(f2, xi, passes)
        sr = p[:FFT_N2] - q[FFT_N2:]
        si = p[FFT_N2:] + q[:FFT_N2]
        if fwd_only:
            ore_ref[rows, :] = sr
            oim_ref[rows, :] = si
        else:
            kr, ki = kr_ref[rows, :], ki_ref[rows, :]
            yr = sr * kr - si * ki
            yi = sr * ki + si * kr
            p2 = _apply_stack(f2, yr, passes)
            q2 = _apply_stack(f2, yi, passes)
            br = p2[:FFT_N2] + q2[FFT_N2:]
            bi = q2[:FFT_N2] - p2[FFT_N2:]
            ore_ref[rows, :] = (br * tr + bi * ti) * inv_n
            oim_ref[rows, :] = (bi * tr - br * ti) * inv_n


def _fft_mid(a_re, a_im, tw_re, tw_im, f2s, kf=None, order=0, passes=FFT_PASSES):
    p, n, c = a_re.shape
    kb = min(4, n // FFT_N2)
    rb = kb * FFT_N2
    fwd_only = kf is None
    blk = pl.BlockSpec((None, rb, c), lambda i, k: (i, k, 0))
    in_specs = [blk, blk,
                pl.BlockSpec((rb, LANE), lambda i, k: (k, 0)),
                pl.BlockSpec((rb, LANE), lambda i, k: (k, 0)),
                pl.BlockSpec(f2s.shape, lambda i, k: (0, 0))]
    args = [a_re, a_im, tw_re, tw_im, f2s]
    if not fwd_only:
        in_specs += [pl.BlockSpec((rb, c), lambda i, k: (k, order))] * 2
        args += list(kf)
    shape = jax.ShapeDtypeStruct((p, n, c), f32)
    return pl.pallas_call(
        functools.partial(_mid_kernel, kb=kb, passes=passes, fwd_only=fwd_only, inv_n=1.0 / n),
        out_shape=[shape, shape],
        grid=(p, n // rb),
        in_specs=in_specs,
        out_specs=[blk, blk],
        compiler_params=_cparams(("parallel", "parallel")),
        name="fft_mid",
    )(*args)


def _fai_kernel(bre_ref, bim_ref, gs_ref, u_ref, gate_ref, skip_ref, o_ref, *, grp, passes):
    gs = gs_ref[...]
    n1h = gs.shape[0] // 4
    jb, c = u_ref.shape[2], u_ref.shape[3]
    skip = skip_ref[...]
    for g in range(jb // grp):
        def gather(ref):
            return jnp.concatenate([ref[:, g * grp + jj, :] for jj in range(grp)], axis=1)
        p = _apply_stack(gs, gather(bre_ref), passes)
        q = _apply_stack(gs, gather(bim_ref), passes)
        ya = p[:n1h] - q[n1h:]
        yb = p[n1h:] + q[:n1h]
        for jj in range(grp):
            j = g * grp + jj
            for bi, y in ((0, ya), (1, yb)):
                u = u_ref[bi, :, j, :]
                o_ref[bi, :, j, :] = gate_ref[bi, :, j, :] * (y[:, jj * c:(jj + 1) * c] + u * skip)


def _fft_step_a_inv(b_re, b_im, gs, u4, u_col, gate4, gate_col, skip, passes):
    p, n1, _, c = b_re.shape
    n1h = n1 // 2
    jb = 32
    grp = max(1, 1024 // c)
    bspec = pl.BlockSpec((None, n1, jb, c), lambda i, j: (i, 0, j, 0))
    return pl.pallas_call(
        functools.partial(_fai_kernel, grp=grp, passes=passes),
        out_shape=jax.ShapeDtypeStruct((2 * p, n1h, FFT_N2, c), f32),
        grid=(p, FFT_N2 // jb),
        in_specs=[
            bspec, bspec,
            pl.BlockSpec(gs.shape, lambda i, j: (0, 0)),
            pl.BlockSpec((2, n1h, jb, c), lambda i, j: (i, 0, j, u_col)),
            pl.BlockSpec((2, n1h, jb, c), lambda i, j: (i, 0, j, gate_col)),
            pl.BlockSpec((1, c), lambda i, j: (0, 0)),
        ],
        out_specs=pl.BlockSpec((2, n1h, jb, c), lambda i, j: (i, 0, j, 0)),
        compiler_params=_cparams(("parallel", "parallel")),
        name="fft_step_a_inv",
    )(b_re, b_im, gs, u4, gate4, skip)


def _hyena_long(xz, filt, skip, n_seq, L, passes=FFT_PASSES):
    n = 2 * L
    n1, f_n1, f_n2, tw_re, tw_im = _dft_consts(n)
    n1h = n1 // 2
    c = HY_W
    fs_full = _stack_dft(f_n1)
    fs_half = _stack_dft(f_n1[:, :n1h])
    gs = _stack_dft(np.conj(f_n1)[:n1h, :])
    f2s = _stack_dft(f_n2)
    kc4 = filt.reshape(1, n1, FFT_N2, 2 * c)
    k_re, k_im = _fft_step_a(kc4, fs_full, 0, 2 * c, True, FILTER_PASSES)
    kf = _fft_mid(k_re.reshape(1, n, 2 * c), k_im.reshape(1, n, 2 * c), tw_re, tw_im, f2s,
                  passes=FILTER_PASSES)
    kf = (kf[0].reshape(n, 2 * c), kf[1].reshape(n, 2 * c))
    xz4 = xz.reshape(n_seq, n1h, FFT_N2, 3 * c)
    z4, z_col = xz4, 2
    for order in range(2):
        a_re, a_im = _fft_step_a(z4, fs_half, z_col, c, False, passes)
        p = n_seq // 2
        b_re, b_im = _fft_mid(a_re.reshape(p, n, c), a_im.reshape(p, n, c), tw_re, tw_im, f2s,
                              kf=kf, order=order, passes=passes)
        z4 = _fft_step_a_inv(b_re.reshape(p, n1, FFT_N2, c), b_im.reshape(p, n1, FFT_N2, c), gs,
                             z4, z_col, xz4, order, skip[order:order + 1], passes)
        z_col = 0
    return z4.reshape(n_seq * L, c)


def _dense_spec_kernel(k_ref, fs_ref, ore_ref, oim_ref, *, passes):
    n = k_ref.shape[0]
    p = _apply_stack(fs_ref[...], k_ref[...], passes)
    ore_ref[...] = p[:n]
    oim_ref[...] = p[n:]


def _dense_conv_kernel(xz_ref, fs_ref, gs_ref, kr_ref, ki_ref, skip_ref, o_ref, *, passes):
    L = xz_ref.shape[1]
    n = 2 * L
    c = HY_W
    fs, gs = fs_ref[...], gs_ref[...]
    za, zb = xz_ref[0, :, 2 * c:], xz_ref[1, :, 2 * c:]
    for order in range(2):
        p = _apply_stack(fs, za, passes)
        q = _apply_stack(fs, zb, passes)
        sr, si = p[:n] - q[n:], p[n:] + q[:n]
        kr, ki = kr_ref[:, order * c:(order + 1) * c], ki_ref[:, order * c:(order + 1) * c]
        yr, yi = sr * kr - si * ki, sr * ki + si * kr
        p2 = _apply_stack(gs, yr, passes)
        q2 = _apply_stack(gs, yi, passes)
        ya = (p2[:L] - q2[L:]) * (1.0 / n)
        yb = (p2[L:] + q2[:L]) * (1.0 / n)
        skip = skip_ref[order:order + 1, :]
        za = xz_ref[0, :, order * c:(order + 1) * c] * (ya + za * skip)
        zb = xz_ref[1, :, order * c:(order + 1) * c] * (yb + zb * skip)
    o_ref[0] = za
    o_ref[1] = zb


def _hyena_short_seq(xz, filt, skip, n_seq, L, passes=FFT_PASSES):
    n = 2 * L
    c = HY_W
    k = np.arange(n)
    f_n = np.exp(-2j * np.pi * np.outer(k, k) / n)
    fs_full = _stack_dft(f_n)
    fs_half = _stack_dft(f_n[:, :L])
    gs = _stack_dft(np.conj(f_n)[:L, :])
    kc = filt
    shape = jax.ShapeDtypeStruct((n, 2 * c), f32)
    k_re, k_im = pl.pallas_call(
        functools.partial(_dense_spec_kernel, passes=FILTER_PASSES),
        out_shape=[shape, shape],
        compiler_params=_cparams(None),
        name="dense_filter_spectrum",
    )(kc, fs_full)
    skip_p = jnp.zeros((8, c), f32).at[:2].set(skip)
    full = lambda a: pl.BlockSpec(a.shape, lambda i: (0,) * a.ndim)
    out = pl.pallas_call(
        functools.partial(_dense_conv_kernel, passes=passes),
        out_shape=jax.ShapeDtypeStruct((n_seq, L, c), f32),
        grid=(n_seq // 2,),
        in_specs=[pl.BlockSpec((2, L, 3 * c), lambda i: (i, 0, 0)),
                  full(fs_half), full(gs), full(k_re), full(k_im), full(skip_p)],
        out_specs=pl.BlockSpec((2, L, c), lambda i: (i, 0, 0)),
        compiler_params=_cparams(("parallel",)),
        name="dense_long_conv",
    )(xz.reshape(n_seq, L, 3 * c), fs_half, gs, k_re, k_im, skip_p)
    return out.reshape(n_seq * L, c)


def _rope_tables(T, tm):
    n_rows = T // GRID_W
    row = np.repeat(np.arange(n_rows, dtype=np.float32), GRID_W)
    col = np.tile(np.arange(GRID_W, dtype=np.float32), n_rows)
    axis_dim = HEAD_DIM // 2
    inv_freq = np.float32(ROPE_THETA) ** (-np.arange(0, axis_dim, 2, dtype=np.float32) / np.float32(axis_dim))
    ang = np.concatenate([row[:, None] * inv_freq, col[:, None] * inv_freq], axis=-1).astype(np.float64)
    cos = np.zeros((T + tm, HEAD_PAD), np.float32)
    sin = np.zeros((T + tm, HEAD_PAD), np.float32)
    cos[:T, 0:HEAD_DIM:2] = np.cos(ang)
    cos[:T, 1:HEAD_DIM:2] = np.cos(ang)
    sin[:T, 0:HEAD_DIM:2] = -np.sin(ang)
    sin[:T, 1:HEAD_DIM:2] = np.sin(ang)
    cos[T:, :HEAD_DIM] = 1.0
    return jnp.asarray(cos), jnp.asarray(sin)


def _qkprep_kernel(q_ref, kv_ref, cos_ref, sin_ref, qg_ref, kg_ref, avg_ref, oq_ref, okv_ref):
    cos, sin = cos_ref[...], sin_ref[...]
    avg = avg_ref[...]
    lane = lax.broadcasted_iota(jnp.int32, cos.shape, 1)
    even = (lane % 2) == 0

    def norm_rope(x, gain, scale):
        sq_hi, sq_lo = _split(x * x)
        ms = _dot(sq_hi, avg) + _dot(sq_lo, avg)
        xn = x * lax.rsqrt(ms + QK_EPS) * gain
        swapped = jnp.where(even, pltpu.roll(xn, LANE - 1, axis=1), pltpu.roll(xn, 1, axis=1))
        return ((xn * cos + swapped * sin) * scale).astype(bf16)

    for h in range(N_HEADS):
        sl = slice(h * HEAD_PAD, (h + 1) * HEAD_PAD)
        oq_ref[:, sl] = norm_rope(q_ref[:, sl], qg_ref[...], HEAD_DIM ** -0.5)
    for g in range(N_KV_HEADS):
        sl = slice(g * HEAD_PAD, (g + 1) * HEAD_PAD)
        okv_ref[:, sl] = norm_rope(kv_ref[:, sl], kg_ref[...], 1.0)
    okv_ref[:, KVP_W:] = kv_ref[:, KVP_W:].astype(bf16)


def _qkprep(q, kv, cos, sin, q_gain, k_gain, pos_block, tm):
    n = q.shape[0]
    pad = lambda g: jnp.zeros((1, HEAD_PAD), f32).at[0, :HEAD_DIM].set(g)
    avg = jnp.full((HEAD_PAD, HEAD_PAD), 1.0 / HEAD_DIM, bf16)
    one = lambda shape: pl.BlockSpec(shape, lambda i: (0, 0))
    return pl.pallas_call(
        _qkprep_kernel,
        out_shape=[jax.ShapeDtypeStruct((n, QP_W), bf16), jax.ShapeDtypeStruct((n, 2 * KVP_W), bf16)],
        grid=(n // tm,),
        in_specs=[
            pl.BlockSpec((tm, QP_W), lambda i: (i, 0)),
            pl.BlockSpec((tm, 2 * KVP_W), lambda i: (i, 0)),
            pl.BlockSpec((tm, HEAD_PAD), lambda i: (pos_block(i), 0)),
            pl.BlockSpec((tm, HEAD_PAD), lambda i: (pos_block(i), 0)),
            one((1, HEAD_PAD)), one((1, HEAD_PAD)), one((HEAD_PAD, HEAD_PAD)),
        ],
        out_specs=[pl.BlockSpec((tm, QP_W), lambda i: (i, 0)),
                   pl.BlockSpec((tm, 2 * KVP_W), lambda i: (i, 0))],
        compiler_params=_cparams(("parallel",)),
        name="qk_prep",
    )(q, kv, cos, sin, pad(q_gain), pad(k_gain), avg)


def _attn_kernel(q_ref, kvl_ref, kvc_ref, o_ref, *, nq_lat):
    i = pl.program_id(1)
    dims = (((1,), (1,)), ((), ()))

    def scores(h, use_lat):
        g = h // N_GROUPS
        ks = slice(g * HEAD_PAD, (g + 1) * HEAD_PAD)
        q = q_ref[:, h * HEAD_PAD:(h + 1) * HEAD_PAD]
        sc = lax.dot_general(q, kvc_ref[:, ks], dims, preferred_element_type=f32)
        sl = lax.dot_general(q, kvl_ref[:, ks], dims, preferred_element_type=f32) if use_lat else None
        return sl, sc

    def heads(use_lat):
        nxt = scores(0, use_lat)
        for h in range(N_HEADS):
            sl, sc = nxt
            if h + 1 < N_HEADS:
                nxt = scores(h + 1, use_lat)
            g = h // N_GROUPS
            vs = slice(KVP_W + g * HEAD_PAD, KVP_W + (g + 1) * HEAD_PAD)
            m = jnp.max(sc, axis=-1, keepdims=True)
            if use_lat:
                m = jnp.maximum(m, jnp.max(sl, axis=-1, keepdims=True))
                p_l = jnp.exp(sl - m)
                denom = jnp.sum(p_l, axis=-1, keepdims=True)
                acc = _dot(p_l.astype(bf16), kvl_ref[:, vs])
            p_c = jnp.exp(sc - m)
            if use_lat:
                denom = denom + jnp.sum(p_c, axis=-1, keepdims=True)
                acc = acc + _dot(p_c.astype(bf16), kvc_ref[:, vs])
            else:
                denom = jnp.sum(p_c, axis=-1, keepdims=True)
                acc = _dot(p_c.astype(bf16), kvc_ref[:, vs])
            o_ref[:, h * HEAD_PAD:(h + 1) * HEAD_PAD] = (acc / denom).astype(bf16)

    @pl.when(i < nq_lat)
    def _():
        heads(True)

    @pl.when(i == nq_lat)
    def _():
        heads(False)


def _attention(qp, kvp, B, T, Lc):
    n = qp.shape[0]
    tq = Lc
    nq = T // tq
    q_idx = lambda b, i: (jnp.where(i < nq, b * nq + i, B * nq + b), 0)
    return pl.pallas_call(
        functools.partial(_attn_kernel, nq_lat=nq),
        out_shape=jax.ShapeDtypeStruct((n, QP_W), bf16),
        grid=(B, nq + 1),
        in_specs=[
            pl.BlockSpec((tq, QP_W), q_idx),
            pl.BlockSpec((T, 2 * KVP_W), lambda b, i: (b, 0)),
            pl.BlockSpec((Lc, 2 * KVP_W), lambda b, i: (B * nq + b, 0)),
        ],
        out_specs=pl.BlockSpec((tq, QP_W), q_idx),
        compiler_params=_cparams(("parallel", "arbitrary")),
        name="attention",
    )(qp, kvp, kvp)


def _scan_kernel(u_ref, wh_ref, wl_ref, bias_ref, lam_ref, h0_ref, *rest, reverse, final):
    if final:
        hprev_ref, g_ref, o_ref, hend_ref, carry_ref = rest
    else:
        o_ref, hend_ref, carry_ref = rest
    t = pl.program_id(1)

    @pl.when(t == 0)
    def _():
        carry_ref[...] = h0_ref[...]

    u = u_ref[...]
    tt = u.shape[0]
    gates = _dot3(u, wh_ref[...], wl_ref[...]) + bias_ref[...]
    r = jax.nn.sigmoid(gates[:, :RG_W])
    ig = jax.nn.sigmoid(gates[:, RG_W:])
    lam = lam_ref[...]
    softplus = jnp.maximum(-lam, 0.0) + jnp.log1p(jnp.exp(-jnp.abs(lam)))
    log_a = -RG_C * r * softplus
    a = jnp.exp(log_a)
    b = jnp.sqrt(-jnp.tanh(log_a) * (a * a + 1.0)) * (ig * u)
    row = lax.broadcasted_iota(jnp.int32, a.shape, 0)
    s = 1
    while s < tt:
        if reverse:
            a_s, b_s = pltpu.roll(a, tt - s, axis=0), pltpu.roll(b, tt - s, axis=0)
            valid = row < tt - s
        else:
            a_s, b_s = pltpu.roll(a, s, axis=0), pltpu.roll(b, s, axis=0)
            valid = row >= s
        b = a * jnp.where(valid, b_s, 0.0) + b
        a = a * jnp.where(valid, a_s, 1.0)
        s *= 2
    h = a * carry_ref[...] + b
    last = h[0:1, :] if reverse else h[tt - 1:tt, :]
    carry_ref[...] = last
    hend_ref[...] = last
    if final:
        o_ref[...] = ((hprev_ref[...] + h) * jax.nn.gelu(g_ref[...], approximate=True)).astype(o_ref.dtype)
    else:
        o_ref[...] = h


def _rg_scan(u, n_seq, L, w_hi, w_lo, bias, lam, h0, reverse, hprev=None, gate=None, gate_row0=0):
    tt = min(SCAN_TILE, L)
    nt = L // tt
    final = hprev is not None
    tidx = (lambda t: nt - 1 - t) if reverse else (lambda t: t)
    row = lambda b, t: (b * nt + tidx(t), 0)
    one = lambda shape: pl.BlockSpec(shape, lambda b, t: (0, 0))
    in_specs = [pl.BlockSpec((tt, RG_W), row), one(w_hi.shape), one(w_lo.shape), one((1, 2 * RG_W)),
                one((1, RG_W)), pl.BlockSpec((None, 1, RG_W), lambda b, t: (b, 0, 0))]
    args = [u, w_hi, w_lo, bias, lam, h0]
    if final:
        in_specs += [pl.BlockSpec((tt, RG_W), row),
                     pl.BlockSpec((tt, RG_W), lambda b, t: (gate_row0 // tt + b * nt + tidx(t), 1))]
        args += [hprev, gate]
    return pl.pallas_call(
        functools.partial(_scan_kernel, reverse=reverse, final=final),
        out_shape=[jax.ShapeDtypeStruct((n_seq * L, RG_W), bf16 if final else f32),
                   jax.ShapeDtypeStruct((n_seq, 1, RG_W), f32)],
        grid=(n_seq, nt),
        in_specs=in_specs,
        out_specs=[pl.BlockSpec((tt, RG_W), row), pl.BlockSpec((None, 1, RG_W), lambda b, t: (b, 0, 0))],
        scratch_shapes=[pltpu.VMEM((1, RG_W), f32)],
        compiler_params=_cparams(("parallel", "arbitrary")),
        name="rg_scan",
    )(*args)


def _block_diag(w):
    bw = w.shape[-1]
    out = jnp.zeros((RG_W, RG_W), f32)
    for h in range(RG_BLOCKS):
        out = out.at[h * bw:(h + 1) * bw, h * bw:(h + 1) * bw].set(w[h])
    return out


def _rglru(rg, B, T, Lc, conv_w, conv_b, lam, w_a, b_a, w_x, b_x):
    bt = B * T
    u_l = _dwconv(rg, conv_w, conv_b, T, 0, B, 0, 1)
    u_c = _dwconv(rg, conv_w, conv_b, Lc, bt // Lc, B, 0, 1)
    zeros = jnp.zeros((B, 1, RG_W), f32)
    prev_l = prev_c = None
    for d, rev in enumerate((False, True)):
        w_hi, w_lo = _split(jnp.concatenate([_block_diag(w_a[d]), _block_diag(w_x[d])], axis=1))
        bias = jnp.concatenate([b_a[d], b_x[d]])[None, :]
        lam_d = lam[d][None, :]
        last = d == 1
        kw_c = dict(hprev=prev_c, gate=rg, gate_row0=bt) if last else {}
        kw_l = dict(hprev=prev_l, gate=rg, gate_row0=0) if last else {}
        prev_c, h_end = _rg_scan(u_c, B, Lc, w_hi, w_lo, bias, lam_d, zeros, rev, **kw_c)
        prev_l, _ = _rg_scan(u_l, B, T, w_hi, w_lo, bias, lam_d, h_end, rev, **kw_l)
    return jnp.concatenate([prev_l, prev_c], axis=0)


def _layer_norm(v, g, b):
    mu = jnp.mean(v, axis=-1, keepdims=True)
    d = v - mu
    var = jnp.mean(d * d, axis=-1, keepdims=True)
    return d * lax.rsqrt(var + LN_EPS) * g + b


def _outproj_kernel(hy_ref, att_ref, rg_ref, h_ref, g1_ref, sh2_ref, sc2_ref, lng_ref, lnb_ref,
                    why_ref, watt_ref, wrg_ref, wrh_ref, wrl_ref, br_ref,
                    oh_ref, om_ref, og_ref, oi_ref, oc_ref, *, alpha):
    y = (_dot(hy_ref[...].astype(bf16), why_ref[...]) + _dot(att_ref[...], watt_ref[...])
         + _dot(rg_ref[...], wrg_ref[...]))
    hn = _layer_norm(alpha * h_ref[...] + g1_ref[...] * y, lng_ref[...], lnb_ref[...])
    oh_ref[...] = hn
    m = hn * (1.0 + sc2_ref[...]) + sh2_ref[...]
    om_ref[...] = m.astype(bf16)
    scores = jax.nn.sigmoid(_dot3(m, wrh_ref[...], wrl_ref[...]))
    sel = scores + br_ref[...]
    lane = lax.broadcasted_iota(jnp.int32, sel.shape, 1)
    picked = jnp.zeros_like(scores)
    chosen = jnp.zeros(sel.shape, jnp.int32)
    count = jnp.zeros((1, LANE), f32)
    for k in range(TOP_K):
        mx = jnp.max(sel, axis=-1, keepdims=True)
        first = jnp.min(jnp.where(sel == mx, lane, LANE), axis=-1, keepdims=True)
        hit = lane == first
        picked = jnp.where(hit, scores, picked)
        sel = jnp.where(hit, -jnp.inf, sel)
        chosen = jnp.where(lane == k, first, chosen)
        count = count + jnp.sum(jnp.where(hit, 1.0, 0.0), axis=0, keepdims=True)
    og_ref[...] = picked / jnp.sum(picked, axis=-1, keepdims=True) * ROUTED_SCALE
    oi_ref[...] = chosen
    oc_ref[...] = count


def _outproj(hy, att, rg, h, mods3, ln_g, ln_b, w_hy, w_att, w_rg, wr_hi, wr_lo, b_router, seg, tm, alpha):
    n = h.shape[0]
    row = lambda w: pl.BlockSpec((tm, w), lambda i: (i, 0))
    one = lambda a: pl.BlockSpec(a.shape, lambda i: (0,) * a.ndim)
    return pl.pallas_call(
        functools.partial(_outproj_kernel, alpha=alpha),
        out_shape=[jax.ShapeDtypeStruct((n, D_MODEL), f32), jax.ShapeDtypeStruct((n, D_MODEL), bf16),
                   jax.ShapeDtypeStruct((n, LANE), f32), jax.ShapeDtypeStruct((n, LANE), jnp.int32),
                   jax.ShapeDtypeStruct((n // tm, 1, LANE), f32)],
        grid=(n // tm,),
        in_specs=[row(HY_W), row(QP_W), row(RG_W), row(D_MODEL),
                  _mod_spec(seg, 2), _mod_spec(seg, 3), _mod_spec(seg, 4),
                  one(ln_g), one(ln_b), one(w_hy), one(w_att), one(w_rg), one(wr_hi), one(wr_lo),
                  one(b_router)],
        out_specs=[row(D_MODEL), row(D_MODEL), row(LANE), row(LANE),
                   pl.BlockSpec((None, 1, LANE), lambda i: (i, 0, 0))],
        compiler_params=_cparams(("parallel",)),
        name="outproj_router",
    )(hy, att, rg, h, mods3, mods3, mods3, ln_g, ln_b, w_hy, w_att, w_rg, wr_hi, wr_lo, b_router)


def _xs_rows(tile):
    return -(-(TOP_K * tile + N_EXPERTS * (RUN_ALIGN - 1)) // 256) * 256


def _perm_matrix(idx, runstart, gates):
    t = idx.shape[0]
    lane = lax.broadcasted_iota(jnp.int32, (t, LANE), 1)
    hits = [lane == idx[:, k:k + 1] for k in range(TOP_K)]
    sel = jnp.zeros((t, LANE), f32)
    for hit in hits:
        sel = jnp.where(hit, 1.0, sel)
    earlier = (lax.broadcasted_iota(jnp.int32, (t, t), 1) < lax.broadcasted_iota(jnp.int32, (t, t), 0))
    rank = _dot(jnp.where(earlier, 1.0, 0.0).astype(bf16), sel.astype(bf16))
    posmat = rank + runstart
    slot = lax.broadcasted_iota(jnp.int32, (t, _xs_rows(t)), 1).astype(jnp.int16)
    acc = jnp.zeros(slot.shape, bf16)
    for hit in hits:
        pos = jnp.sum(jnp.where(hit, posmat, 0.0), axis=-1, keepdims=True).astype(jnp.int32).astype(jnp.int16)
        if gates is None:
            val = jnp.ones((t, 1), bf16)
        else:
            val = jnp.sum(jnp.where(hit, gates, 0.0), axis=-1, keepdims=True).astype(bf16)
        acc = jnp.where(slot == pos, val, acc)
    return acc


def _piece_scratch():
    return [pltpu.SMEM((PIECE_LIST,), jnp.int32) for _ in range(2 * len(PIECE_SIZES))]


def _plan_pieces(i, rs_ref, len_ref, dst_ref, lists):
    (src64, dst64), (src32, dst32), (src16, dst16) = lists

    def body(e, counts):
        c64, c32, c16 = counts
        base = i * N_EXPERTS + e
        rs, ln, dst = rs_ref[base], len_ref[base], dst_ref[base]
        n_chunks = ln // RUN_CHUNK
        for j in range(DISPATCH_TILE // RUN_CHUNK):
            src64[c64] = rs + j * RUN_CHUNK
            dst64[c64] = dst + j * RUN_CHUNK
            c64 = c64 + (j < n_chunks).astype(jnp.int32)
        off = n_chunks * RUN_CHUNK
        src32[c32] = rs + off
        dst32[c32] = dst + off
        c32 = c32 + ((ln >> 5) & 1)
        off = off + (ln & 32)
        src16[c16] = rs + off
        dst16[c16] = dst + off
        c16 = c16 + ((ln >> 4) & 1)
        return c64, c32, c16

    zero = jnp.int32(0)
    return lax.fori_loop(0, N_EXPERTS, body, (zero, zero, zero))


def _for_pieces(lists, counts, fn):
    for (src, dst), count, size in zip(lists, counts, PIECE_SIZES):
        def body(j, carry, src=src, dst=dst, size=size):
            fn(pl.multiple_of(src[j], RUN_ALIGN), pl.multiple_of(dst[j], RUN_ALIGN), size)
            return carry

        lax.fori_loop(0, count, body, 0)


def _dispatch_kernel(rs_ref, len_ref, dst_ref, gap_ref, reg_ref, x_ref, idx_ref, rsf_ref, s_ref,
                     xs_ref, zero_ref, sem, *piece_refs):
    i = pl.program_id(0)

    @pl.when(i == 0)
    def _():
        zero_ref[...] = jnp.zeros(zero_ref.shape, zero_ref.dtype)

        def fill(action):
            def body(e, carry):
                @pl.when(reg_ref[e] > 0)
                def _():
                    dst = pl.multiple_of(gap_ref[e], RUN_ALIGN)
                    action(pltpu.make_async_copy(zero_ref, s_ref.at[pl.ds(dst, FFN_TILE)], sem))
                return carry
            lax.fori_loop(0, N_EXPERTS, body, 0)

        fill(lambda cp: cp.start())
        fill(lambda cp: cp.wait())

    p_t = _perm_matrix(idx_ref[...], rsf_ref[...], None)
    xs_ref[...] = lax.dot_general(p_t, x_ref[...], (((0,), (0,)), ((), ())),
                                  preferred_element_type=f32).astype(bf16)

    def copy(src, dst, sz):
        return pltpu.make_async_copy(xs_ref.at[pl.ds(src, sz)], s_ref.at[pl.ds(dst, sz)], sem)

    lists = list(zip(piece_refs[0::2], piece_refs[1::2]))
    counts = _plan_pieces(i, rs_ref, len_ref, dst_ref, lists)
    _for_pieces(lists, counts, lambda s, d, sz: copy(s, d, sz).start())
    _for_pieces(lists, counts, lambda s, d, sz: copy(s, d, sz).wait())


def _ffn_kernel(te_ref, nu_ref, x_ref, wg_ref, wu_ref, wd_ref, o_ref):
    @pl.when(pl.program_id(0) < nu_ref[0])
    def _():
        x = x_ref[...]
        hid = jax.nn.silu(_dot(x, wg_ref[...].astype(bf16))) * _dot(x, wu_ref[...].astype(bf16))
        o_ref[...] = _dot(hid.astype(bf16), wd_ref[...].astype(bf16)).astype(bf16)


def _combine_kernel(rs_ref, len_ref, dst_ref, z_ref, idx_ref, gate_ref, rsf_ref, x_ref, sgu_ref, sd_ref,
                    h_ref, g2_ref, lng_ref, lnb_ref, o_ref, zbuf_ref, sem, *piece_refs, alpha):
    i = pl.program_id(0)

    @pl.when(i == 0)
    def _():
        zbuf_ref[...] = jnp.zeros(zbuf_ref.shape, zbuf_ref.dtype)

    def copy(src, dst, sz):
        return pltpu.make_async_copy(z_ref.at[pl.ds(dst, sz)], zbuf_ref.at[pl.ds(src, sz)], sem)

    lists = list(zip(piece_refs[0::2], piece_refs[1::2]))
    counts = _plan_pieces(i, rs_ref, len_ref, dst_ref, lists)
    _for_pieces(lists, counts, lambda s, d, sz: copy(s, d, sz).start())
    hs = _dot(x_ref[...], sgu_ref[...])
    shared = _dot((jax.nn.silu(hs[:, :EXPERT_FF]) * hs[:, EXPERT_FF:]).astype(bf16), sd_ref[...])
    w_t = _perm_matrix(idx_ref[...], rsf_ref[...], gate_ref[...])
    _for_pieces(lists, counts, lambda s, d, sz: copy(s, d, sz).wait())
    y = _dot(w_t, zbuf_ref[...]) + shared
    o_ref[...] = _layer_norm(alpha * h_ref[...] + g2_ref[...] * y, lng_ref[...], lnb_ref[...])


def _route_plan(counts, n_ffn_tiles):
    c = counts[:, 0, :N_EXPERTS].astype(jnp.int32)
    lens = (c + RUN_ALIGN - 1) // RUN_ALIGN * RUN_ALIGN
    runstart = jnp.cumsum(lens, axis=1) - lens
    region = (jnp.sum(lens, axis=0) + FFN_TILE - 1) // FFN_TILE * FFN_TILE
    region_end = jnp.cumsum(region)
    dest = (region_end - region)[None, :] + jnp.cumsum(lens, axis=0) - lens
    n_used = region_end[-1] // FFN_TILE
    tile_row = jnp.minimum(jnp.arange(n_ffn_tiles, dtype=jnp.int32), n_used - 1) * FFN_TILE
    tile_expert = jnp.minimum(jnp.sum(tile_row[:, None] >= region_end[None, :], axis=1), N_EXPERTS - 1)
    runstart_f = jnp.zeros((c.shape[0], 1, LANE), f32).at[:, 0, :N_EXPERTS].set(runstart.astype(f32))
    return dict(rs=runstart.reshape(-1), ln=lens.reshape(-1), dst=dest.reshape(-1),
                gap=region_end - FFN_TILE, region=region, n_used=n_used.reshape(1),
                tile_expert=tile_expert.astype(jnp.int32), runstart_f=runstart_f)


def _moe(m, gates, idx, counts, layer, w_gate, w_up, w_down, sgu, sd, h, mods3, ln_g, ln_b, seg, alpha):
    n = h.shape[0]
    t = DISPATCH_TILE
    nt = n // t
    xs_rows = _xs_rows(t)
    rows_max = TOP_K * n + nt * N_EXPERTS * (RUN_ALIGN - 1) + N_EXPERTS * (FFN_TILE - RUN_ALIGN)
    n_ffn_tiles = -(-rows_max // FFN_TILE)
    plan = _route_plan(counts, n_ffn_tiles)
    tile = lambda w: pl.BlockSpec((t, w), lambda i, *_: (i, 0))
    rsf_spec = pl.BlockSpec((None, 1, LANE), lambda i, *_: (i, 0, 0))
    one = lambda a: pl.BlockSpec(a.shape, lambda i, *_: (0,) * a.ndim)
    any_spec = pl.BlockSpec(memory_space=pl.ANY)

    sorted_x = pl.pallas_call(
        _dispatch_kernel,
        out_shape=jax.ShapeDtypeStruct((n_ffn_tiles * FFN_TILE, D_MODEL), bf16),
        grid_spec=pltpu.PrefetchScalarGridSpec(
            num_scalar_prefetch=5, grid=(nt,),
            in_specs=[tile(D_MODEL), tile(LANE), rsf_spec],
            out_specs=any_spec,
            scratch_shapes=[pltpu.VMEM((xs_rows, D_MODEL), bf16), pltpu.VMEM((FFN_TILE, D_MODEL), bf16),
                            pltpu.SemaphoreType.DMA] + _piece_scratch()),
        compiler_params=_cparams(("arbitrary",)),
        name="moe_dispatch",
    )(plan["rs"], plan["ln"], plan["dst"], plan["gap"], plan["region"], m, idx, plan["runstart_f"])

    used = lambda i, te, nu: jnp.minimum(i, nu[0] - 1)
    sorted_z = pl.pallas_call(
        _ffn_kernel,
        out_shape=jax.ShapeDtypeStruct((n_ffn_tiles * FFN_TILE, D_MODEL), bf16),
        grid_spec=pltpu.PrefetchScalarGridSpec(
            num_scalar_prefetch=2, grid=(n_ffn_tiles,),
            in_specs=[pl.BlockSpec((FFN_TILE, D_MODEL), lambda i, te, nu: (used(i, te, nu), 0)),
                      pl.BlockSpec((None, None, D_MODEL, EXPERT_FF), lambda i, te, nu: (layer, te[i], 0, 0)),
                      pl.BlockSpec((None, None, D_MODEL, EXPERT_FF), lambda i, te, nu: (layer, te[i], 0, 0)),
                      pl.BlockSpec((None, None, EXPERT_FF, D_MODEL), lambda i, te, nu: (layer, te[i], 0, 0))],
            out_specs=pl.BlockSpec((FFN_TILE, D_MODEL), lambda i, te, nu: (used(i, te, nu), 0))),
        compiler_params=_cparams(("arbitrary",)),
        name="moe_ffn",
    )(plan["tile_expert"], plan["n_used"], sorted_x, w_gate, w_up, w_down)

    return pl.pallas_call(
        functools.partial(_combine_kernel, alpha=alpha),
        out_shape=jax.ShapeDtypeStruct((n, D_MODEL), f32),
        grid_spec=pltpu.PrefetchScalarGridSpec(
            num_scalar_prefetch=3, grid=(nt,),
            in_specs=[any_spec, tile(LANE), tile(LANE), rsf_spec, tile(D_MODEL), one(sgu), one(sd),
                      tile(D_MODEL), pl.BlockSpec((None, 1, D_MODEL), lambda i, *_: (seg(i), 0, 5)),
                      one(ln_g), one(ln_b)],
            out_specs=tile(D_MODEL),
            scratch_shapes=[pltpu.VMEM((xs_rows, D_MODEL), bf16), pltpu.SemaphoreType.DMA]
            + _piece_scratch()),
        compiler_params=_cparams(("arbitrary",)),
        name="moe_combine",
    )(plan["rs"], plan["ln"], plan["dst"], sorted_z, idx, gates, plan["runstart_f"], m, sgu, sd,
      h, mods3, ln_g, ln_b)


def _pad_heads_cols(w, n_heads):
    lead = w.shape[:-1]
    w = w.reshape(*lead, n_heads, HEAD_DIM)
    w = jnp.concatenate([w, jnp.zeros_like(w)], axis=-1)
    return w.reshape(*lead, n_heads * HEAD_PAD)


def _pad_in_proj(w_in):
    hy, q, k, v, rgx, rgg = (w_in[..., :HY_END], w_in[..., HY_END:Q_END], w_in[..., Q_END:K_END],
                             w_in[..., K_END:V_END], w_in[..., V_END:RGX_END], w_in[..., RGX_END:])
    return jnp.concatenate([hy, _pad_heads_cols(q, N_HEADS), _pad_heads_cols(k, N_KV_HEADS),
                            _pad_heads_cols(v, N_KV_HEADS), rgx, rgg], axis=-1).astype(bf16)


def kernel(x, c, ctx, c_ctx, w_mod, b_mod, ln1_g, ln1_b, ln2_g, ln2_b, w_in, w_out,
           hy_short_w, hy_short_b, hy_f_w1, hy_f_b1, hy_f_freq, hy_f_w2, hy_f_b2, hy_f_w3, hy_skip,
           q_norm, k_norm, rg_conv_w, rg_conv_b, rg_lambda, rg_w_a, rg_b_a, rg_w_x, rg_b_x,
           w_router, b_router, w_gate, w_up, w_down, ws_gate, ws_up, ws_down):
    B, T, D = x.shape
    Lc = ctx.shape[1]
    depth = w_mod.shape[0]
    bt, bc = B * T, B * Lc
    n_tok = bt + bc
    tm = TOKEN_TILE
    assert D == D_MODEL and T % tm == 0 and bc % tm == 0 and B % 2 == 0
    assert T % ATT_Q_TILE == 0 and Lc == ATT_Q_TILE and T % GRID_W == 0
    alpha = (2 * depth) ** 0.25
    tiles_per_seq = T // tm
    seg = lambda i: jnp.minimum(i // tiles_per_seq, B)
    pos_block = lambda i: jnp.where(i < B * tiles_per_seq, i % tiles_per_seq, tiles_per_seq)
    moe_tm = DISPATCH_TILE
    assert T % moe_tm == 0 and bc % moe_tm == 0
    moe_seg = lambda i: jnp.minimum(i // (T // moe_tm), B)

    rows = -(-(B + 1) // 8) * 8
    cond = jnp.zeros((rows, D), f32).at[:B].set(c).at[B].set(c_ctx)
    mods = _modulation(cond, w_mod, b_mod)

    cos, sin = _rope_tables(T, tm)
    w_in_pad = _pad_in_proj(w_in)
    w_out_hy = w_out[:, :HY_W].astype(bf16)
    w_out_att = _pad_heads_cols(w_out[:, HY_W:HY_W + ATT_W].transpose(0, 2, 1), N_HEADS).transpose(0, 2, 1).astype(bf16)
    w_out_rg = w_out[:, HY_W + ATT_W:].astype(bf16)
    wr_pad = jnp.zeros((depth, D, LANE), f32).at[:, :, :N_EXPERTS].set(w_router)
    br_pad = jnp.full((depth, 1, LANE), -jnp.inf, f32).at[:, 0, :N_EXPERTS].set(b_router)
    sgu = jnp.concatenate([ws_gate, ws_up], axis=-1).astype(bf16)
    sd = ws_down.astype(bf16)

    h = jnp.concatenate([x.reshape(bt, D), ctx.reshape(bc, D)], axis=0)
    for l in range(depth):
        mods3 = mods[l].reshape(rows, 1, 6 * D)
        hy, q, kv, rg = _inproj(h, mods3, w_in_pad[l], seg, tm)

        xz_l = _dwconv(hy, hy_short_w[l], hy_short_b[l], T, 0, B, 0, 3)
        xz_c = _dwconv(hy, hy_short_w[l], hy_short_b[l], Lc, bt // Lc, B, 0, 3)
        fargs = (hy_f_w1[l], hy_f_b1[l], hy_f_freq[l], hy_f_w2[l], hy_f_b2[l], hy_f_w3[l])
        hy_l = _hyena_long(xz_l, _hyena_filters(T, *fargs), hy_skip[l], B, T)
        hy_c = _hyena_short_seq(xz_c, _hyena_filters(Lc, *fargs), hy_skip[l], B, Lc)
        hy_out = jnp.concatenate([hy_l, hy_c], axis=0)

        qp, kvp = _qkprep(q, kv, cos, sin, q_norm[l], k_norm[l], pos_block, tm)
        att = _attention(qp, kvp, B, T, Lc)

        rg_out = _rglru(rg, B, T, Lc, rg_conv_w[l], rg_conv_b[l], rg_lambda[l],
                        rg_w_a[l], rg_b_a[l], rg_w_x[l], rg_b_x[l])

        wr_hi, wr_lo = _split(wr_pad[l])
        h, m, gates, chosen, counts = _outproj(
            hy_out, att, rg_out, h, mods3, ln1_g[l][None, :], ln1_b[l][None, :],
            w_out_hy[l], w_out_att[l], w_out_rg[l], wr_hi, wr_lo, br_pad[l], moe_seg, moe_tm, alpha)
        h = _moe(m, gates, chosen, counts, l, w_gate, w_up, w_down, sgu[l], sd[l], h, mods3,
                 ln2_g[l][None, :], ln2_b[l][None, :], moe_seg, alpha)
    return h[:bt].reshape(B, T, D)
```

```python
import functools
import math

import numpy as np
import jax
import jax.numpy as jnp
from jax import lax
from jax.experimental import pallas as pl
from jax.experimental.pallas import tpu as pltpu

f32 = jnp.float32
bf16 = jnp.bfloat16

D_MODEL = 1024
GRID_W = 64
HY_W = 256
N_HEADS = 8
N_KV_HEADS = 2
N_GROUPS = N_HEADS // N_KV_HEADS
HEAD_DIM = 64
ATT_W = N_HEADS * HEAD_DIM
KV_W = N_KV_HEADS * HEAD_DIM
RG_W = 256
RG_BLOCKS = 4
HY_END = 3 * HY_W
Q_END = HY_END + ATT_W
K_END = Q_END + KV_W
V_END = K_END + KV_W
RGX_END = V_END + RG_W
PROJ_W = RGX_END + RG_W
HY_BANDS = 8
HY_EMB = 1 + 2 * HY_BANDS
HY_MAX_DECAY = math.log(1e-2) / 0.3
HY_MIN_DECAY = math.log(1e-2) / 1.5
ROPE_THETA = 10000.0
QK_EPS = 1e-6
RG_C = 8.0
N_EXPERTS = 64
TOP_K = 8
EXPERT_FF = 256
ROUTED_SCALE = 2.5
LN_EPS = 1e-6

LANE = 128
HEAD_PAD = LANE
QP_W = N_HEADS * HEAD_PAD
KVP_W = N_KV_HEADS * HEAD_PAD
PROJ_PAD_W = HY_END + QP_W + 2 * KVP_W + 2 * RG_W
VMEM_LIMIT = 52 * 1024 * 1024

FFT_N2 = 128
FFT_PASSES = 1
FILTER_PASSES = 3
TOKEN_TILE = 512
ATT_Q_TILE = 256
SCAN_TILE = 256
DISPATCH_TILE = 256
RUN_ALIGN = 16
RUN_CHUNK = 64
PIECE_SIZES = (RUN_CHUNK, 32, 16)
PLAN_ROWS = 8
PLAN_COUNT_ROW = 6
FFN_TILE = 1024


def _cparams(sem):
    return pltpu.CompilerParams(dimension_semantics=sem, vmem_limit_bytes=VMEM_LIMIT)


def _dot(a, b):
    return jnp.dot(a, b, preferred_element_type=f32)


def _split(x):
    hi = x.astype(bf16)
    lo = (x - hi.astype(f32)).astype(bf16)
    return hi, lo


def _np_split(x):
    x = jnp.asarray(np.asarray(x, np.float32))
    return _split(x)


def _dot3(a, w_hi, w_lo):
    a_hi, a_lo = _split(a)
    return _dot(a_hi, w_hi) + _dot(a_lo, w_hi) + _dot(a_hi, w_lo)


def _stack_dft(mat):
    re_hi, re_lo = _np_split(mat.real)
    im_hi, im_lo = _np_split(mat.imag)
    return jnp.concatenate([re_hi, im_hi, re_lo, im_lo], axis=0)


def _apply_stack(stack, x, passes):
    m2 = stack.shape[0] // 2
    x_hi, x_lo = _split(x)
    if passes == 1:
        return _dot(stack[:m2], x_hi)
    r = _dot(stack, x_hi)
    out = r[:m2] + r[m2:]
    if passes >= 3:
        out = out + _dot(stack[:m2], x_lo)
    return out


def _mod_kernel(a_ref, w_ref, b_ref, o_ref):
    a = a_ref[...]
    a = a * jax.nn.sigmoid(a)
    w_hi, w_lo = _split(w_ref[...])
    o_ref[...] = _dot3(a, w_hi, w_lo) + b_ref[...]


def _modulation(cond, w_mod, b_mod):
    depth, d, n = w_mod.shape
    rows = cond.shape[0]
    tn = 512
    return pl.pallas_call(
        _mod_kernel,
        out_shape=jax.ShapeDtypeStruct((depth, rows, n), f32),
        grid=(depth, n // tn),
        in_specs=[
            pl.BlockSpec((rows, d), lambda l, j: (0, 0)),
            pl.BlockSpec((None, d, tn), lambda l, j: (l, 0, j)),
            pl.BlockSpec((None, 1, tn), lambda l, j: (l, 0, j)),
        ],
        out_specs=pl.BlockSpec((None, rows, tn), lambda l, j: (l, 0, j)),
        compiler_params=_cparams(("parallel", "parallel")),
        name="modulation",
    )(cond, w_mod, b_mod.reshape(depth, 1, n))


def _inproj_kernel(h_ref, sh_ref, sc_ref, w_ref, ohy_ref, oq_ref, okv_ref, org_ref):
    a = h_ref[...] * (1.0 + sc_ref[...]) + sh_ref[...]
    p = _dot(a.astype(bf16), w_ref[...])
    c0, c1, c2 = HY_END, HY_END + QP_W, HY_END + QP_W + 2 * KVP_W
    ohy_ref[...] = p[:, :c0]
    oq_ref[...] = p[:, c0:c1]
    okv_ref[...] = p[:, c1:c2]
    org_ref[...] = p[:, c2:]


def _mod_spec(seg, j):
    return pl.BlockSpec((None, 1, D_MODEL), lambda i: (seg(i), 0, j))


def _inproj(h, mods3, w_pad, seg, tm):
    n = h.shape[0]
    widths = (HY_END, QP_W, 2 * KVP_W, 2 * RG_W)
    return pl.pallas_call(
        _inproj_kernel,
        out_shape=[jax.ShapeDtypeStruct((n, w), f32) for w in widths],
        grid=(n // tm,),
        in_specs=[
            pl.BlockSpec((tm, D_MODEL), lambda i: (i, 0)),
            _mod_spec(seg, 0),
            _mod_spec(seg, 1),
            pl.BlockSpec((D_MODEL, PROJ_PAD_W), lambda i: (0, 0)),
        ],
        out_specs=[pl.BlockSpec((tm, w), lambda i: (i, 0)) for w in widths],
        compiler_params=_cparams(("parallel",)),
        name="inproj",
    )(h, mods3, mods3, w_pad)


def _dwconv_kernel(x_ref, w_ref, b_ref, o_ref, *, width, left):
    x = x_ref[...]
    n = x.shape[0]
    row = lax.broadcasted_iota(jnp.int32, x.shape, 0)
    acc = jnp.zeros_like(x) + b_ref[...]
    for j in range(width):
        off = j - left
        if off == 0:
            xs = x
        else:
            xs = pltpu.roll(x, (-off) % n, axis=0)
            valid = jnp.logical_and(row + off >= 0, row + off < n)
            xs = jnp.where(valid, xs, 0.0)
        acc = acc + xs * w_ref[j:j + 1, :]
    o_ref[...] = acc


def _dwconv(x, w, b, seq_len, row_block0, n_seq, col_block0, n_col_blocks):
    width = w.shape[0]
    wp = jnp.zeros((8, w.shape[1]), f32).at[:width].set(w)
    ct = 256
    return pl.pallas_call(
        functools.partial(_dwconv_kernel, width=width, left=(width - 1) // 2),
        out_shape=jax.ShapeDtypeStruct((n_seq * seq_len, n_col_blocks * ct), f32),
        grid=(n_seq, n_col_blocks),
        in_specs=[
            pl.BlockSpec((seq_len, ct), lambda s, c: (row_block0 + s, col_block0 + c)),
            pl.BlockSpec((8, ct), lambda s, c: (0, c)),
            pl.BlockSpec((1, ct), lambda s, c: (0, c)),
        ],
        out_specs=pl.BlockSpec((seq_len, ct), lambda s, c: (s, c)),
        compiler_params=_cparams(("parallel", "parallel")),
        name="dwconv",
    )(x, wp, b.reshape(1, -1))


def _filter_kernel(z_ref, w1h_ref, w1l_ref, b1_ref, fr_ref, w2h_ref, w2l_ref, b2_ref,
                   w3h_ref, w3l_ref, win_ref, o_ref):
    fr = fr_ref[...]
    f = jnp.sin(fr * (_dot3(z_ref[...], w1h_ref[...], w1l_ref[...]) + b1_ref[...]))
    f = jnp.sin(fr * (_dot3(f, w2h_ref[...], w2l_ref[...]) + b2_ref[...]))
    f = _dot3(f, w3h_ref[...], w3l_ref[...])
    win = win_ref[...]
    o_ref[...] = f * jnp.concatenate([win] * 2, axis=1)


def _filter_features(L):
    t = np.linspace(0.0, 1.0, L, dtype=np.float32)[:, None].astype(np.float64)
    w = np.float32(2.0 * math.pi) * np.arange(L, dtype=np.float32)[:, None] / np.float32(L)
    bands = np.linspace(1e-4, HY_BANDS - 1, HY_BANDS, dtype=np.float32)
    arg = (bands * w).astype(np.float64)
    z = np.concatenate([t, np.cos(arg), -np.sin(arg)], axis=-1)
    deltas = np.abs(np.linspace(HY_MIN_DECAY, HY_MAX_DECAY, HY_W, dtype=np.float32)).astype(np.float64)
    win = np.exp(-t * deltas)
    zp = np.zeros((2 * L, LANE), np.float32)
    zp[:L, :HY_EMB] = z
    zp[L + 1:, :HY_EMB] = z[:0:-1]
    win2 = np.zeros((2 * L, HY_W), np.float32)
    win2[:L] = win
    win2[L + 1:] = win[:0:-1]
    return jnp.asarray(zp), jnp.asarray(win2)


def _pad2(w, rows, cols):
    return jnp.zeros((rows, cols), f32).at[:w.shape[0], :w.shape[1]].set(w)


def _hyena_filters(L, w1, b1, freq, w2, b2, w3):
    zp, win = _filter_features(L)
    w1h, w1l = _split(_pad2(w1, LANE, LANE))
    w2h, w2l = _split(_pad2(w2, LANE, LANE))
    w3h, w3l = _split(_pad2(w3, LANE, 4 * HY_W))
    b1p = _pad2(b1[None, :], 1, LANE)
    b2p = _pad2(b2[None, :], 1, LANE)
    frp = _pad2(freq[None, :], 1, LANE)
    tl = min(L, 512)
    half = L // tl
    full = lambda shape: pl.BlockSpec(shape, lambda i: (0, 0))
    w3_spec = pl.BlockSpec((LANE, 2 * HY_W), lambda i: (0, i // half))
    return pl.pallas_call(
        _filter_kernel,
        out_shape=jax.ShapeDtypeStruct((2 * L, 2 * HY_W), f32),
        grid=(2 * half,),
        in_specs=[
            pl.BlockSpec((tl, LANE), lambda i: (i, 0)),
            full((LANE, LANE)), full((LANE, LANE)), full((1, LANE)), full((1, LANE)),
            full((LANE, LANE)), full((LANE, LANE)), full((1, LANE)),
            w3_spec, w3_spec,
            pl.BlockSpec((tl, HY_W), lambda i: (i, 0)),
        ],
        out_specs=pl.BlockSpec((tl, 2 * HY_W), lambda i: (i, 0)),
        compiler_params=_cparams(("parallel",)),
        name="hyena_filter",
    )(zp, w1h, w1l, b1p, frp, w2h, w2l, b2p, w3h, w3l, win)


def _dft_consts(n):
    n1 = n // FFT_N2
    k = np.arange(n1)
    f_n1 = np.exp(-2j * np.pi * np.outer(k, k) / n1)
    k2 = np.arange(FFT_N2)
    f_n2 = np.exp(-2j * np.pi * np.outer(k2, k2) / FFT_N2)
    tw = np.exp(-2j * np.pi * np.outer(k, k2) / n).reshape(n, 1)
    tw_re = jnp.asarray(np.broadcast_to(tw.real, (n, LANE)).astype(np.float32))
    tw_im = jnp.asarray(np.broadcast_to(tw.imag, (n, LANE)).astype(np.float32))
    return n1, f_n1, f_n2, tw_re, tw_im


def _fa_kernel(u_ref, fs_ref, are_ref, aim_ref, *, grp, passes, real_only):
    fs = fs_ref[...]
    n1 = fs.shape[0] // 4
    jb, c = u_ref.shape[2], u_ref.shape[3]
    for g in range(jb // grp):
        def gather(bi):
            return jnp.concatenate([u_ref[bi, :, g * grp + jj, :] for jj in range(grp)], axis=1)
        p = _apply_stack(fs, gather(0), passes)
        if real_only:
            re, im = p[:n1], p[n1:]
        else:
            q = _apply_stack(fs, gather(1), passes)
            re, im = p[:n1] - q[n1:], p[n1:] + q[:n1]
        for jj in range(grp):
            are_ref[:, g * grp + jj, :] = re[:, jj * c:(jj + 1) * c]
            aim_ref[:, g * grp + jj, :] = im[:, jj * c:(jj + 1) * c]


def _fft_step_a(u4, fs, col_block, c, real_only, passes):
    s, n1_in = u4.shape[0], u4.shape[1]
    n1 = fs.shape[0] // 4
    per = 1 if real_only else 2
    jb = 32
    grp = max(1, 1024 // c)
    shape = jax.ShapeDtypeStruct((s // per, n1, FFT_N2, c), f32)
    return pl.pallas_call(
        functools.partial(_fa_kernel, grp=grp, passes=passes, real_only=real_only),
        out_shape=[shape, shape],
        grid=(s // per, FFT_N2 // jb),
        in_specs=[
            pl.BlockSpec((per, n1_in, jb, c), lambda p, j: (p, 0, j, col_block)),
            pl.BlockSpec(fs.shape, lambda p, j: (0, 0)),
        ],
        out_specs=[pl.BlockSpec((None, n1, jb, c), lambda p, j: (p, 0, j, 0))] * 2,
        compiler_params=_cparams(("parallel", "parallel")),
        name="fft_step_a",
    )(u4, fs)


def _mid_kernel(are_ref, aim_ref, twr_ref, twi_ref, f2_ref, *rest, kb, passes, fwd_only, inv_n):
    if fwd_only:
        ore_ref, oim_ref = rest
    else:
        kr_ref, ki_ref, ore_ref, oim_ref = rest
    f2 = f2_ref[...]
    c = are_ref.shape[1]
    for kk in range(kb):
        rows = pl.ds(kk * FFT_N2, FFT_N2)
        ar, ai = are_ref[rows, :], aim_ref[rows, :]
        tr = jnp.concatenate([twr_ref[rows, :]] * (c // LANE), axis=1)
        ti = jnp.concatenate([twi_ref[rows, :]] * (c // LANE), axis=1)
        xr = ar * tr - ai * ti
        xi = ar * ti + ai * tr
        p = _apply_stack(f2, xr, passes)
        q = _apply_stack(f2, xi, passes)
        sr = p[:FFT_N2] - q[FFT_N2:]
        si = p[FFT_N2:] + q[:FFT_N2]
        if fwd_only:
            ore_ref[rows, :] = sr
            oim_ref[rows, :] = si
        else:
            kr, ki = kr_ref[rows, :], ki_ref[rows, :]
            yr = sr * kr - si * ki
            yi = sr * ki + si * kr
            p2 = _apply_stack(f2, yr, passes)
            q2 = _apply_stack(f2, yi, passes)
            br = p2[:FFT_N2] + q2[FFT_N2:]
            bi = q2[:FFT_N2] - p2[FFT_N2:]
            ore_ref[rows, :] = (br * tr + bi * ti) * inv_n
            oim_ref[rows, :] = (bi * tr - br * ti) * inv_n


def _fft_mid(a_re, a_im, tw_re, tw_im, f2s, kf=None, order=0, passes=FFT_PASSES):
    p, n, c = a_re.shape
    kb = min(4, n // FFT_N2)
    rb = kb * FFT_N2
    fwd_only = kf is None
    blk = pl.BlockSpec((None, rb, c), lambda i, k: (i, k, 0))
    in_specs = [blk, blk,
                pl.BlockSpec((rb, LANE), lambda i, k: (k, 0)),
                pl.BlockSpec((rb, LANE), lambda i, k: (k, 0)),
                pl.BlockSpec(f2s.shape, lambda i, k: (0, 0))]
    args = [a_re, a_im, tw_re, tw_im, f2s]
    if not fwd_only:
        in_specs += [pl.BlockSpec((rb, c), lambda i, k: (k, order))] * 2
        args += list(kf)
    shape = jax.ShapeDtypeStruct((p, n, c), f32)
    return pl.pallas_call(
        functools.partial(_mid_kernel, kb=kb, passes=passes, fwd_only=fwd_only, inv_n=1.0 / n),
        out_shape=[shape, shape],
        grid=(p, n // rb),
        in_specs=in_specs,
        out_specs=[blk, blk],
        compiler_params=_cparams(("parallel", "parallel")),
        name="fft_mid",
    )(*args)


def _fai_kernel(bre_ref, bim_ref, gs_ref, u_ref, gate_ref, skip_ref, o_ref, *, grp, passes):
    gs = gs_ref[...]
    n1h = gs.shape[0] // 4
    jb, c = u_ref.shape[2], u_ref.shape[3]
    skip = skip_ref[...]
    for g in range(jb // grp):
        def gather(ref):
            return jnp.concatenate([ref[:, g * grp + jj, :] for jj in range(grp)], axis=1)
        p = _apply_stack(gs, gather(bre_ref), passes)
        q = _apply_stack(gs, gather(bim_ref), passes)
        ya = p[:n1h] - q[n1h:]
        yb = p[n1h:] + q[:n1h]
        for jj in range(grp):
            j = g * grp + jj
            for bi, y in ((0, ya), (1, yb)):
                u = u_ref[bi, :, j, :]
                o_ref[bi, :, j, :] = gate_ref[bi, :, j, :] * (y[:, jj * c:(jj + 1) * c] + u * skip)


def _fft_step_a_inv(b_re, b_im, gs, u4, u_col, gate4, gate_col, skip, passes):
    p, n1, _, c = b_re.shape
    n1h = n1 // 2
    jb = 32
    grp = max(1, 1024 // c)
    bspec = pl.BlockSpec((None, n1, jb, c), lambda i, j: (i, 0, j, 0))
    return pl.pallas_call(
        functools.partial(_fai_kernel, grp=grp, passes=passes),
        out_shape=jax.ShapeDtypeStruct((2 * p, n1h, FFT_N2, c), f32),
        grid=(p, FFT_N2 // jb),
        in_specs=[
            bspec, bspec,
            pl.BlockSpec(gs.shape, lambda i, j: (0, 0)),
            pl.BlockSpec((2, n1h, jb, c), lambda i, j: (i, 0, j, u_col)),
            pl.BlockSpec((2, n1h, jb, c), lambda i, j: (i, 0, j, gate_col)),
            pl.BlockSpec((1, c), lambda i, j: (0, 0)),
        ],
        out_specs=pl.BlockSpec((2, n1h, jb, c), lambda i, j: (i, 0, j, 0)),
        compiler_params=_cparams(("parallel", "parallel")),
        name="fft_step_a_inv",
    )(b_re, b_im, gs, u4, gate4, skip)


def _hyena_long(xz, filt, skip, n_seq, L, passes=FFT_PASSES):
    n = 2 * L
    n1, f_n1, f_n2, tw_re, tw_im = _dft_consts(n)
    n1h = n1 // 2
    c = HY_W
    fs_full = _stack_dft(f_n1)
    fs_half = _stack_dft(f_n1[:, :n1h])
    gs = _stack_dft(np.conj(f_n1)[:n1h, :])
    f2s = _stack_dft(f_n2)
    kc4 = filt.reshape(1, n1, FFT_N2, 2 * c)
    k_re, k_im = _fft_step_a(kc4, fs_full, 0, 2 * c, True, FILTER_PASSES)
    kf = _fft_mid(k_re.reshape(1, n, 2 * c), k_im.reshape(1, n, 2 * c), tw_re, tw_im, f2s,
                  passes=FILTER_PASSES)
    kf = (kf[0].reshape(n, 2 * c), kf[1].reshape(n, 2 * c))
    xz4 = xz.reshape(n_seq, n1h, FFT_N2, 3 * c)
    z4, z_col = xz4, 2
    for order in range(2):
        a_re, a_im = _fft_step_a(z4, fs_half, z_col, c, False, passes)
        p = n_seq // 2
        b_re, b_im = _fft_mid(a_re.reshape(p, n, c), a_im.reshape(p, n, c), tw_re, tw_im, f2s,
                              kf=kf, order=order, passes=passes)
        z4 = _fft_step_a_inv(b_re.reshape(p, n1, FFT_N2, c), b_im.reshape(p, n1, FFT_N2, c), gs,
                             z4, z_col, xz4, order, skip[order:order + 1], passes)
        z_col = 0
    return z4.reshape(n_seq * L, c)


def _dense_spec_kernel(k_ref, fs_ref, ore_ref, oim_ref, *, passes):
    n = k_ref.shape[0]
    p = _apply_stack(fs_ref[...], k_ref[...], passes)
    ore_ref[...] = p[:n]
    oim_ref[...] = p[n:]


def _dense_conv_kernel(xz_ref, fs_ref, gs_ref, kr_ref, ki_ref, skip_ref, o_ref, *, passes):
    L = xz_ref.shape[1]
    n = 2 * L
    c = HY_W
    fs, gs = fs_ref[...], gs_ref[...]
    za, zb = xz_ref[0, :, 2 * c:], xz_ref[1, :, 2 * c:]
    for order in range(2):
        p = _apply_stack(fs, za, passes)
        q = _apply_stack(fs, zb, passes)
        sr, si = p[:n] - q[n:], p[n:] + q[:n]
        kr, ki = kr_ref[:, order * c:(order + 1) * c], ki_ref[:, order * c:(order + 1) * c]
        yr, yi = sr * kr - si * ki, sr * ki + si * kr
        p2 = _apply_stack(gs, yr, passes)
        q2 = _apply_stack(gs, yi, passes)
        ya = (p2[:L] - q2[L:]) * (1.0 / n)
        yb = (p2[L:] + q2[:L]) * (1.0 / n)
        skip = skip_ref[order:order + 1, :]
        za = xz_ref[0, :, order * c:(order + 1) * c] * (ya + za * skip)
        zb = xz_ref[1, :, order * c:(order + 1) * c] * (yb + zb * skip)
    o_ref[0] = za
    o_ref[1] = zb


def _hyena_short_seq(xz, filt, skip, n_seq, L, passes=FFT_PASSES):
    n = 2 * L
    c = HY_W
    k = np.arange(n)
    f_n = np.exp(-2j * np.pi * np.outer(k, k) / n)
    fs_full = _stack_dft(f_n)
    fs_half = _stack_dft(f_n[:, :L])
    gs = _stack_dft(np.conj(f_n)[:L, :])
    kc = filt
    shape = jax.ShapeDtypeStruct((n, 2 * c), f32)
    k_re, k_im = pl.pallas_call(
        functools.partial(_dense_spec_kernel, passes=FILTER_PASSES),
        out_shape=[shape, shape],
        compiler_params=_cparams(None),
        name="dense_filter_spectrum",
    )(kc, fs_full)
    skip_p = jnp.zeros((8, c), f32).at[:2].set(skip)
    full = lambda a: pl.BlockSpec(a.shape, lambda i: (0,) * a.ndim)
    out = pl.pallas_call(
        functools.partial(_dense_conv_kernel, passes=passes),
        out_shape=jax.ShapeDtypeStruct((n_seq, L, c), f32),
        grid=(n_seq // 2,),
        in_specs=[pl.BlockSpec((2, L, 3 * c), lambda i: (i, 0, 0)),
                  full(fs_half), full(gs), full(k_re), full(k_im), full(skip_p)],
        out_specs=pl.BlockSpec((2, L, c), lambda i: (i, 0, 0)),
        compiler_params=_cparams(("parallel",)),
        name="dense_long_conv",
    )(xz.reshape(n_seq, L, 3 * c), fs_half, gs, k_re, k_im, skip_p)
    return out.reshape(n_seq * L, c)


def _rope_tables(T, tm):
    n_rows = T // GRID_W
    row = np.repeat(np.arange(n_rows, dtype=np.float32), GRID_W)
    col = np.tile(np.arange(GRID_W, dtype=np.float32), n_rows)
    axis_dim = HEAD_DIM // 2
    inv_freq = np.float32(ROPE_THETA) ** (-np.arange(0, axis_dim, 2, dtype=np.float32) / np.float32(axis_dim))
    ang = np.concatenate([row[:, None] * inv_freq, col[:, None] * inv_freq], axis=-1).astype(np.float64)
    cos = np.zeros((T + tm, HEAD_PAD), np.float32)
    sin = np.zeros((T + tm, HEAD_PAD), np.float32)
    cos[:T, 0:HEAD_DIM:2] = np.cos(ang)
    cos[:T, 1:HEAD_DIM:2] = np.cos(ang)
    sin[:T, 0:HEAD_DIM:2] = -np.sin(ang)
    sin[:T, 1:HEAD_DIM:2] = np.sin(ang)
    cos[T:, :HEAD_DIM] = 1.0
    return jnp.asarray(cos), jnp.asarray(sin)


def _qkprep_kernel(q_ref, kv_ref, cos_ref, sin_ref, qg_ref, kg_ref, avg_ref, oq_ref, okv_ref):
    cos, sin = cos_ref[...], sin_ref[...]
    avg = avg_ref[...]
    lane = lax.broadcasted_iota(jnp.int32, cos.shape, 1)
    even = (lane % 2) == 0

    def norm_rope(x, gain, scale):
        sq_hi, sq_lo = _split(x * x)
        ms = _dot(sq_hi, avg) + _dot(sq_lo, avg)
        xn = x * lax.rsqrt(ms + QK_EPS) * gain
        swapped = jnp.where(even, pltpu.roll(xn, LANE - 1, axis=1), pltpu.roll(xn, 1, axis=1))
        return ((xn * cos + swapped * sin) * scale).astype(bf16)

    for h in range(N_HEADS):
        sl = slice(h * HEAD_PAD, (h + 1) * HEAD_PAD)
        oq_ref[:, sl] = norm_rope(q_ref[:, sl], qg_ref[...], HEAD_DIM ** -0.5)
    for g in range(N_KV_HEADS):
        sl = slice(g * HEAD_PAD, (g + 1) * HEAD_PAD)
        okv_ref[:, sl] = norm_rope(kv_ref[:, sl], kg_ref[...], 1.0)
    okv_ref[:, KVP_W:] = kv_ref[:, KVP_W:].astype(bf16)


def _qkprep(q, kv, cos, sin, q_gain, k_gain, pos_block, tm):
    n = q.shape[0]
    pad = lambda g: jnp.zeros((1, HEAD_PAD), f32).at[0, :HEAD_DIM].set(g)
    avg = jnp.full((HEAD_PAD, HEAD_PAD), 1.0 / HEAD_DIM, bf16)
    one = lambda shape: pl.BlockSpec(shape, lambda i: (0, 0))
    return pl.pallas_call(
        _qkprep_kernel,
        out_shape=[jax.ShapeDtypeStruct((n, QP_W), bf16), jax.ShapeDtypeStruct((n, 2 * KVP_W), bf16)],
        grid=(n // tm,),
        in_specs=[
            pl.BlockSpec((tm, QP_W), lambda i: (i, 0)),
            pl.BlockSpec((tm, 2 * KVP_W), lambda i: (i, 0)),
            pl.BlockSpec((tm, HEAD_PAD), lambda i: (pos_block(i), 0)),
            pl.BlockSpec((tm, HEAD_PAD), lambda i: (pos_block(i), 0)),
            one((1, HEAD_PAD)), one((1, HEAD_PAD)), one((HEAD_PAD, HEAD_PAD)),
        ],
        out_specs=[pl.BlockSpec((tm, QP_W), lambda i: (i, 0)),
                   pl.BlockSpec((tm, 2 * KVP_W), lambda i: (i, 0))],
        compiler_params=_cparams(("parallel",)),
        name="qk_prep",
    )(q, kv, cos, sin, pad(q_gain), pad(k_gain), avg)


def _attn_kernel(q_ref, kvl_ref, kvc_ref, o_ref, *, nq_lat):
    i = pl.program_id(1)
    dims = (((1,), (1,)), ((), ()))

    def scores(h, use_lat):
        g = h // N_GROUPS
        ks = slice(g * HEAD_PAD, (g + 1) * HEAD_PAD)
        q = q_ref[:, h * HEAD_PAD:(h + 1) * HEAD_PAD]
        sc = lax.dot_general(q, kvc_ref[:, ks], dims, preferred_element_type=f32)
        sl = lax.dot_general(q, kvl_ref[:, ks], dims, preferred_element_type=f32) if use_lat else None
        return sl, sc

    def heads(use_lat):
        nxt = scores(0, use_lat)
        for h in range(N_HEADS):
            sl, sc = nxt
            if h + 1 < N_HEADS:
                nxt = scores(h + 1, use_lat)
            g = h // N_GROUPS
            vs = slice(KVP_W + g * HEAD_PAD, KVP_W + (g + 1) * HEAD_PAD)
            m = jnp.max(sc, axis=-1, keepdims=True)
            if use_lat:
                m = jnp.maximum(m, jnp.max(sl, axis=-1, keepdims=True))
                p_l = jnp.exp(sl - m)
                denom = jnp.sum(p_l, axis=-1, keepdims=True)
                acc = _dot(p_l.astype(bf16), kvl_ref[:, vs])
            p_c = jnp.exp(sc - m)
            if use_lat:
                denom = denom + jnp.sum(p_c, axis=-1, keepdims=True)
                acc = acc + _dot(p_c.astype(bf16), kvc_ref[:, vs])
            else:
                denom = jnp.sum(p_c, axis=-1, keepdims=True)
                acc = _dot(p_c.astype(bf16), kvc_ref[:, vs])
            o_ref[:, h * HEAD_PAD:(h + 1) * HEAD_PAD] = (acc / denom).astype(bf16)

    @pl.when(i < nq_lat)
    def _():
        heads(True)

    @pl.when(i == nq_lat)
    def _():
        heads(False)


def _attention(qp, kvp, B, T, Lc):
    n = qp.shape[0]
    tq = Lc
    nq = T // tq
    q_idx = lambda b, i: (jnp.where(i < nq, b * nq + i, B * nq + b), 0)
    return pl.pallas_call(
        functools.partial(_attn_kernel, nq_lat=nq),
        out_shape=jax.ShapeDtypeStruct((n, QP_W), bf16),
        grid=(B, nq + 1),
        in_specs=[
            pl.BlockSpec((tq, QP_W), q_idx),
            pl.BlockSpec((T, 2 * KVP_W), lambda b, i: (b, 0)),
            pl.BlockSpec((Lc, 2 * KVP_W), lambda b, i: (B * nq + b, 0)),
        ],
        out_specs=pl.BlockSpec((tq, QP_W), q_idx),
        compiler_params=_cparams(("parallel", "arbitrary")),
        name="attention",
    )(qp, kvp, kvp)


def _scan_kernel(u_ref, wh_ref, wl_ref, bias_ref, lam_ref, h0_ref, *rest, reverse, final):
    if final:
        hprev_ref, g_ref, o_ref, hend_ref, carry_ref = rest
    else:
        o_ref, hend_ref, carry_ref = rest
    t = pl.program_id(1)

    @pl.when(t == 0)
    def _():
        carry_ref[...] = h0_ref[...]

    u = u_ref[...]
    tt = u.shape[0]
    gates = _dot3(u, wh_ref[...], wl_ref[...]) + bias_ref[...]
    r = jax.nn.sigmoid(gates[:, :RG_W])
    ig = jax.nn.sigmoid(gates[:, RG_W:])
    lam = lam_ref[...]
    softplus = jnp.maximum(-lam, 0.0) + jnp.log1p(jnp.exp(-jnp.abs(lam)))
    log_a = -RG_C * r * softplus
    a = jnp.exp(log_a)
    b = jnp.sqrt(-jnp.tanh(log_a) * (a * a + 1.0)) * (ig * u)
    row = lax.broadcasted_iota(jnp.int32, a.shape, 0)
    s = 1
    while s < tt:
        if reverse:
            a_s, b_s = pltpu.roll(a, tt - s, axis=0), pltpu.roll(b, tt - s, axis=0)
            valid = row < tt - s
        else:
            a_s, b_s = pltpu.roll(a, s, axis=0), pltpu.roll(b, s, axis=0)
            valid = row >= s
        b = a * jnp.where(valid, b_s, 0.0) + b
        a = a * jnp.where(valid, a_s, 1.0)
        s *= 2
    h = a * carry_ref[...] + b
    last = h[0:1, :] if reverse else h[tt - 1:tt, :]
    carry_ref[...] = last
    hend_ref[...] = last
    if final:
        o_ref[...] = ((hprev_ref[...] + h) * jax.nn.gelu(g_ref[...], approximate=True)).astype(o_ref.dtype)
    else:
        o_ref[...] = h


def _rg_scan(u, n_seq, L, w_hi, w_lo, bias, lam, h0, reverse, hprev=None, gate=None, gate_row0=0):
    tt = min(SCAN_TILE, L)
    nt = L // tt
    final = hprev is not None
    tidx = (lambda t: nt - 1 - t) if reverse else (lambda t: t)
    row = lambda b, t: (b * nt + tidx(t), 0)
    one = lambda shape: pl.BlockSpec(shape, lambda b, t: (0, 0))
    in_specs = [pl.BlockSpec((tt, RG_W), row), one(w_hi.shape), one(w_lo.shape), one((1, 2 * RG_W)),
                one((1, RG_W)), pl.BlockSpec((None, 1, RG_W), lambda b, t: (b, 0, 0))]
    args = [u, w_hi, w_lo, bias, lam, h0]
    if final:
        in_specs += [pl.BlockSpec((tt, RG_W), row),
                     pl.BlockSpec((tt, RG_W), lambda b, t: (gate_row0 // tt + b * nt + tidx(t), 1))]
        args += [hprev, gate]
    return pl.pallas_call(
        functools.partial(_scan_kernel, reverse=reverse, final=final),
        out_shape=[jax.ShapeDtypeStruct((n_seq * L, RG_W), bf16 if final else f32),
                   jax.ShapeDtypeStruct((n_seq, 1, RG_W), f32)],
        grid=(n_seq, nt),
        in_specs=in_specs,
        out_specs=[pl.BlockSpec((tt, RG_W), row), pl.BlockSpec((None, 1, RG_W), lambda b, t: (b, 0, 0))],
        scratch_shapes=[pltpu.VMEM((1, RG_W), f32)],
        compiler_params=_cparams(("parallel", "arbitrary")),
        name="rg_scan",
    )(*args)


def _block_diag(w):
    bw = w.shape[-1]
    out = jnp.zeros((RG_W, RG_W), f32)
    for h in range(RG_BLOCKS):
        out = out.at[h * bw:(h + 1) * bw, h * bw:(h + 1) * bw].set(w[h])
    return out


def _rglru(rg, B, T, Lc, conv_w, conv_b, lam, w_a, b_a, w_x, b_x):
    bt = B * T
    u_l = _dwconv(rg, conv_w, conv_b, T, 0, B, 0, 1)
    u_c = _dwconv(rg, conv_w, conv_b, Lc, bt // Lc, B, 0, 1)
    zeros = jnp.zeros((B, 1, RG_W), f32)
    prev_l = prev_c = None
    for d, rev in enumerate((False, True)):
        w_hi, w_lo = _split(jnp.concatenate([_block_diag(w_a[d]), _block_diag(w_x[d])], axis=1))
        bias = jnp.concatenate([b_a[d], b_x[d]])[None, :]
        lam_d = lam[d][None, :]
        last = d == 1
        kw_c = dict(hprev=prev_c, gate=rg, gate_row0=bt) if last else {}
        kw_l = dict(hprev=prev_l, gate=rg, gate_row0=0) if last else {}
        prev_c, h_end = _rg_scan(u_c, B, Lc, w_hi, w_lo, bias, lam_d, zeros, rev, **kw_c)
        prev_l, _ = _rg_scan(u_l, B, T, w_hi, w_lo, bias, lam_d, h_end, rev, **kw_l)
    return jnp.concatenate([prev_l, prev_c], axis=0)


def _layer_norm(v, g, b):
    mu = jnp.mean(v, axis=-1, keepdims=True)
    d = v - mu
    var = jnp.mean(d * d, axis=-1, keepdims=True)
    return d * lax.rsqrt(var + LN_EPS) * g + b


def _outproj_kernel(hy_ref, att_ref, rg_ref, h_ref, g1_ref, sh2_ref, sc2_ref, lng_ref, lnb_ref,
                    why_ref, watt_ref, wrg_ref, wrh_ref, wrl_ref, br_ref,
                    oh_ref, om_ref, og_ref, oi_ref, oc_ref, *, alpha):
    y = (_dot(hy_ref[...].astype(bf16), why_ref[...]) + _dot(att_ref[...], watt_ref[...])
         + _dot(rg_ref[...], wrg_ref[...]))
    hn = _layer_norm(alpha * h_ref[...] + g1_ref[...] * y, lng_ref[...], lnb_ref[...])
    oh_ref[...] = hn
    m = hn * (1.0 + sc2_ref[...]) + sh2_ref[...]
    om_ref[...] = m.astype(bf16)
    scores = jax.nn.sigmoid(_dot3(m, wrh_ref[...], wrl_ref[...]))
    sel = scores + br_ref[...]
    lane = lax.broadcasted_iota(jnp.int32, sel.shape, 1)
    picked = jnp.zeros_like(scores)
    chosen = jnp.zeros(sel.shape, jnp.int32)
    count = jnp.zeros((1, LANE), f32)
    for k in range(TOP_K):
        mx = jnp.max(sel, axis=-1, keepdims=True)
        first = jnp.min(jnp.where(sel == mx, lane, LANE), axis=-1, keepdims=True)
        hit = lane == first
        picked = jnp.where(hit, scores, picked)
        sel = jnp.where(hit, -jnp.inf, sel)
        chosen = jnp.where(lane == k, first, chosen)
        count = count + jnp.sum(jnp.where(hit, 1.0, 0.0), axis=0, keepdims=True)
    og_ref[...] = picked / jnp.sum(picked, axis=-1, keepdims=True) * ROUTED_SCALE
    oi_ref[...] = chosen
    oc_ref[...] = count


def _outproj(hy, att, rg, h, mods3, ln_g, ln_b, w_hy, w_att, w_rg, wr_hi, wr_lo, b_router, seg, tm, alpha):
    n = h.shape[0]
    row = lambda w: pl.BlockSpec((tm, w), lambda i: (i, 0))
    one = lambda a: pl.BlockSpec(a.shape, lambda i: (0,) * a.ndim)
    return pl.pallas_call(
        functools.partial(_outproj_kernel, alpha=alpha),
        out_shape=[jax.ShapeDtypeStruct((n, D_MODEL), f32), jax.ShapeDtypeStruct((n, D_MODEL), bf16),
                   jax.ShapeDtypeStruct((n, LANE), f32), jax.ShapeDtypeStruct((n, LANE), jnp.int32),
                   jax.ShapeDtypeStruct((n // tm, 1, LANE), f32)],
        grid=(n // tm,),
        in_specs=[row(HY_W), row(QP_W), row(RG_W), row(D_MODEL),
                  _mod_spec(seg, 2), _mod_spec(seg, 3), _mod_spec(seg, 4),
                  one(ln_g), one(ln_b), one(w_hy), one(w_att), one(w_rg), one(wr_hi), one(wr_lo),
                  one(b_router)],
        out_specs=[row(D_MODEL), row(D_MODEL), row(LANE), row(LANE),
                   pl.BlockSpec((None, 1, LANE), lambda i: (i, 0, 0))],
        compiler_params=_cparams(("parallel",)),
        name="outproj_router",
    )(hy, att, rg, h, mods3, mods3, mods3, ln_g, ln_b, w_hy, w_att, w_rg, wr_hi, wr_lo, b_router)


def _xs_rows(tile):
    return -(-(TOP_K * tile + N_EXPERTS * (RUN_ALIGN - 1)) // 256) * 256


def _perm_matrix(idx, runstart, gates):
    t = idx.shape[0]
    lane = lax.broadcasted_iota(jnp.int32, (t, LANE), 1)
    hits = [lane == idx[:, k:k + 1] for k in range(TOP_K)]
    sel = jnp.zeros((t, LANE), f32)
    for hit in hits:
        sel = jnp.where(hit, 1.0, sel)
    earlier = (lax.broadcasted_iota(jnp.int32, (t, t), 1) < lax.broadcasted_iota(jnp.int32, (t, t), 0))
    rank = _dot(jnp.where(earlier, 1.0, 0.0).astype(bf16), sel.astype(bf16))
    posmat = rank + runstart
    slot = lax.broadcasted_iota(jnp.int32, (t, _xs_rows(t)), 1).astype(jnp.int16)
    acc = jnp.zeros(slot.shape, bf16)
    for hit in hits:
        pos = jnp.sum(jnp.where(hit, posmat, 0.0), axis=-1, keepdims=True).astype(jnp.int32).astype(jnp.int16)
        if gates is None:
            val = jnp.ones((t, 1), bf16)
        else:
            val = jnp.sum(jnp.where(hit, gates, 0.0), axis=-1, keepdims=True).astype(bf16)
        acc = jnp.where(slot == pos, val, acc)
    return acc


def _plan_pieces(i, rs_ref, len_ref, dst_ref, plan_refs):
    def put(row, pos, val):
        for ref in plan_refs:
            ref[row, pos] = val

    def body(e, counts):
        c64, c32, c16 = counts
        base = i * N_EXPERTS + e
        rs, ln, dst = rs_ref[base], len_ref[base], dst_ref[base]
        n_chunks = ln // RUN_CHUNK
        for j in range(DISPATCH_TILE // RUN_CHUNK):
            put(0, c64, rs + j * RUN_CHUNK)
            put(1, c64, dst + j * RUN_CHUNK)
            c64 = c64 + (j < n_chunks).astype(jnp.int32)
        off = n_chunks * RUN_CHUNK
        put(2, c32, rs + off)
        put(3, c32, dst + off)
        c32 = c32 + ((ln >> 5) & 1)
        off = off + (ln & 32)
        put(4, c16, rs + off)
        put(5, c16, dst + off)
        c16 = c16 + ((ln >> 4) & 1)
        return c64, c32, c16

    zero = jnp.int32(0)
    counts = lax.fori_loop(0, N_EXPERTS, body, (zero, zero, zero))
    for k, count in enumerate(counts):
        put(PLAN_COUNT_ROW, k, count)


def _for_pieces(plan_ref, fn):
    for k, size in enumerate(PIECE_SIZES):
        def body(j, carry, k=k, size=size):
            fn(pl.multiple_of(plan_ref[2 * k, j], RUN_ALIGN), pl.multiple_of(plan_ref[2 * k + 1, j], RUN_ALIGN),
               size)
            return carry

        lax.fori_loop(0, plan_ref[PLAN_COUNT_ROW, k], body, 0)


def _dispatch_kernel(rs_ref, len_ref, dst_ref, gap_ref, reg_ref, x_ref, idx_ref, rsf_ref, s_ref, plan_ref,
                     xs_ref, zero_ref, sem, mine_ref):
    i = pl.program_id(0)
    slot = i % 2

    @pl.when(i == 0)
    def _():
        zero_ref[...] = jnp.zeros(zero_ref.shape, zero_ref.dtype)

        def fill(action):
            def body(e, carry):
                @pl.when(reg_ref[e] > 0)
                def _():
                    dst = pl.multiple_of(gap_ref[e], RUN_ALIGN)
                    action(pltpu.make_async_copy(zero_ref, s_ref.at[pl.ds(dst, FFN_TILE)], sem))
                return carry
            lax.fori_loop(0, N_EXPERTS, body, 0)

        fill(lambda cp: cp.start())
        fill(lambda cp: cp.wait())

    p_t = _perm_matrix(idx_ref[...], rsf_ref[...], None)
    xs_ref[slot] = lax.dot_general(p_t, x_ref[...], (((0,), (0,)), ((), ())),
                                   preferred_element_type=f32).astype(bf16)

    def copy(buf, src, dst, sz):
        return pltpu.make_async_copy(xs_ref.at[buf].at[pl.ds(src, sz)], s_ref.at[pl.ds(dst, sz)], sem)

    @pl.when(i > 0)
    def _():
        _for_pieces(mine_ref, lambda s, d, sz: copy(1 - slot, s, d, sz).wait())

    _plan_pieces(i, rs_ref, len_ref, dst_ref, (mine_ref, plan_ref))
    _for_pieces(mine_ref, lambda s, d, sz: copy(slot, s, d, sz).start())

    @pl.when(i == pl.num_programs(0) - 1)
    def _():
        _for_pieces(mine_ref, lambda s, d, sz: copy(slot, s, d, sz).wait())


def _ffn_kernel(te_ref, nu_ref, x_ref, wg_ref, wu_ref, wd_ref, o_ref):
    @pl.when(pl.program_id(0) < nu_ref[0])
    def _():
        x = x_ref[...]
        hid = jax.nn.silu(_dot(x, wg_ref[...].astype(bf16))) * _dot(x, wu_ref[...].astype(bf16))
        o_ref[...] = _dot(hid.astype(bf16), wd_ref[...].astype(bf16)).astype(bf16)


def _combine_kernel(z_ref, plan_ref, idx_ref, gate_ref, rsf_ref, x_ref, sgu_ref, sd_ref,
                    h_ref, g2_ref, lng_ref, lnb_ref, o_ref, zbuf_ref, sem, *, alpha):
    i = pl.program_id(0)

    @pl.when(i == 0)
    def _():
        zbuf_ref[...] = jnp.zeros(zbuf_ref.shape, zbuf_ref.dtype)

    def copy(src, dst, sz):
        return pltpu.make_async_copy(z_ref.at[pl.ds(dst, sz)], zbuf_ref.at[pl.ds(src, sz)], sem)

    _for_pieces(plan_ref, lambda s, d, sz: copy(s, d, sz).start())
    hs = _dot(x_ref[...], sgu_ref[...])
    shared = _dot((jax.nn.silu(hs[:, :EXPERT_FF]) * hs[:, EXPERT_FF:]).astype(bf16), sd_ref[...])
    w_t = _perm_matrix(idx_ref[...], rsf_ref[...], gate_ref[...])
    _for_pieces(plan_ref, lambda s, d, sz: copy(s, d, sz).wait())
    y = _dot(w_t, zbuf_ref[...]) + shared
    o_ref[...] = _layer_norm(alpha * h_ref[...] + g2_ref[...] * y, lng_ref[...], lnb_ref[...])


def _route_plan(counts, n_ffn_tiles):
    c = counts[:, 0, :N_EXPERTS].astype(jnp.int32)
    lens = (c + RUN_ALIGN - 1) // RUN_ALIGN * RUN_ALIGN
    runstart = jnp.cumsum(lens, axis=1) - lens
    region = (jnp.sum(lens, axis=0) + FFN_TILE - 1) // FFN_TILE * FFN_TILE
    region_end = jnp.cumsum(region)
    dest = (region_end - region)[None, :] + jnp.cumsum(lens, axis=0) - lens
    n_used = region_end[-1] // FFN_TILE
    tile_row = jnp.minimum(jnp.arange(n_ffn_tiles, dtype=jnp.int32), n_used - 1) * FFN_TILE
    tile_expert = jnp.minimum(jnp.sum(tile_row[:, None] >= region_end[None, :], axis=1), N_EXPERTS - 1)
    runstart_f = jnp.zeros((c.shape[0], 1, LANE), f32).at[:, 0, :N_EXPERTS].set(runstart.astype(f32))
    return dict(rs=runstart.reshape(-1), ln=lens.reshape(-1), dst=dest.reshape(-1),
                gap=region_end - FFN_TILE, region=region, n_used=n_used.reshape(1),
                tile_expert=tile_expert.astype(jnp.int32), runstart_f=runstart_f)


def _moe(m, gates, idx, counts, layer, w_gate, w_up, w_down, sgu, sd, h, mods3, ln_g, ln_b, seg, alpha):
    n = h.shape[0]
    t = DISPATCH_TILE
    nt = n // t
    xs_rows = _xs_rows(t)
    rows_max = TOP_K * n + nt * N_EXPERTS * (RUN_ALIGN - 1) + N_EXPERTS * (FFN_TILE - RUN_ALIGN)
    n_ffn_tiles = -(-rows_max // FFN_TILE)
    plan = _route_plan(counts, n_ffn_tiles)
    tile = lambda w: pl.BlockSpec((t, w), lambda i, *_: (i, 0))
    rsf_spec = pl.BlockSpec((None, 1, LANE), lambda i, *_: (i, 0, 0))
    one = lambda a: pl.BlockSpec(a.shape, lambda i, *_: (0,) * a.ndim)
    any_spec = pl.BlockSpec(memory_space=pl.ANY)

    plan_spec = pl.BlockSpec((None, PLAN_ROWS, LANE), lambda i, *_: (i, 0, 0), memory_space=pltpu.SMEM)
    sorted_x, pieces = pl.pallas_call(
        _dispatch_kernel,
        out_shape=[jax.ShapeDtypeStruct((n_ffn_tiles * FFN_TILE, D_MODEL), bf16),
                   jax.ShapeDtypeStruct((nt, PLAN_ROWS, LANE), jnp.int32)],
        grid_spec=pltpu.PrefetchScalarGridSpec(
            num_scalar_prefetch=5, grid=(nt,),
            in_specs=[tile(D_MODEL), tile(LANE), rsf_spec],
            out_specs=[any_spec, plan_spec],
            scratch_shapes=[pltpu.VMEM((2, xs_rows, D_MODEL), bf16), pltpu.VMEM((FFN_TILE, D_MODEL), bf16),
                            pltpu.SemaphoreType.DMA, pltpu.SMEM((PLAN_ROWS, LANE), jnp.int32)]),
        compiler_params=_cparams(("arbitrary",)),
        name="moe_dispatch",
    )(plan["rs"], plan["ln"], plan["dst"], plan["gap"], plan["region"], m, idx, plan["runstart_f"])

    used = lambda i, te, nu: jnp.minimum(i, nu[0] - 1)
    sorted_z = pl.pallas_call(
        _ffn_kernel,
        out_shape=jax.ShapeDtypeStruct((n_ffn_tiles * FFN_TILE, D_MODEL), bf16),
        grid_spec=pltpu.PrefetchScalarGridSpec(
            num_scalar_prefetch=2, grid=(n_ffn_tiles,),
            in_specs=[pl.BlockSpec((FFN_TILE, D_MODEL), lambda i, te, nu: (used(i, te, nu), 0)),
                      pl.BlockSpec((None, None, D_MODEL, EXPERT_FF), lambda i, te, nu: (layer, te[i], 0, 0)),
                      pl.BlockSpec((None, None, D_MODEL, EXPERT_FF), lambda i, te, nu: (layer, te[i], 0, 0)),
                      pl.BlockSpec((None, None, EXPERT_FF, D_MODEL), lambda i, te, nu: (layer, te[i], 0, 0))],
            out_specs=pl.BlockSpec((FFN_TILE, D_MODEL), lambda i, te, nu: (used(i, te, nu), 0))),
        compiler_params=_cparams(("arbitrary",)),
        name="moe_ffn",
    )(plan["tile_expert"], plan["n_used"], sorted_x, w_gate, w_up, w_down)

    return pl.pallas_call(
        functools.partial(_combine_kernel, alpha=alpha),
        out_shape=jax.ShapeDtypeStruct((n, D_MODEL), f32),
        grid=(nt,),
        in_specs=[any_spec, plan_spec, tile(LANE), tile(LANE), rsf_spec, tile(D_MODEL), one(sgu), one(sd),
                  tile(D_MODEL), pl.BlockSpec((None, 1, D_MODEL), lambda i, *_: (seg(i), 0, 5)),
                  one(ln_g), one(ln_b)],
        out_specs=tile(D_MODEL),
        scratch_shapes=[pltpu.VMEM((xs_rows, D_MODEL), bf16), pltpu.SemaphoreType.DMA],
        compiler_params=_cparams(("arbitrary",)),
        name="moe_combine",
    )(sorted_z, pieces, idx, gates, plan["runstart_f"], m, sgu, sd, h, mods3, ln_g, ln_b)


def _pad_heads_cols(w, n_heads):
    lead = w.shape[:-1]
    w = w.reshape(*lead, n_heads, HEAD_DIM)
    w = jnp.concatenate([w, jnp.zeros_like(w)], axis=-1)
    return w.reshape(*lead, n_heads * HEAD_PAD)


def _pad_in_proj(w_in):
    hy, q, k, v, rgx, rgg = (w_in[..., :HY_END], w_in[..., HY_END:Q_END], w_in[..., Q_END:K_END],
                             w_in[..., K_END:V_END], w_in[..., V_END:RGX_END], w_in[..., RGX_END:])
    return jnp.concatenate([hy, _pad_heads_cols(q, N_HEADS), _pad_heads_cols(k, N_KV_HEADS),
                            _pad_heads_cols(v, N_KV_HEADS), rgx, rgg], axis=-1).astype(bf16)


def kernel(x, c, ctx, c_ctx, w_mod, b_mod, ln1_g, ln1_b, ln2_g, ln2_b, w_in, w_out,
           hy_short_w, hy_short_b, hy_f_w1, hy_f_b1, hy_f_freq, hy_f_w2, hy_f_b2, hy_f_w3, hy_skip,
           q_norm, k_norm, rg_conv_w, rg_conv_b, rg_lambda, rg_w_a, rg_b_a, rg_w_x, rg_b_x,
           w_router, b_router, w_gate, w_up, w_down, ws_gate, ws_up, ws_down):
    B, T, D = x.shape
    Lc = ctx.shape[1]
    depth = w_mod.shape[0]
    bt, bc = B * T, B * Lc
    n_tok = bt + bc
    tm = TOKEN_TILE
    assert D == D_MODEL and T % tm == 0 and bc % tm == 0 and B % 2 == 0
    assert T % ATT_Q_TILE == 0 and Lc == ATT_Q_TILE and T % GRID_W == 0
    alpha = (2 * depth) ** 0.25
    tiles_per_seq = T // tm
    seg = lambda i: jnp.minimum(i // tiles_per_seq, B)
    pos_block = lambda i: jnp.where(i < B * tiles_per_seq, i % tiles_per_seq, tiles_per_seq)
    moe_tm = DISPATCH_TILE
    assert T % moe_tm == 0 and bc % moe_tm == 0
    moe_seg = lambda i: jnp.minimum(i // (T // moe_tm), B)

    rows = -(-(B + 1) // 8) * 8
    cond = jnp.zeros((rows, D), f32).at[:B].set(c).at[B].set(c_ctx)
    mods = _modulation(cond, w_mod, b_mod)

    cos, sin = _rope_tables(T, tm)
    w_in_pad = _pad_in_proj(w_in)
    w_out_hy = w_out[:, :HY_W].astype(bf16)
    w_out_att = _pad_heads_cols(w_out[:, HY_W:HY_W + ATT_W].transpose(0, 2, 1), N_HEADS).transpose(0, 2, 1).astype(bf16)
    w_out_rg = w_out[:, HY_W + ATT_W:].astype(bf16)
    wr_pad = jnp.zeros((depth, D, LANE), f32).at[:, :, :N_EXPERTS].set(w_router)
    br_pad = jnp.full((depth, 1, LANE), -jnp.inf, f32).at[:, 0, :N_EXPERTS].set(b_router)
    sgu = jnp.concatenate([ws_gate, ws_up], axis=-1).astype(bf16)
    sd = ws_down.astype(bf16)

    h = jnp.concatenate([x.reshape(bt, D), ctx.reshape(bc, D)], axis=0)
    for l in range(depth):
        mods3 = mods[l].reshape(rows, 1, 6 * D)
        hy, q, kv, rg = _inproj(h, mods3, w_in_pad[l], seg, tm)

        xz_l = _dwconv(hy, hy_short_w[l], hy_short_b[l], T, 0, B, 0, 3)
        xz_c = _dwconv(hy, hy_short_w[l], hy_short_b[l], Lc, bt // Lc, B, 0, 3)
        fargs = (hy_f_w1[l], hy_f_b1[l], hy_f_freq[l], hy_f_w2[l], hy_f_b2[l], hy_f_w3[l])
        hy_l = _hyena_long(xz_l, _hyena_filters(T, *fargs), hy_skip[l], B, T)
        hy_c = _hyena_short_seq(xz_c, _hyena_filters(Lc, *fargs), hy_skip[l], B, Lc)
        hy_out = jnp.concatenate([hy_l, hy_c], axis=0)

        qp, kvp = _qkprep(q, kv, cos, sin, q_norm[l], k_norm[l], pos_block, tm)
        att = _attention(qp, kvp, B, T, Lc)

        rg_out = _rglru(rg, B, T, Lc, rg_conv_w[l], rg_conv_b[l], rg_lambda[l],
                        rg_w_a[l], rg_b_a[l], rg_w_x[l], rg_b_x[l])

        wr_hi, wr_lo = _split(wr_pad[l])
        h, m, gates, chosen, counts = _outproj(
            hy_out, att, rg_out, h, mods3, ln1_g[l][None, :], ln1_b[l][None, :],
            w_out_hy[l], w_out_att[l], w_out_rg[l], wr_hi, wr_lo, br_pad[l], moe_seg, moe_tm, alpha)
        h = _moe(m, gates, chosen, counts, l, w_gate, w_up, w_down, sgu[l], sd[l], h, mods3,
                 ln2_g[l][None, :], ln2_b[l][None, :], moe_seg, alpha)
    return h[:bt].reshape(B, T, D)
```

```python
import functools
import math

import numpy as np
import jax
import jax.numpy as jnp
from jax import lax
from jax.experimental import pallas as pl
from jax.experimental.pallas import tpu as pltpu

f32 = jnp.float32
bf16 = jnp.bfloat16

D_MODEL = 1024
GRID_W = 64
HY_W = 256
N_HEADS = 8
N_KV_HEADS = 2
N_GROUPS = N_HEADS // N_KV_HEADS
HEAD_DIM = 64
ATT_W = N_HEADS * HEAD_DIM
KV_W = N_KV_HEADS * HEAD_DIM
RG_W = 256
RG_BLOCKS = 4
HY_END = 3 * HY_W
Q_END = HY_END + ATT_W
K_END = Q_END + KV_W
V_END = K_END + KV_W
RGX_END = V_END + RG_W
PROJ_W = RGX_END + RG_W
HY_BANDS = 8
HY_EMB = 1 + 2 * HY_BANDS
HY_MAX_DECAY = math.log(1e-2) / 0.3
HY_MIN_DECAY = math.log(1e-2) / 1.5
ROPE_THETA = 10000.0
QK_EPS = 1e-6
RG_C = 8.0
N_EXPERTS = 64
TOP_K = 8
EXPERT_FF = 256
ROUTED_SCALE = 2.5
LN_EPS = 1e-6

LANE = 128
HEAD_PAD = LANE
QP_W = N_HEADS * HEAD_PAD
KVP_W = N_KV_HEADS * HEAD_PAD
PROJ_PAD_W = HY_END + QP_W + 2 * KVP_W + 2 * RG_W
VMEM_LIMIT = 52 * 1024 * 1024

FFT_N2 = 128
FFT_PASSES = 1
FILTER_PASSES = 3
TOKEN_TILE = 512
ATT_Q_TILE = 256
SCAN_TILE = 256
DISPATCH_TILE = 256
RUN_ALIGN = 16
RUN_CHUNK = 64
PIECE_SIZES = (RUN_CHUNK, 32, 16)
PLAN_ROWS = 8
PLAN_COUNT_ROW = 6
FFN_TILE = 1024


def _cparams(sem):
    return pltpu.CompilerParams(dimension_semantics=sem, vmem_limit_bytes=VMEM_LIMIT)


def _dot(a, b):
    return jnp.dot(a, b, preferred_element_type=f32)


def _split(x):
    hi = x.astype(bf16)
    lo = (x - hi.astype(f32)).astype(bf16)
    return hi, lo


def _np_split(x):
    x = jnp.asarray(np.asarray(x, np.float32))
    return _split(x)


def _dot3(a, w_hi, w_lo):
    a_hi, a_lo = _split(a)
    return _dot(a_hi, w_hi) + _dot(a_lo, w_hi) + _dot(a_hi, w_lo)


def _stack_dft(mat):
    re_hi, re_lo = _np_split(mat.real)
    im_hi, im_lo = _np_split(mat.imag)
    return jnp.concatenate([re_hi, im_hi, re_lo, im_lo], axis=0)


def _apply_stack(stack, x, passes):
    m2 = stack.shape[0] // 2
    x_hi, x_lo = _split(x)
    if passes == 1:
        return _dot(stack[:m2], x_hi)
    r = _dot(stack, x_hi)
    out = r[:m2] + r[m2:]
    if passes >= 3:
        out = out + _dot(stack[:m2], x_lo)
    return out


def _mod_kernel(a_ref, w_ref, b_ref, o_ref):
    a = a_ref[...]
    a = a * jax.nn.sigmoid(a)
    w_hi, w_lo = _split(w_ref[...])
    o_ref[...] = _dot3(a, w_hi, w_lo) + b_ref[...]


def _modulation(cond, w_mod, b_mod):
    depth, d, n = w_mod.shape
    rows = cond.shape[0]
    tn = 512
    return pl.pallas_call(
        _mod_kernel,
        out_shape=jax.ShapeDtypeStruct((depth, rows, n), f32),
        grid=(depth, n // tn),
        in_specs=[
            pl.BlockSpec((rows, d), lambda l, j: (0, 0)),
            pl.BlockSpec((None, d, tn), lambda l, j: (l, 0, j)),
            pl.BlockSpec((None, 1, tn), lambda l, j: (l, 0, j)),
        ],
        out_specs=pl.BlockSpec((None, rows, tn), lambda l, j: (l, 0, j)),
        compiler_params=_cparams(("parallel", "parallel")),
        name="modulation",
    )(cond, w_mod, b_mod.reshape(depth, 1, n))


def _inproj_kernel(h_ref, sh_ref, sc_ref, w_ref, ohy_ref, oq_ref, okv_ref, org_ref):
    a = h_ref[...] * (1.0 + sc_ref[...]) + sh_ref[...]
    p = _dot(a.astype(bf16), w_ref[...])
    c0, c1, c2 = HY_END, HY_END + QP_W, HY_END + QP_W + 2 * KVP_W
    ohy_ref[...] = p[:, :c0]
    oq_ref[...] = p[:, c0:c1]
    okv_ref[...] = p[:, c1:c2]
    org_ref[...] = p[:, c2:]


def _mod_spec(seg, j):
    return pl.BlockSpec((None, 1, D_MODEL), lambda i: (seg(i), 0, j))


def _inproj(h, mods3, w_pad, seg, tm):
    n = h.shape[0]
    widths = (HY_END, QP_W, 2 * KVP_W, 2 * RG_W)
    return pl.pallas_call(
        _inproj_kernel,
        out_shape=[jax.ShapeDtypeStruct((n, w), f32) for w in widths],
        grid=(n // tm,),
        in_specs=[
            pl.BlockSpec((tm, D_MODEL), lambda i: (i, 0)),
            _mod_spec(seg, 0),
            _mod_spec(seg, 1),
            pl.BlockSpec((D_MODEL, PROJ_PAD_W), lambda i: (0, 0)),
        ],
        out_specs=[pl.BlockSpec((tm, w), lambda i: (i, 0)) for w in widths],
        compiler_params=_cparams(("parallel",)),
        name="inproj",
    )(h, mods3, mods3, w_pad)


def _dwconv_kernel(x_ref, w_ref, b_ref, o_ref, *, width, left):
    x = x_ref[...]
    n = x.shape[0]
    row = lax.broadcasted_iota(jnp.int32, x.shape, 0)
    acc = jnp.zeros_like(x) + b_ref[...]
    for j in range(width):
        off = j - left
        if off == 0:
            xs = x
        else:
            xs = pltpu.roll(x, (-off) % n, axis=0)
            valid = jnp.logical_and(row + off >= 0, row + off < n)
            xs = jnp.where(valid, xs, 0.0)
        acc = acc + xs * w_ref[j:j + 1, :]
    o_ref[...] = acc


def _dwconv(x, w, b, seq_len, row_block0, n_seq, col_block0, n_col_blocks):
    width = w.shape[0]
    wp = jnp.zeros((8, w.shape[1]), f32).at[:width].set(w)
    ct = 256
    return pl.pallas_call(
        functools.partial(_dwconv_kernel, width=width, left=(width - 1) // 2),
        out_shape=jax.ShapeDtypeStruct((n_seq * seq_len, n_col_blocks * ct), f32),
        grid=(n_seq, n_col_blocks),
        in_specs=[
            pl.BlockSpec((seq_len, ct), lambda s, c: (row_block0 + s, col_block0 + c)),
            pl.BlockSpec((8, ct), lambda s, c: (0, c)),
            pl.BlockSpec((1, ct), lambda s, c: (0, c)),
        ],
        out_specs=pl.BlockSpec((seq_len, ct), lambda s, c: (s, c)),
        compiler_params=_cparams(("parallel", "parallel")),
        name="dwconv",
    )(x, wp, b.reshape(1, -1))


def _filter_kernel(z_ref, w1h_ref, w1l_ref, b1_ref, fr_ref, w2h_ref, w2l_ref, b2_ref,
                   w3h_ref, w3l_ref, win_ref, o_ref):
    fr = fr_ref[...]
    f = jnp.sin(fr * (_dot3(z_ref[...], w1h_ref[...], w1l_ref[...]) + b1_ref[...]))
    f = jnp.sin(fr * (_dot3(f, w2h_ref[...], w2l_ref[...]) + b2_ref[...]))
    f = _dot3(f, w3h_ref[...], w3l_ref[...])
    win = win_ref[...]
    o_ref[...] = f * jnp.concatenate([win] * 2, axis=1)


def _filter_features(L):
    t = np.linspace(0.0, 1.0, L, dtype=np.float32)[:, None].astype(np.float64)
    w = np.float32(2.0 * math.pi) * np.arange(L, dtype=np.float32)[:, None] / np.float32(L)
    bands = np.linspace(1e-4, HY_BANDS - 1, HY_BANDS, dtype=np.float32)
    arg = (bands * w).astype(np.float64)
    z = np.concatenate([t, np.cos(arg), -np.sin(arg)], axis=-1)
    deltas = np.abs(np.linspace(HY_MIN_DECAY, HY_MAX_DECAY, HY_W, dtype=np.float32)).astype(np.float64)
    win = np.exp(-t * deltas)
    zp = np.zeros((2 * L, LANE), np.float32)
    zp[:L, :HY_EMB] = z
    zp[L + 1:, :HY_EMB] = z[:0:-1]
    win2 = np.zeros((2 * L, HY_W), np.float32)
    win2[:L] = win
    win2[L + 1:] = win[:0:-1]
    return jnp.asarray(zp), jnp.asarray(win2)


def _pad2(w, rows, cols):
    return jnp.zeros((rows, cols), f32).at[:w.shape[0], :w.shape[1]].set(w)


def _hyena_filters(L, w1, b1, freq, w2, b2, w3):
    zp, win = _filter_features(L)
    w1h, w1l = _split(_pad2(w1, LANE, LANE))
    w2h, w2l = _split(_pad2(w2, LANE, LANE))
    w3h, w3l = _split(_pad2(w3, LANE, 4 * HY_W))
    b1p = _pad2(b1[None, :], 1, LANE)
    b2p = _pad2(b2[None, :], 1, LANE)
    frp = _pad2(freq[None, :], 1, LANE)
    tl = min(L, 512)
    half = L // tl
    full = lambda shape: pl.BlockSpec(shape, lambda i: (0, 0))
    w3_spec = pl.BlockSpec((LANE, 2 * HY_W), lambda i: (0, i // half))
    return pl.pallas_call(
        _filter_kernel,
        out_shape=jax.ShapeDtypeStruct((2 * L, 2 * HY_W), f32),
        grid=(2 * half,),
        in_specs=[
            pl.BlockSpec((tl, LANE), lambda i: (i, 0)),
            full((LANE, LANE)), full((LANE, LANE)), full((1, LANE)), full((1, LANE)),
            full((LANE, LANE)), full((LANE, LANE)), full((1, LANE)),
            w3_spec, w3_spec,
            pl.BlockSpec((tl, HY_W), lambda i: (i, 0)),
        ],
        out_specs=pl.BlockSpec((tl, 2 * HY_W), lambda i: (i, 0)),
        compiler_params=_cparams(("parallel",)),
        name="hyena_filter",
    )(zp, w1h, w1l, b1p, frp, w2h, w2l, b2p, w3h, w3l, win)


def _dft_consts(n):
    n1 = n // FFT_N2
    k = np.arange(n1)
    f_n1 = np.exp(-2j * np.pi * np.outer(k, k) / n1)
    k2 = np.arange(FFT_N2)
    f_n2 = np.exp(-2j * np.pi * np.outer(k2, k2) / FFT_N2)
    tw = np.exp(-2j * np.pi * np.outer(k, k2) / n).reshape(n, 1)
    tw_re = jnp.asarray(np.broadcast_to(tw.real, (n, LANE)).astype(np.float32))
    tw_im = jnp.asarray(np.broadcast_to(tw.imag, (n, LANE)).astype(np.float32))
    return n1, f_n1, f_n2, tw_re, tw_im


def _fa_kernel(u_ref, fs_ref, are_ref, aim_ref, *, grp, passes, real_only):
    fs = fs_ref[...]
    n1 = fs.shape[0] // 4
    jb, c = u_ref.shape[2], u_ref.shape[3]
    for g in range(jb // grp):
        def gather(bi):
            return jnp.concatenate([u_ref[bi, :, g * grp + jj, :] for jj in range(grp)], axis=1)
        p = _apply_stack(fs, gather(0), passes)
        if real_only:
            re, im = p[:n1], p[n1:]
        else:
            q = _apply_stack(fs, gather(1), passes)
            re, im = p[:n1] - q[n1:], p[n1:] + q[:n1]
        for jj in range(grp):
            are_ref[:, g * grp + jj, :] = re[:, jj * c:(jj + 1) * c]
            aim_ref[:, g * grp + jj, :] = im[:, jj * c:(jj + 1) * c]


def _fft_step_a(u4, fs, col_block, c, real_only, passes):
    s, n1_in = u4.shape[0], u4.shape[1]
    n1 = fs.shape[0] // 4
    per = 1 if real_only else 2
    jb = 32
    grp = max(1, 1024 // c)
    shape = jax.ShapeDtypeStruct((s // per, n1, FFT_N2, c), f32)
    return pl.pallas_call(
        functools.partial(_fa_kernel, grp=grp, passes=passes, real_only=real_only),
        out_shape=[shape, shape],
        grid=(s // per, FFT_N2 // jb),
        in_specs=[
            pl.BlockSpec((per, n1_in, jb, c), lambda p, j: (p, 0, j, col_block)),
            pl.BlockSpec(fs.shape, lambda p, j: (0, 0)),
        ],
        out_specs=[pl.BlockSpec((None, n1, jb, c), lambda p, j: (p, 0, j, 0))] * 2,
        compiler_params=_cparams(("parallel", "parallel")),
        name="fft_step_a",
    )(u4, fs)


def _mid_kernel(are_ref, aim_ref, twr_ref, twi_ref, f2_ref, *rest, kb, passes, fwd_only, inv_n):
    if fwd_only:
        ore_ref, oim_ref = rest
    else:
        kr_ref, ki_ref, ore_ref, oim_ref = rest
    f2 = f2_ref[...]
    c = are_ref.shape[1]
    for kk in range(kb):
        rows = pl.ds(kk * FFT_N2, FFT_N2)
        ar, ai = are_ref[rows, :], aim_ref[rows, :]
        tr = jnp.concatenate([twr_ref[rows, :]] * (c // LANE), axis=1)
        ti = jnp.concatenate([twi_ref[rows, :]] * (c // LANE), axis=1)
        xr = ar * tr - ai * ti
        xi = ar * ti + ai * tr
        p = _apply_stack(f2, xr, passes)
        q = _apply_stack(f2, xi, passes)
        sr = p[:FFT_N2] - q[FFT_N2:]
        si = p[FFT_N2:] + q[:FFT_N2]
        if fwd_only:
            ore_ref[rows, :] = sr
            oim_ref[rows, :] = si
        else:
            kr, ki = kr_ref[rows, :], ki_ref[rows, :]
            yr = sr * kr - si * ki
            yi = sr * ki + si * kr
            p2 = _apply_stack(f2, yr, passes)
            q2 = _apply_stack(f2, yi, passes)
            br = p2[:FFT_N2] + q2[FFT_N2:]
            bi = q2[:FFT_N2] - p2[FFT_N2:]
            ore_ref[rows, :] = (br * tr + bi * ti) * inv_n
            oim_ref[rows, :] = (bi * tr - br * ti) * inv_n


def _fft_mid(a_re, a_im, tw_re, tw_im, f2s, kf=None, order=0, passes=FFT_PASSES):
    p, n, c = a_re.shape
    kb = min(4, n // FFT_N2)
    rb = kb * FFT_N2
    fwd_only = kf is None
    blk = pl.BlockSpec((None, rb, c), lambda i, k: (i, k, 0))
    in_specs = [blk, blk,
                pl.BlockSpec((rb, LANE), lambda i, k: (k, 0)),
                pl.BlockSpec((rb, LANE), lambda i, k: (k, 0)),
                pl.BlockSpec(f2s.shape, lambda i, k: (0, 0))]
    args = [a_re, a_im, tw_re, tw_im, f2s]
    if not fwd_only:
        in_specs += [pl.BlockSpec((rb, c), lambda i, k: (k, order))] * 2
        args += list(kf)
    shape = jax.ShapeDtypeStruct((p, n, c), f32)
    return pl.pallas_call(
        functools.partial(_mid_kernel, kb=kb, passes=passes, fwd_only=fwd_only, inv_n=1.0 / n),
        out_shape=[shape, shape],
        grid=(p, n // rb),
        in_specs=in_specs,
        out_specs=[blk, blk],
        compiler_params=_cparams(("parallel", "parallel")),
        name="fft_mid",
    )(*args)


def _fai_kernel(bre_ref, bim_ref, gs_ref, u_ref, gate_ref, skip_ref, o_ref, *, grp, passes):
    gs = gs_ref[...]
    n1h = gs.shape[0] // 4
    jb, c = u_ref.shape[2], u_ref.shape[3]
    skip = skip_ref[...]
    for g in range(jb // grp):
        def gather(ref):
            return jnp.concatenate([ref[:, g * grp + jj, :] for jj in range(grp)], axis=1)
        p = _apply_stack(gs, gather(bre_ref), passes)
        q = _apply_stack(gs, gather(bim_ref), passes)
        ya = p[:n1h] - q[n1h:]
        yb = p[n1h:] + q[:n1h]
        for jj in range(grp):
            j = g * grp + jj
            for bi, y in ((0, ya), (1, yb)):
                u = u_ref[bi, :, j, :]
                o_ref[bi, :, j, :] = gate_ref[bi, :, j, :] * (y[:, jj * c:(jj + 1) * c] + u * skip)


def _fft_step_a_inv(b_re, b_im, gs, u4, u_col, gate4, gate_col, skip, passes):
    p, n1, _, c = b_re.shape
    n1h = n1 // 2
    jb = 32
    grp = max(1, 1024 // c)
    bspec = pl.BlockSpec((None, n1, jb, c), lambda i, j: (i, 0, j, 0))
    return pl.pallas_call(
        functools.partial(_fai_kernel, grp=grp, passes=passes),
        out_shape=jax.ShapeDtypeStruct((2 * p, n1h, FFT_N2, c), f32),
        grid=(p, FFT_N2 // jb),
        in_specs=[
            bspec, bspec,
            pl.BlockSpec(gs.shape, lambda i, j: (0, 0)),
            pl.BlockSpec((2, n1h, jb, c), lambda i, j: (i, 0, j, u_col)),
            pl.BlockSpec((2, n1h, jb, c), lambda i, j: (i, 0, j, gate_col)),
            pl.BlockSpec((1, c), lambda i, j: (0, 0)),
        ],
        out_specs=pl.BlockSpec((2, n1h, jb, c), lambda i, j: (i, 0, j, 0)),
        compiler_params=_cparams(("parallel", "parallel")),
        name="fft_step_a_inv",
    )(b_re, b_im, gs, u4, gate4, skip)


def _hyena_long(xz, filt, skip, n_seq, L, passes=FFT_PASSES):
    n = 2 * L
    n1, f_n1, f_n2, tw_re, tw_im = _dft_consts(n)
    n1h = n1 // 2
    c = HY_W
    fs_full = _stack_dft(f_n1)
    fs_half = _stack_dft(f_n1[:, :n1h])
    gs = _stack_dft(np.conj(f_n1)[:n1h, :])
    f2s = _stack_dft(f_n2)
    kc4 = filt.reshape(1, n1, FFT_N2, 2 * c)
    k_re, k_im = _fft_step_a(kc4, fs_full, 0, 2 * c, True, FILTER_PASSES)
    kf = _fft_mid(k_re.reshape(1, n, 2 * c), k_im.reshape(1, n, 2 * c), tw_re, tw_im, f2s,
                  passes=FILTER_PASSES)
    kf = (kf[0].reshape(n, 2 * c), kf[1].reshape(n, 2 * c))
    xz4 = xz.reshape(n_seq, n1h, FFT_N2, 3 * c)
    z4, z_col = xz4, 2
    for order in range(2):
        a_re, a_im = _fft_step_a(z4, fs_half, z_col, c, False, passes)
        p = n_seq // 2
        b_re, b_im = _fft_mid(a_re.reshape(p, n, c), a_im.reshape(p, n, c), tw_re, tw_im, f2s,
                              kf=kf, order=order, passes=passes)
        z4 = _fft_step_a_inv(b_re.reshape(p, n1, FFT_N2, c), b_im.reshape(p, n1, FFT_N2, c), gs,
                             z4, z_col, xz4, order, skip[order:order + 1], passes)
        z_col = 0
    return z4.reshape(n_seq * L, c)


def _dense_spec_kernel(k_ref, fs_ref, ore_ref, oim_ref, *, passes):
    n = k_ref.shape[0]
    p = _apply_stack(fs_ref[...], k_ref[...], passes)
    ore_ref[...] = p[:n]
    oim_ref[...] = p[n:]


def _dense_conv_kernel(xz_ref, fs_ref, gs_ref, kr_ref, ki_ref, skip_ref, o_ref, *, passes):
    L = xz_ref.shape[1]
    n = 2 * L
    c = HY_W
    fs, gs = fs_ref[...], gs_ref[...]
    za, zb = xz_ref[0, :, 2 * c:], xz_ref[1, :, 2 * c:]
    for order in range(2):
        p = _apply_stack(fs, za, passes)
        q = _apply_stack(fs, zb, passes)
        sr, si = p[:n] - q[n:], p[n:] + q[:n]
        kr, ki = kr_ref[:, order * c:(order + 1) * c], ki_ref[:, order * c:(order + 1) * c]
        yr, yi = sr * kr - si * ki, sr * ki + si * kr
        p2 = _apply_stack(gs, yr, passes)
        q2 = _apply_stack(gs, yi, passes)
        ya = (p2[:L] - q2[L:]) * (1.0 / n)
        yb = (p2[L:] + q2[:L]) * (1.0 / n)
        skip = skip_ref[order:order + 1, :]
        za = xz_ref[0, :, order * c:(order + 1) * c] * (ya + za * skip)
        zb = xz_ref[1, :, order * c:(order + 1) * c] * (yb + zb * skip)
    o_ref[0] = za
    o_ref[1] = zb


def _hyena_short_seq(xz, filt, skip, n_seq, L, passes=FFT_PASSES):
    n = 2 * L
    c = HY_W
    k = np.arange(n)
    f_n = np.exp(-2j * np.pi * np.outer(k, k) / n)
    fs_full = _stack_dft(f_n)
    fs_half = _stack_dft(f_n[:, :L])
    gs = _stack_dft(np.conj(f_n)[:L, :])
    kc = filt
    shape = jax.ShapeDtypeStruct((n, 2 * c), f32)
    k_re, k_im = pl.pallas_call(
        functools.partial(_dense_spec_kernel, passes=FILTER_PASSES),
        out_shape=[shape, shape],
        compiler_params=_cparams(None),
        name="dense_filter_spectrum",
    )(kc, fs_full)
    skip_p = jnp.zeros((8, c), f32).at[:2].set(skip)
    full = lambda a: pl.BlockSpec(a.shape, lambda i: (0,) * a.ndim)
    out = pl.pallas_call(
        functools.partial(_dense_conv_kernel, passes=passes),
        out_shape=jax.ShapeDtypeStruct((n_seq, L, c), f32),
        grid=(n_seq // 2,),
        in_specs=[pl.BlockSpec((2, L, 3 * c), lambda i: (i, 0, 0)),
                  full(fs_half), full(gs), full(k_re), full(k_im), full(skip_p)],
        out_specs=pl.BlockSpec((2, L, c), lambda i: (i, 0, 0)),
        compiler_params=_cparams(("parallel",)),
        name="dense_long_conv",
    )(xz.reshape(n_seq, L, 3 * c), fs_half, gs, k_re, k_im, skip_p)
    return out.reshape(n_seq * L, c)


def _rope_tables(T, tm):
    n_rows = T // GRID_W
    row = np.repeat(np.arange(n_rows, dtype=np.float32), GRID_W)
    col = np.tile(np.arange(GRID_W, dtype=np.float32), n_rows)
    axis_dim = HEAD_DIM // 2
    inv_freq = np.float32(ROPE_THETA) ** (-np.arange(0, axis_dim, 2, dtype=np.float32) / np.float32(axis_dim))
    ang = np.concatenate([row[:, None] * inv_freq, col[:, None] * inv_freq], axis=-1).astype(np.float64)
    cos = np.zeros((T + tm, HEAD_PAD), np.float32)
    sin = np.zeros((T + tm, HEAD_PAD), np.float32)
    cos[:T, 0:HEAD_DIM:2] = np.cos(ang)
    cos[:T, 1:HEAD_DIM:2] = np.cos(ang)
    sin[:T, 0:HEAD_DIM:2] = -np.sin(ang)
    sin[:T, 1:HEAD_DIM:2] = np.sin(ang)
    cos[T:, :HEAD_DIM] = 1.0
    return jnp.asarray(cos), jnp.asarray(sin)


def _qkprep_kernel(q_ref, kv_ref, cos_ref, sin_ref, qg_ref, kg_ref, avg_ref, oq_ref, okv_ref):
    cos, sin = cos_ref[...], sin_ref[...]
    avg = avg_ref[...]
    lane = lax.broadcasted_iota(jnp.int32, cos.shape, 1)
    even = (lane % 2) == 0

    def norm_rope(x, gain, scale):
        sq_hi, sq_lo = _split(x * x)
        ms = _dot(sq_hi, avg) + _dot(sq_lo, avg)
        xn = x * lax.rsqrt(ms + QK_EPS) * gain
        swapped = jnp.where(even, pltpu.roll(xn, LANE - 1, axis=1), pltpu.roll(xn, 1, axis=1))
        return ((xn * cos + swapped * sin) * scale).astype(bf16)

    for h in range(N_HEADS):
        sl = slice(h * HEAD_PAD, (h + 1) * HEAD_PAD)
        oq_ref[:, sl] = norm_rope(q_ref[:, sl], qg_ref[...], HEAD_DIM ** -0.5)
    for g in range(N_KV_HEADS):
        sl = slice(g * HEAD_PAD, (g + 1) * HEAD_PAD)
        okv_ref[:, sl] = norm_rope(kv_ref[:, sl], kg_ref[...], 1.0)
    okv_ref[:, KVP_W:] = kv_ref[:, KVP_W:].astype(bf16)


def _qkprep(q, kv, cos, sin, q_gain, k_gain, pos_block, tm):
    n = q.shape[0]
    pad = lambda g: jnp.zeros((1, HEAD_PAD), f32).at[0, :HEAD_DIM].set(g)
    avg = jnp.full((HEAD_PAD, HEAD_PAD), 1.0 / HEAD_DIM, bf16)
    one = lambda shape: pl.BlockSpec(shape, lambda i: (0, 0))
    return pl.pallas_call(
        _qkprep_kernel,
        out_shape=[jax.ShapeDtypeStruct((n, QP_W), bf16), jax.ShapeDtypeStruct((n, 2 * KVP_W), bf16)],
        grid=(n // tm,),
        in_specs=[
            pl.BlockSpec((tm, QP_W), lambda i: (i, 0)),
            pl.BlockSpec((tm, 2 * KVP_W), lambda i: (i, 0)),
            pl.BlockSpec((tm, HEAD_PAD), lambda i: (pos_block(i), 0)),
            pl.BlockSpec((tm, HEAD_PAD), lambda i: (pos_block(i), 0)),
            one((1, HEAD_PAD)), one((1, HEAD_PAD)), one((HEAD_PAD, HEAD_PAD)),
        ],
        out_specs=[pl.BlockSpec((tm, QP_W), lambda i: (i, 0)),
                   pl.BlockSpec((tm, 2 * KVP_W), lambda i: (i, 0))],
        compiler_params=_cparams(("parallel",)),
        name="qk_prep",
    )(q, kv, cos, sin, pad(q_gain), pad(k_gain), avg)


def _attn_kernel(q_ref, kvl_ref, kvc_ref, o_ref, *, nq_lat):
    i = pl.program_id(1)
    dims = (((1,), (1,)), ((), ()))

    def scores(h, use_lat):
        g = h // N_GROUPS
        ks = slice(g * HEAD_PAD, (g + 1) * HEAD_PAD)
        q = q_ref[:, h * HEAD_PAD:(h + 1) * HEAD_PAD]
        sc = lax.dot_general(q, kvc_ref[:, ks], dims, preferred_element_type=f32)
        sl = lax.dot_general(q, kvl_ref[:, ks], dims, preferred_element_type=f32) if use_lat else None
        return sl, sc

    def heads(use_lat):
        nxt = scores(0, use_lat)
        for h in range(N_HEADS):
            sl, sc = nxt
            if h + 1 < N_HEADS:
                nxt = scores(h + 1, use_lat)
            g = h // N_GROUPS
            vs = slice(KVP_W + g * HEAD_PAD, KVP_W + (g + 1) * HEAD_PAD)
            m = jnp.max(sc, axis=-1, keepdims=True)
            if use_lat:
                m = jnp.maximum(m, jnp.max(sl, axis=-1, keepdims=True))
                p_l = jnp.exp(sl - m)
                denom = jnp.sum(p_l, axis=-1, keepdims=True)
                acc = _dot(p_l.astype(bf16), kvl_ref[:, vs])
            p_c = jnp.exp(sc - m)
            if use_lat:
                denom = denom + jnp.sum(p_c, axis=-1, keepdims=True)
                acc = acc + _dot(p_c.astype(bf16), kvc_ref[:, vs])
            else:
                denom = jnp.sum(p_c, axis=-1, keepdims=True)
                acc = _dot(p_c.astype(bf16), kvc_ref[:, vs])
            o_ref[:, h * HEAD_PAD:(h + 1) * HEAD_PAD] = (acc / denom).astype(bf16)

    @pl.when(i < nq_lat)
    def _():
        heads(True)

    @pl.when(i == nq_lat)
    def _():
        heads(False)


def _attention(qp, kvp, B, T, Lc):
    n = qp.shape[0]
    tq = Lc
    nq = T // tq
    q_idx = lambda b, i: (jnp.where(i < nq, b * nq + i, B * nq + b), 0)
    return pl.pallas_call(
        functools.partial(_attn_kernel, nq_lat=nq),
        out_shape=jax.ShapeDtypeStruct((n, QP_W), bf16),
        grid=(B, nq + 1),
        in_specs=[
            pl.BlockSpec((tq, QP_W), q_idx),
            pl.BlockSpec((T, 2 * KVP_W), lambda b, i: (b, 0)),
            pl.BlockSpec((Lc, 2 * KVP_W), lambda b, i: (B * nq + b, 0)),
        ],
        out_specs=pl.BlockSpec((tq, QP_W), q_idx),
        compiler_params=_cparams(("parallel", "arbitrary")),
        name="attention",
    )(qp, kvp, kvp)


def _scan_kernel(u_ref, wh_ref, wl_ref, bias_ref, lam_ref, h0_ref, *rest, reverse, final):
    if final:
        hprev_ref, g_ref, o_ref, hend_ref, carry_ref = rest
    else:
        o_ref, hend_ref, carry_ref = rest
    t = pl.program_id(1)

    @pl.when(t == 0)
    def _():
        carry_ref[...] = h0_ref[...]

    u = u_ref[...]
    tt = u.shape[0]
    gates = _dot3(u, wh_ref[...], wl_ref[...]) + bias_ref[...]
    r = jax.nn.sigmoid(gates[:, :RG_W])
    ig = jax.nn.sigmoid(gates[:, RG_W:])
    lam = lam_ref[...]
    softplus = jnp.maximum(-lam, 0.0) + jnp.log1p(jnp.exp(-jnp.abs(lam)))
    log_a = -RG_C * r * softplus
    a = jnp.exp(log_a)
    b = jnp.sqrt(-jnp.tanh(log_a) * (a * a + 1.0)) * (ig * u)
    row = lax.broadcasted_iota(jnp.int32, a.shape, 0)
    s = 1
    while s < tt:
        if reverse:
            a_s, b_s = pltpu.roll(a, tt - s, axis=0), pltpu.roll(b, tt - s, axis=0)
            valid = row < tt - s
        else:
            a_s, b_s = pltpu.roll(a, s, axis=0), pltpu.roll(b, s, axis=0)
            valid = row >= s
        b = a * jnp.where(valid, b_s, 0.0) + b
        a = a * jnp.where(valid, a_s, 1.0)
        s *= 2
    h = a * carry_ref[...] + b
    last = h[0:1, :] if reverse else h[tt - 1:tt, :]
    carry_ref[...] = last
    hend_ref[...] = last
    if final:
        o_ref[...] = ((hprev_ref[...] + h) * jax.nn.gelu(g_ref[...], approximate=True)).astype(o_ref.dtype)
    else:
        o_ref[...] = h


def _rg_scan(u, n_seq, L, w_hi, w_lo, bias, lam, h0, reverse, hprev=None, gate=None, gate_row0=0):
    tt = min(SCAN_TILE, L)
    nt = L // tt
    final = hprev is not None
    tidx = (lambda t: nt - 1 - t) if reverse else (lambda t: t)
    row = lambda b, t: (b * nt + tidx(t), 0)
    one = lambda shape: pl.BlockSpec(shape, lambda b, t: (0, 0))
    in_specs = [pl.BlockSpec((tt, RG_W), row), one(w_hi.shape), one(w_lo.shape), one((1, 2 * RG_W)),
                one((1, RG_W)), pl.BlockSpec((None, 1, RG_W), lambda b, t: (b, 0, 0))]
    args = [u, w_hi, w_lo, bias, lam, h0]
    if final:
        in_specs += [pl.BlockSpec((tt, RG_W), row),
                     pl.BlockSpec((tt, RG_W), lambda b, t: (gate_row0 // tt + b * nt + tidx(t), 1))]
        args += [hprev, gate]
    return pl.pallas_call(
        functools.partial(_scan_kernel, reverse=reverse, final=final),
        out_shape=[jax.ShapeDtypeStruct((n_seq * L, RG_W), bf16 if final else f32),
                   jax.ShapeDtypeStruct((n_seq, 1, RG_W), f32)],
        grid=(n_seq, nt),
        in_specs=in_specs,
        out_specs=[pl.BlockSpec((tt, RG_W), row), pl.BlockSpec((None, 1, RG_W), lambda b, t: (b, 0, 0))],
        scratch_shapes=[pltpu.VMEM((1, RG_W), f32)],
        compiler_params=_cparams(("parallel", "arbitrary")),
        name="rg_scan",
    )(*args)


def _block_diag(w):
    bw = w.shape[-1]
    out = jnp.zeros((RG_W, RG_W), f32)
    for h in range(RG_BLOCKS):
        out = out.at[h * bw:(h + 1) * bw, h * bw:(h + 1) * bw].set(w[h])
    return out


def _rglru(rg, B, T, Lc, conv_w, conv_b, lam, w_a, b_a, w_x, b_x):
    bt = B * T
    u_l = _dwconv(rg, conv_w, conv_b, T, 0, B, 0, 1)
    u_c = _dwconv(rg, conv_w, conv_b, Lc, bt // Lc, B, 0, 1)
    zeros = jnp.zeros((B, 1, RG_W), f32)
    prev_l = prev_c = None
    for d, rev in enumerate((False, True)):
        w_hi, w_lo = _split(jnp.concatenate([_block_diag(w_a[d]), _block_diag(w_x[d])], axis=1))
        bias = jnp.concatenate([b_a[d], b_x[d]])[None, :]
        lam_d = lam[d][None, :]
        last = d == 1
        kw_c = dict(hprev=prev_c, gate=rg, gate_row0=bt) if last else {}
        kw_l = dict(hprev=prev_l, gate=rg, gate_row0=0) if last else {}
        prev_c, h_end = _rg_scan(u_c, B, Lc, w_hi, w_lo, bias, lam_d, zeros, rev, **kw_c)
        prev_l, _ = _rg_scan(u_l, B, T, w_hi, w_lo, bias, lam_d, h_end, rev, **kw_l)
    return jnp.concatenate([prev_l, prev_c], axis=0)


def _layer_norm(v, g, b):
    mu = jnp.mean(v, axis=-1, keepdims=True)
    d = v - mu
    var = jnp.mean(d * d, axis=-1, keepdims=True)
    return d * lax.rsqrt(var + LN_EPS) * g + b


def _outproj_kernel(hy_ref, att_ref, rg_ref, h_ref, g1_ref, sh2_ref, sc2_ref, lng_ref, lnb_ref,
                    why_ref, watt_ref, wrg_ref, wrh_ref, wrl_ref, br_ref,
                    oh_ref, om_ref, og_ref, oi_ref, oc_ref, *, alpha):
    y = (_dot(hy_ref[...].astype(bf16), why_ref[...]) + _dot(att_ref[...], watt_ref[...])
         + _dot(rg_ref[...], wrg_ref[...]))
    hn = _layer_norm(alpha * h_ref[...] + g1_ref[...] * y, lng_ref[...], lnb_ref[...])
    oh_ref[...] = hn
    m = hn * (1.0 + sc2_ref[...]) + sh2_ref[...]
    om_ref[...] = m.astype(bf16)
    scores = jax.nn.sigmoid(_dot3(m, wrh_ref[...], wrl_ref[...]))
    sel = scores + br_ref[...]
    lane = lax.broadcasted_iota(jnp.int32, sel.shape, 1)
    picked = jnp.zeros_like(scores)
    chosen = jnp.zeros(sel.shape, jnp.int32)
    count = jnp.zeros((1, LANE), f32)
    for k in range(TOP_K):
        mx = jnp.max(sel, axis=-1, keepdims=True)
        first = jnp.min(jnp.where(sel == mx, lane, LANE), axis=-1, keepdims=True)
        hit = lane == first
        picked = jnp.where(hit, scores, picked)
        sel = jnp.where(hit, -jnp.inf, sel)
        chosen = jnp.where(lane == k, first, chosen)
        count = count + jnp.sum(jnp.where(hit, 1.0, 0.0), axis=0, keepdims=True)
    og_ref[...] = picked / jnp.sum(picked, axis=-1, keepdims=True) * ROUTED_SCALE
    oi_ref[...] = chosen
    oc_ref[...] = count


def _outproj(hy, att, rg, h, mods3, ln_g, ln_b, w_hy, w_att, w_rg, wr_hi, wr_lo, b_router, seg, tm, alpha):
    n = h.shape[0]
    row = lambda w: pl.BlockSpec((tm, w), lambda i: (i, 0))
    one = lambda a: pl.BlockSpec(a.shape, lambda i: (0,) * a.ndim)
    return pl.pallas_call(
        functools.partial(_outproj_kernel, alpha=alpha),
        out_shape=[jax.ShapeDtypeStruct((n, D_MODEL), f32), jax.ShapeDtypeStruct((n, D_MODEL), bf16),
                   jax.ShapeDtypeStruct((n, LANE), f32), jax.ShapeDtypeStruct((n, LANE), jnp.int32),
                   jax.ShapeDtypeStruct((n // tm, 1, LANE), f32)],
        grid=(n // tm,),
        in_specs=[row(HY_W), row(QP_W), row(RG_W), row(D_MODEL),
                  _mod_spec(seg, 2), _mod_spec(seg, 3), _mod_spec(seg, 4),
                  one(ln_g), one(ln_b), one(w_hy), one(w_att), one(w_rg), one(wr_hi), one(wr_lo),
                  one(b_router)],
        out_specs=[row(D_MODEL), row(D_MODEL), row(LANE), row(LANE),
                   pl.BlockSpec((None, 1, LANE), lambda i: (i, 0, 0))],
        compiler_params=_cparams(("parallel",)),
        name="outproj_router",
    )(hy, att, rg, h, mods3, mods3, mods3, ln_g, ln_b, w_hy, w_att, w_rg, wr_hi, wr_lo, b_router)


def _xs_rows(tile):
    return -(-(TOP_K * tile + N_EXPERTS * (RUN_ALIGN - 1)) // 256) * 256


def _perm_matrix(idx, runstart, gates):
    t = idx.shape[0]
    lane = lax.broadcasted_iota(jnp.int32, (t, LANE), 1)
    hits = [lane == idx[:, k:k + 1] for k in range(TOP_K)]
    sel = jnp.zeros((t, LANE), f32)
    for hit in hits:
        sel = jnp.where(hit, 1.0, sel)
    earlier = (lax.broadcasted_iota(jnp.int32, (t, t), 1) < lax.broadcasted_iota(jnp.int32, (t, t), 0))
    rank = _dot(jnp.where(earlier, 1.0, 0.0).astype(bf16), sel.astype(bf16))
    posmat = rank + runstart
    slot = lax.broadcasted_iota(jnp.int32, (t, _xs_rows(t)), 1).astype(jnp.int16)
    acc = jnp.zeros(slot.shape, bf16)
    for hit in hits:
        pos = jnp.sum(jnp.where(hit, posmat, 0.0), axis=-1, keepdims=True).astype(jnp.int32).astype(jnp.int16)
        if gates is None:
            val = jnp.ones((t, 1), bf16)
        else:
            val = jnp.sum(jnp.where(hit, gates, 0.0), axis=-1, keepdims=True).astype(bf16)
        acc = jnp.where(slot == pos, val, acc)
    return acc


def _for_pieces(plan_ref, fn):
    for k, size in enumerate(PIECE_SIZES):
        def body(j, carry, k=k, size=size):
            fn(pl.multiple_of(plan_ref[2 * k, j], RUN_ALIGN), pl.multiple_of(plan_ref[2 * k + 1, j], RUN_ALIGN),
               size)
            return carry

        lax.fori_loop(0, plan_ref[PLAN_COUNT_ROW, k], body, 0)


def _dispatch_kernel(gap_ref, reg_ref, x_ref, idx_ref, rsf_ref, plan_ref, prev_ref, s_ref,
                     xs_ref, zero_ref, sem):
    i = pl.program_id(0)
    slot = i % 2

    @pl.when(i == 0)
    def _():
        zero_ref[...] = jnp.zeros(zero_ref.shape, zero_ref.dtype)

        def fill(action):
            def body(e, carry):
                @pl.when(reg_ref[e] > 0)
                def _():
                    dst = pl.multiple_of(gap_ref[e], RUN_ALIGN)
                    action(pltpu.make_async_copy(zero_ref, s_ref.at[pl.ds(dst, FFN_TILE)], sem))
                return carry
            lax.fori_loop(0, N_EXPERTS, body, 0)

        fill(lambda cp: cp.start())
        fill(lambda cp: cp.wait())

    p_t = _perm_matrix(idx_ref[...], rsf_ref[...], None)
    xs_ref[slot] = lax.dot_general(p_t, x_ref[...], (((0,), (0,)), ((), ())),
                                   preferred_element_type=f32).astype(bf16)

    def copy(buf, src, dst, sz):
        return pltpu.make_async_copy(xs_ref.at[buf].at[pl.ds(src, sz)], s_ref.at[pl.ds(dst, sz)], sem)

    @pl.when(i > 0)
    def _():
        _for_pieces(prev_ref, lambda s, d, sz: copy(1 - slot, s, d, sz).wait())

    _for_pieces(plan_ref, lambda s, d, sz: copy(slot, s, d, sz).start())

    @pl.when(i == pl.num_programs(0) - 1)
    def _():
        _for_pieces(plan_ref, lambda s, d, sz: copy(slot, s, d, sz).wait())


def _ffn_kernel(te_ref, nu_ref, x_ref, wg_ref, wu_ref, wd_ref, o_ref):
    @pl.when(pl.program_id(0) < nu_ref[0])
    def _():
        x = x_ref[...]
        hid = jax.nn.silu(_dot(x, wg_ref[...].astype(bf16))) * _dot(x, wu_ref[...].astype(bf16))
        o_ref[...] = _dot(hid.astype(bf16), wd_ref[...].astype(bf16)).astype(bf16)


def _combine_kernel(z_ref, plan_ref, idx_ref, gate_ref, rsf_ref, x_ref, sgu_ref, sd_ref,
                    h_ref, g2_ref, lng_ref, lnb_ref, o_ref, zbuf_ref, sem, *, alpha):
    i = pl.program_id(0)

    @pl.when(i == 0)
    def _():
        zbuf_ref[...] = jnp.zeros(zbuf_ref.shape, zbuf_ref.dtype)

    def copy(src, dst, sz):
        return pltpu.make_async_copy(z_ref.at[pl.ds(dst, sz)], zbuf_ref.at[pl.ds(src, sz)], sem)

    _for_pieces(plan_ref, lambda s, d, sz: copy(s, d, sz).start())
    hs = _dot(x_ref[...], sgu_ref[...])
    shared = _dot((jax.nn.silu(hs[:, :EXPERT_FF]) * hs[:, EXPERT_FF:]).astype(bf16), sd_ref[...])
    w_t = _perm_matrix(idx_ref[...], rsf_ref[...], gate_ref[...])
    _for_pieces(plan_ref, lambda s, d, sz: copy(s, d, sz).wait())
    y = _dot(w_t, zbuf_ref[...]) + shared
    o_ref[...] = _layer_norm(alpha * h_ref[...] + g2_ref[...] * y, lng_ref[...], lnb_ref[...])


def _route_plan(counts, n_ffn_tiles):
    c = counts[:, 0, :N_EXPERTS].astype(jnp.int32)
    lens = (c + RUN_ALIGN - 1) // RUN_ALIGN * RUN_ALIGN
    runstart = jnp.cumsum(lens, axis=1) - lens
    region = (jnp.sum(lens, axis=0) + FFN_TILE - 1) // FFN_TILE * FFN_TILE
    region_end = jnp.cumsum(region)
    dest = (region_end - region)[None, :] + jnp.cumsum(lens, axis=0) - lens
    n_used = region_end[-1] // FFN_TILE
    tile_row = jnp.minimum(jnp.arange(n_ffn_tiles, dtype=jnp.int32), n_used - 1) * FFN_TILE
    tile_expert = jnp.minimum(jnp.sum(tile_row[:, None] >= region_end[None, :], axis=1), N_EXPERTS - 1)
    runstart_f = jnp.zeros((c.shape[0], 1, LANE), f32).at[:, 0, :N_EXPERTS].set(runstart.astype(f32))

    def compact(mask, *vals):
        pos = jnp.cumsum(mask, axis=1) - mask
        hit = jnp.logical_and(mask[:, :, None] > 0, pos[:, :, None] == jnp.arange(LANE, dtype=jnp.int32))
        return [jnp.sum(jnp.where(hit, v[:, :, None], 0), axis=1) for v in vals], jnp.sum(mask, axis=1)

    n_chunks = lens // RUN_CHUNK
    j = jnp.arange(DISPATCH_TILE // RUN_CHUNK, dtype=jnp.int32) * RUN_CHUNK
    wide = lambda v: (v[:, :, None] + j).reshape(v.shape[0], -1)
    chunk_mask = (j < (n_chunks * RUN_CHUNK)[:, :, None]).astype(jnp.int32).reshape(lens.shape[0], -1)
    (s64, d64), c64 = compact(chunk_mask, wide(runstart), wide(dest))
    off = n_chunks * RUN_CHUNK
    (s32, d32), c32 = compact((lens >> 5) & 1, runstart + off, dest + off)
    off = off + (lens & 32)
    (s16, d16), c16 = compact((lens >> 4) & 1, runstart + off, dest + off)
    count_row = jnp.zeros_like(s64).at[:, 0].set(c64).at[:, 1].set(c32).at[:, 2].set(c16)
    pieces = jnp.stack([s64, d64, s32, d32, s16, d16, count_row, jnp.zeros_like(s64)], axis=1)
    return dict(pieces=pieces.astype(jnp.int32), gap=region_end - FFN_TILE, region=region,
                n_used=n_used.reshape(1), tile_expert=tile_expert.astype(jnp.int32), runstart_f=runstart_f)


def _moe(m, gates, idx, counts, layer, w_gate, w_up, w_down, sgu, sd, h, mods3, ln_g, ln_b, seg, alpha):
    n = h.shape[0]
    t = DISPATCH_TILE
    nt = n // t
    xs_rows = _xs_rows(t)
    rows_max = TOP_K * n + nt * N_EXPERTS * (RUN_ALIGN - 1) + N_EXPERTS * (FFN_TILE - RUN_ALIGN)
    n_ffn_tiles = -(-rows_max // FFN_TILE)
    plan = _route_plan(counts, n_ffn_tiles)
    tile = lambda w: pl.BlockSpec((t, w), lambda i, *_: (i, 0))
    rsf_spec = pl.BlockSpec((None, 1, LANE), lambda i, *_: (i, 0, 0))
    one = lambda a: pl.BlockSpec(a.shape, lambda i, *_: (0,) * a.ndim)
    any_spec = pl.BlockSpec(memory_space=pl.ANY)

    pieces = plan["pieces"]
    plan_spec = pl.BlockSpec((None, PLAN_ROWS, LANE), lambda i, *_: (i, 0, 0), memory_space=pltpu.SMEM)
    prev_spec = pl.BlockSpec((None, PLAN_ROWS, LANE), lambda i, *_: (jnp.maximum(i - 1, 0), 0, 0),
                             memory_space=pltpu.SMEM)
    sorted_x = pl.pallas_call(
        _dispatch_kernel,
        out_shape=jax.ShapeDtypeStruct((n_ffn_tiles * FFN_TILE, D_MODEL), bf16),
        grid_spec=pltpu.PrefetchScalarGridSpec(
            num_scalar_prefetch=2, grid=(nt,),
            in_specs=[tile(D_MODEL), tile(LANE), rsf_spec, plan_spec, prev_spec],
            out_specs=any_spec,
            scratch_shapes=[pltpu.VMEM((2, xs_rows, D_MODEL), bf16), pltpu.VMEM((FFN_TILE, D_MODEL), bf16),
                            pltpu.SemaphoreType.DMA]),
        compiler_params=_cparams(("arbitrary",)),
        name="moe_dispatch",
    )(plan["gap"], plan["region"], m, idx, plan["runstart_f"], pieces, pieces)

    used = lambda i, te, nu: jnp.minimum(i, nu[0] - 1)
    sorted_z = pl.pallas_call(
        _ffn_kernel,
        out_shape=jax.ShapeDtypeStruct((n_ffn_tiles * FFN_TILE, D_MODEL), bf16),
        grid_spec=pltpu.PrefetchScalarGridSpec(
            num_scalar_prefetch=2, grid=(n_ffn_tiles,),
            in_specs=[pl.BlockSpec((FFN_TILE, D_MODEL), lambda i, te, nu: (used(i, te, nu), 0)),
                      pl.BlockSpec((None, None, D_MODEL, EXPERT_FF), lambda i, te, nu: (layer, te[i], 0, 0)),
                      pl.BlockSpec((None, None, D_MODEL, EXPERT_FF), lambda i, te, nu: (layer, te[i], 0, 0)),
                      pl.BlockSpec((None, None, EXPERT_FF, D_MODEL), lambda i, te, nu: (layer, te[i], 0, 0))],
            out_specs=pl.BlockSpec((FFN_TILE, D_MODEL), lambda i, te, nu: (used(i, te, nu), 0))),
        compiler_params=_cparams(("arbitrary",)),
        name="moe_ffn",
    )(plan["tile_expert"], plan["n_used"], sorted_x, w_gate, w_up, w_down)

    return pl.pallas_call(
        functools.partial(_combine_kernel, alpha=alpha),
        out_shape=jax.ShapeDtypeStruct((n, D_MODEL), f32),
        grid=(nt,),
        in_specs=[any_spec, plan_spec, tile(LANE), tile(LANE), rsf_spec, tile(D_MODEL), one(sgu), one(sd),
                  tile(D_MODEL), pl.BlockSpec((None, 1, D_MODEL), lambda i, *_: (seg(i), 0, 5)),
                  one(ln_g), one(ln_b)],
        out_specs=tile(D_MODEL),
        scratch_shapes=[pltpu.VMEM((xs_rows, D_MODEL), bf16), pltpu.SemaphoreType.DMA],
        compiler_params=_cparams(("arbitrary",)),
        name="moe_combine",
    )(sorted_z, pieces, idx, gates, plan["runstart_f"], m, sgu, sd, h, mods3, ln_g, ln_b)


def _pad_heads_cols(w, n_heads):
    lead = w.shape[:-1]
    w = w.reshape(*lead, n_heads, HEAD_DIM)
    w = jnp.concatenate([w, jnp.zeros_like(w)], axis=-1)
    return w.reshape(*lead, n_heads * HEAD_PAD)


def _pad_in_proj(w_in):
    hy, q, k, v, rgx, rgg = (w_in[..., :HY_END], w_in[..., HY_END:Q_END], w_in[..., Q_END:K_END],
                             w_in[..., K_END:V_END], w_in[..., V_END:RGX_END], w_in[..., RGX_END:])
    return jnp.concatenate([hy, _pad_heads_cols(q, N_HEADS), _pad_heads_cols(k, N_KV_HEADS),
                            _pad_heads_cols(v, N_KV_HEADS), rgx, rgg], axis=-1).astype(bf16)


def kernel(x, c, ctx, c_ctx, w_mod, b_mod, ln1_g, ln1_b, ln2_g, ln2_b, w_in, w_out,
           hy_short_w, hy_short_b, hy_f_w1, hy_f_b1, hy_f_freq, hy_f_w2, hy_f_b2, hy_f_w3, hy_skip,
           q_norm, k_norm, rg_conv_w, rg_conv_b, rg_lambda, rg_w_a, rg_b_a, rg_w_x, rg_b_x,
           w_router, b_router, w_gate, w_up, w_down, ws_gate, ws_up, ws_down):
    B, T, D = x.shape
    Lc = ctx.shape[1]
    depth = w_mod.shape[0]
    bt, bc = B * T, B * Lc
    n_tok = bt + bc
    tm = TOKEN_TILE
    assert D == D_MODEL and T % tm == 0 and bc % tm == 0 and B % 2 == 0
    assert T % ATT_Q_TILE == 0 and Lc == ATT_Q_TILE and T % GRID_W == 0
    alpha = (2 * depth) ** 0.25
    tiles_per_seq = T // tm
    seg = lambda i: jnp.minimum(i // tiles_per_seq, B)
    pos_block = lambda i: jnp.where(i < B * tiles_per_seq, i % tiles_per_seq, tiles_per_seq)
    moe_tm = DISPATCH_TILE
    assert T % moe_tm == 0 and bc % moe_tm == 0
    moe_seg = lambda i: jnp.minimum(i // (T // moe_tm), B)

    rows = -(-(B + 1) // 8) * 8
    cond = jnp.zeros((rows, D), f32).at[:B].set(c).at[B].set(c_ctx)
    mods = _modulation(cond, w_mod, b_mod)

    cos, sin = _rope_tables(T, tm)
    w_in_pad = _pad_in_proj(w_in)
    w_out_hy = w_out[:, :HY_W].astype(bf16)
    w_out_att = _pad_heads_cols(w_out[:, HY_W:HY_W + ATT_W].transpose(0, 2, 1), N_HEADS).transpose(0, 2, 1).astype(bf16)
    w_out_rg = w_out[:, HY_W + ATT_W:].astype(bf16)
    wr_pad = jnp.zeros((depth, D, LANE), f32).at[:, :, :N_EXPERTS].set(w_router)
    br_pad = jnp.full((depth, 1, LANE), -jnp.inf, f32).at[:, 0, :N_EXPERTS].set(b_router)
    sgu = jnp.concatenate([ws_gate, ws_up], axis=-1).astype(bf16)
    sd = ws_down.astype(bf16)

    h = jnp.concatenate([x.reshape(bt, D), ctx.reshape(bc, D)], axis=0)
    for l in range(depth):
        mods3 = mods[l].reshape(rows, 1, 6 * D)
        hy, q, kv, rg = _inproj(h, mods3, w_in_pad[l], seg, tm)

        xz_l = _dwconv(hy, hy_short_w[l], hy_short_b[l], T, 0, B, 0, 3)
        xz_c = _dwconv(hy, hy_short_w[l], hy_short_b[l], Lc, bt // Lc, B, 0, 3)
        fargs = (hy_f_w1[l], hy_f_b1[l], hy_f_freq[l], hy_f_w2[l], hy_f_b2[l], hy_f_w3[l])
        hy_l = _hyena_long(xz_l, _hyena_filters(T, *fargs), hy_skip[l], B, T)
        hy_c = _hyena_short_seq(xz_c, _hyena_filters(Lc, *fargs), hy_skip[l], B, Lc)
        hy_out = jnp.concatenate([hy_l, hy_c], axis=0)

        qp, kvp = _qkprep(q, kv, cos, sin, q_norm[l], k_norm[l], pos_block, tm)
        att = _attention(qp, kvp, B, T, Lc)

        rg_out = _rglru(rg, B, T, Lc, rg_conv_w[l], rg_conv_b[l], rg_lambda[l],
                        rg_w_a[l], rg_b_a[l], rg_w_x[l], rg_b_x[l])

        wr_hi, wr_lo = _split(wr_pad[l])
        h, m, gates, chosen, counts = _outproj(
            hy_out, att, rg_out, h, mods3, ln1_g[l][None, :], ln1_b[l][None, :],
            w_out_hy[l], w_out_att[l], w_out_rg[l], wr_hi, wr_lo, br_pad[l], moe_seg, moe_tm, alpha)
        h = _moe(m, gates, chosen, counts, l, w_gate, w_up, w_down, sgu[l], sd[l], h, mods3,
                 ln2_g[l][None, :], ln2_b[l][None, :], moe_seg, alpha)
    return h[:bt].reshape(B, T, D)
```

```python
import functools
import math

import numpy as np
import jax
import jax.numpy as jnp
from jax import lax
from jax.experimental import pallas as pl
from jax.experimental.pallas import tpu as pltpu

f32 = jnp.float32
bf16 = jnp.bfloat16

D_MODEL = 1024
GRID_W = 64
HY_W = 256
N_HEADS = 8
N_KV_HEADS = 2
N_GROUPS = N_HEADS // N_KV_HEADS
HEAD_DIM = 64
ATT_W = N_HEADS * HEAD_DIM
KV_W = N_KV_HEADS * HEAD_DIM
RG_W = 256
RG_BLOCKS = 4
HY_END = 3 * HY_W
Q_END = HY_END + ATT_W
K_END = Q_END + KV_W
V_END = K_END + KV_W
RGX_END = V_END + RG_W
PROJ_W = RGX_END + RG_W
HY_BANDS = 8
HY_EMB = 1 + 2 * HY_BANDS
HY_MAX_DECAY = math.log(1e-2) / 0.3
HY_MIN_DECAY = math.log(1e-2) / 1.5
ROPE_THETA = 10000.0
QK_EPS = 1e-6
RG_C = 8.0
N_EXPERTS = 64
TOP_K = 8
EXPERT_FF = 256
ROUTED_SCALE = 2.5
LN_EPS = 1e-6

LANE = 128
HEAD_PAD = LANE
QP_W = N_HEADS * HEAD_PAD
KVP_W = N_KV_HEADS * HEAD_PAD
PROJ_PAD_W = HY_END + QP_W + 2 * KVP_W + 2 * RG_W
VMEM_LIMIT = 52 * 1024 * 1024

FFT_N2 = 128
FFT_PASSES = 1
FILTER_PASSES = 3
TOKEN_TILE = 512
ATT_Q_TILE = 256
SCAN_TILE = 256
DISPATCH_TILE = 256
RUN_ALIGN = 16
RUN_CHUNK = 64
PIECE_SIZES = (RUN_CHUNK, 32, 16)
PLAN_ROWS = 8
PLAN_COUNT_ROW = 6
FFN_TILE = 1024


def _cparams(sem):
    return pltpu.CompilerParams(dimension_semantics=sem, vmem_limit_bytes=VMEM_LIMIT)


def _dot(a, b):
    return jnp.dot(a, b, preferred_element_type=f32)


def _split(x):
    hi = x.astype(bf16)
    lo = (x - hi.astype(f32)).astype(bf16)
    return hi, lo


def _np_split(x):
    x = jnp.asarray(np.asarray(x, np.float32))
    return _split(x)


def _dot3(a, w_hi, w_lo):
    a_hi, a_lo = _split(a)
    return _dot(a_hi, w_hi) + _dot(a_lo, w_hi) + _dot(a_hi, w_lo)


def _stack_dft(mat):
    re_hi, re_lo = _np_split(mat.real)
    im_hi, im_lo = _np_split(mat.imag)
    return jnp.concatenate([re_hi, im_hi, re_lo, im_lo], axis=0)


def _apply_stack(stack, x, passes):
    m2 = stack.shape[0] // 2
    x_hi, x_lo = _split(x)
    if passes == 1:
        return _dot(stack[:m2], x_hi)
    r = _dot(stack, x_hi)
    out = r[:m2] + r[m2:]
    if passes >= 3:
        out = out + _dot(stack[:m2], x_lo)
    return out


def _mod_kernel(a_ref, w_ref, b_ref, o_ref):
    a = a_ref[...]
    a = a * jax.nn.sigmoid(a)
    w_hi, w_lo = _split(w_ref[...])
    o_ref[...] = _dot3(a, w_hi, w_lo) + b_ref[...]


def _modulation(cond, w_mod, b_mod):
    depth, d, n = w_mod.shape
    rows = cond.shape[0]
    tn = 512
    return pl.pallas_call(
        _mod_kernel,
        out_shape=jax.ShapeDtypeStruct((depth, rows, n), f32),
        grid=(depth, n // tn),
        in_specs=[
            pl.BlockSpec((rows, d), lambda l, j: (0, 0)),
            pl.BlockSpec((None, d, tn), lambda l, j: (l, 0, j)),
            pl.BlockSpec((None, 1, tn), lambda l, j: (l, 0, j)),
        ],
        out_specs=pl.BlockSpec((None, rows, tn), lambda l, j: (l, 0, j)),
        compiler_params=_cparams(("parallel", "parallel")),
        name="modulation",
    )(cond, w_mod, b_mod.reshape(depth, 1, n))


def _inproj_kernel(h_ref, sh_ref, sc_ref, w_ref, ohy_ref, oq_ref, okv_ref, org_ref):
    a = h_ref[...] * (1.0 + sc_ref[...]) + sh_ref[...]
    p = _dot(a.astype(bf16), w_ref[...])
    c0, c1, c2 = HY_END, HY_END + QP_W, HY_END + QP_W + 2 * KVP_W
    ohy_ref[...] = p[:, :c0]
    oq_ref[...] = p[:, c0:c1]
    okv_ref[...] = p[:, c1:c2]
    org_ref[...] = p[:, c2:]


def _mod_spec(seg, j):
    return pl.BlockSpec((None, 1, D_MODEL), lambda i: (seg(i), 0, j))


def _inproj(h, mods3, w_pad, seg, tm):
    n = h.shape[0]
    widths = (HY_END, QP_W, 2 * KVP_W, 2 * RG_W)
    return pl.pallas_call(
        _inproj_kernel,
        out_shape=[jax.ShapeDtypeStruct((n, w), f32) for w in widths],
        grid=(n // tm,),
        in_specs=[
            pl.BlockSpec((tm, D_MODEL), lambda i: (i, 0)),
            _mod_spec(seg, 0),
            _mod_spec(seg, 1),
            pl.BlockSpec((D_MODEL, PROJ_PAD_W), lambda i: (0, 0)),
        ],
        out_specs=[pl.BlockSpec((tm, w), lambda i: (i, 0)) for w in widths],
        compiler_params=_cparams(("parallel",)),
        name="inproj",
    )(h, mods3, mods3, w_pad)


def _dwconv_kernel(x_ref, w_ref, b_ref, o_ref, *, width, left):
    x = x_ref[...]
    n = x.shape[0]
    row = lax.broadcasted_iota(jnp.int32, x.shape, 0)
    acc = jnp.zeros_like(x) + b_ref[...]
    for j in range(width):
        off = j - left
        if off == 0:
            xs = x
        else:
            xs = pltpu.roll(x, (-off) % n, axis=0)
            valid = jnp.logical_and(row + off >= 0, row + off < n)
            xs = jnp.where(valid, xs, 0.0)
        acc = acc + xs * w_ref[j:j + 1, :]
    o_ref[...] = acc


def _dwconv(x, w, b, seq_len, row_block0, n_seq, col_block0, n_col_blocks):
    width = w.shape[0]
    wp = jnp.zeros((8, w.shape[1]), f32).at[:width].set(w)
    ct = 256
    return pl.pallas_call(
        functools.partial(_dwconv_kernel, width=width, left=(width - 1) // 2),
        out_shape=jax.ShapeDtypeStruct((n_seq * seq_len, n_col_blocks * ct), f32),
        grid=(n_seq, n_col_blocks),
        in_specs=[
            pl.BlockSpec((seq_len, ct), lambda s, c: (row_block0 + s, col_block0 + c)),
            pl.BlockSpec((8, ct), lambda s, c: (0, c)),
            pl.BlockSpec((1, ct), lambda s, c: (0, c)),
        ],
        out_specs=pl.BlockSpec((seq_len, ct), lambda s, c: (s, c)),
        compiler_params=_cparams(("parallel", "parallel")),
        name="dwconv",
    )(x, wp, b.reshape(1, -1))


def _filter_kernel(z_ref, w1h_ref, w1l_ref, b1_ref, fr_ref, w2h_ref, w2l_ref, b2_ref,
                   w3h_ref, w3l_ref, win_ref, o_ref):
    fr = fr_ref[...]
    f = jnp.sin(fr * (_dot3(z_ref[...], w1h_ref[...], w1l_ref[...]) + b1_ref[...]))
    f = jnp.sin(fr * (_dot3(f, w2h_ref[...], w2l_ref[...]) + b2_ref[...]))
    f = _dot3(f, w3h_ref[...], w3l_ref[...])
    win = win_ref[...]
    o_ref[...] = f * jnp.concatenate([win] * 2, axis=1)


def _filter_features(L):
    t = np.linspace(0.0, 1.0, L, dtype=np.float32)[:, None].astype(np.float64)
    w = np.float32(2.0 * math.pi) * np.arange(L, dtype=np.float32)[:, None] / np.float32(L)
    bands = np.linspace(1e-4, HY_BANDS - 1, HY_BANDS, dtype=np.float32)
    arg = (bands * w).astype(np.float64)
    z = np.concatenate([t, np.cos(arg), -np.sin(arg)], axis=-1)
    deltas = np.abs(np.linspace(HY_MIN_DECAY, HY_MAX_DECAY, HY_W, dtype=np.float32)).astype(np.float64)
    win = np.exp(-t * deltas)
    zp = np.zeros((2 * L, LANE), np.float32)
    zp[:L, :HY_EMB] = z
    zp[L + 1:, :HY_EMB] = z[:0:-1]
    win2 = np.zeros((2 * L, HY_W), np.float32)
    win2[:L] = win
    win2[L + 1:] = win[:0:-1]
    return jnp.asarray(zp), jnp.asarray(win2)


def _pad2(w, rows, cols):
    return jnp.zeros((rows, cols), f32).at[:w.shape[0], :w.shape[1]].set(w)


def _hyena_filters(L, w1, b1, freq, w2, b2, w3):
    zp, win = _filter_features(L)
    w1h, w1l = _split(_pad2(w1, LANE, LANE))
    w2h, w2l = _split(_pad2(w2, LANE, LANE))
    w3h, w3l = _split(_pad2(w3, LANE, 4 * HY_W))
    b1p = _pad2(b1[None, :], 1, LANE)
    b2p = _pad2(b2[None, :], 1, LANE)
    frp = _pad2(freq[None, :], 1, LANE)
    tl = min(L, 512)
    half = L // tl
    full = lambda shape: pl.BlockSpec(shape, lambda i: (0, 0))
    w3_spec = pl.BlockSpec((LANE, 2 * HY_W), lambda i: (0, i // half))
    return pl.pallas_call(
        _filter_kernel,
        out_shape=jax.ShapeDtypeStruct((2 * L, 2 * HY_W), f32),
        grid=(2 * half,),
        in_specs=[
            pl.BlockSpec((tl, LANE), lambda i: (i, 0)),
            full((LANE, LANE)), full((LANE, LANE)), full((1, LANE)), full((1, LANE)),
            full((LANE, LANE)), full((LANE, LANE)), full((1, LANE)),
            w3_spec, w3_spec,
            pl.BlockSpec((tl, HY_W), lambda i: (i, 0)),
        ],
        out_specs=pl.BlockSpec((tl, 2 * HY_W), lambda i: (i, 0)),
        compiler_params=_cparams(("parallel",)),
        name="hyena_filter",
    )(zp, w1h, w1l, b1p, frp, w2h, w2l, b2p, w3h, w3l, win)


def _dft_consts(n):
    n1 = n // FFT_N2
    k = np.arange(n1)
    f_n1 = np.exp(-2j * np.pi * np.outer(k, k) / n1)
    k2 = np.arange(FFT_N2)
    f_n2 = np.exp(-2j * np.pi * np.outer(k2, k2) / FFT_N2)
    tw = np.exp(-2j * np.pi * np.outer(k, k2) / n).reshape(n, 1)
    tw_re = jnp.asarray(np.broadcast_to(tw.real, (n, LANE)).astype(np.float32))
    tw_im = jnp.asarray(np.broadcast_to(tw.imag, (n, LANE)).astype(np.float32))
    return n1, f_n1, f_n2, tw_re, tw_im


def _fa_kernel(u_ref, fs_ref, are_ref, aim_ref, *, grp, passes, real_only):
    fs = fs_ref[...]
    n1 = fs.shape[0] // 4
    jb, c = u_ref.shape[2], u_ref.shape[3]
    for g in range(jb // grp):
        def gather(bi):
            return jnp.concatenate([u_ref[bi, :, g * grp + jj, :] for jj in range(grp)], axis=1)
        p = _apply_stack(fs, gather(0), passes)
        if real_only:
            re, im = p[:n1], p[n1:]
        else:
            q = _apply_stack(fs, gather(1), passes)
            re, im = p[:n1] - q[n1:], p[n1:] + q[:n1]
        for jj in range(grp):
            are_ref[:, g * grp + jj, :] = re[:, jj * c:(jj + 1) * c]
            aim_ref[:, g * grp + jj, :] = im[:, jj * c:(jj + 1) * c]


def _fft_step_a(u4, fs, col_block, c, real_only, passes):
    s, n1_in = u4.shape[0], u4.shape[1]
    n1 = fs.shape[0] // 4
    per = 1 if real_only else 2
    jb = 32
    grp = max(1, 1024 // c)
    shape = jax.ShapeDtypeStruct((s // per, n1, FFT_N2, c), f32)
    return pl.pallas_call(
        functools.partial(_fa_kernel, grp=grp, passes=passes, real_only=real_only),
        out_shape=[shape, shape],
        grid=(s // per, FFT_N2 // jb),
        in_specs=[
            pl.BlockSpec((per, n1_in, jb, c), lambda p, j: (p, 0, j, col_block)),
            pl.BlockSpec(fs.shape, lambda p, j: (0, 0)),
        ],
        out_specs=[pl.BlockSpec((None, n1, jb, c), lambda p, j: (p, 0, j, 0))] * 2,
        compiler_params=_cparams(("parallel", "parallel")),
        name="fft_step_a",
    )(u4, fs)


def _mid_kernel(are_ref, aim_ref, twr_ref, twi_ref, f2_ref, *rest, kb, passes, fwd_only, inv_n):
    if fwd_only:
        ore_ref, oim_ref = rest
    else:
        kr_ref, ki_ref, ore_ref, oim_ref = rest
    f2 = f2_ref[...]
    c = are_ref.shape[1]
    for kk in range(kb):
        rows = pl.ds(kk * FFT_N2, FFT_N2)
        ar, ai = are_ref[rows, :], aim_ref[rows, :]
        tr = jnp.concatenate([twr_ref[rows, :]] * (c // LANE), axis=1)
        ti = jnp.concatenate([twi_ref[rows, :]] * (c // LANE), axis=1)
        xr = ar * tr - ai * ti
        xi = ar * ti + ai * tr
        p = _apply_stack(f2, xr, passes)
        q = _apply_stack(f2, xi, passes)
        sr = p[:FFT_N2] - q[FFT_N2:]
        si = p[FFT_N2:] + q[:FFT_N2]
        if fwd_only:
            ore_ref[rows, :] = sr
            oim_ref[rows, :] = si
        else:
            kr, ki = kr_ref[rows, :], ki_ref[rows, :]
            yr = sr * kr - si * ki
            yi = sr * ki + si * kr
            p2 = _apply_stack(f2, yr, passes)
            q2 = _apply_stack(f2, yi, passes)
            br = p2[:FFT_N2] + q2[FFT_N2:]
            bi = q2[:FFT_N2] - p2[FFT_N2:]
            ore_ref[rows, :] = (br * tr + bi * ti) * inv_n
            oim_ref[rows, :] = (bi * tr - br * ti) * inv_n


def _fft_mid(a_re, a_im, tw_re, tw_im, f2s, kf=None, order=0, passes=FFT_PASSES):
    p, n, c = a_re.shape
    kb = min(4, n // FFT_N2)
    rb = kb * FFT_N2
    fwd_only = kf is None
    blk = pl.BlockSpec((None, rb, c), lambda i, k: (i, k, 0))
    in_specs = [blk, blk,
                pl.BlockSpec((rb, LANE), lambda i, k: (k, 0)),
                pl.BlockSpec((rb, LANE), lambda i, k: (k, 0)),
                pl.BlockSpec(f2s.shape, lambda i, k: (0, 0))]
    args = [a_re, a_im, tw_re, tw_im, f2s]
    if not fwd_only:
        in_specs += [pl.BlockSpec((rb, c), lambda i, k: (k, order))] * 2
        args += list(kf)
    shape = jax.ShapeDtypeStruct((p, n, c), f32)
    return pl.pallas_call(
        functools.partial(_mid_kernel, kb=kb, passes=passes, fwd_only=fwd_only, inv_n=1.0 / n),
        out_shape=[shape, shape],
        grid=(p, n // rb),
        in_specs=in_specs,
        out_specs=[blk, blk],
        compiler_params=_cparams(("parallel", "parallel")),
        name="fft_mid",
    )(*args)


def _fai_kernel(bre_ref, bim_ref, gs_ref, u_ref, gate_ref, skip_ref, o_ref, *, grp, passes):
    gs = gs_ref[...]
    n1h = gs.shape[0] // 4
    jb, c = u_ref.shape[2], u_ref.shape[3]
    skip = skip_ref[...]
    for g in range(jb // grp):
        def gather(ref):
            return jnp.concatenate([ref[:, g * grp + jj, :] for jj in range(grp)], axis=1)
        p = _apply_stack(gs, gather(bre_ref), passes)
        q = _apply_stack(gs, gather(bim_ref), passes)
        ya = p[:n1h] - q[n1h:]
        yb = p[n1h:] + q[:n1h]
        for jj in range(grp):
            j = g * grp + jj
            for bi, y in ((0, ya), (1, yb)):
                u = u_ref[bi, :, j, :]
                o_ref[bi, :, j, :] = gate_ref[bi, :, j, :] * (y[:, jj * c:(jj + 1) * c] + u * skip)


def _fft_step_a_inv(b_re, b_im, gs, u4, u_col, gate4, gate_col, skip, passes):
    p, n1, _, c = b_re.shape
    n1h = n1 // 2
    jb = 32
    grp = max(1, 1024 // c)
    bspec = pl.BlockSpec((None, n1, jb, c), lambda i, j: (i, 0, j, 0))
    return pl.pallas_call(
        functools.partial(_fai_kernel, grp=grp, passes=passes),
        out_shape=jax.ShapeDtypeStruct((2 * p, n1h, FFT_N2, c), f32),
        grid=(p, FFT_N2 // jb),
        in_specs=[
            bspec, bspec,
            pl.BlockSpec(gs.shape, lambda i, j: (0, 0)),
            pl.BlockSpec((2, n1h, jb, c), lambda i, j: (i, 0, j, u_col)),
            pl.BlockSpec((2, n1h, jb, c), lambda i, j: (i, 0, j, gate_col)),
            pl.BlockSpec((1, c), lambda i, j: (0, 0)),
        ],
        out_specs=pl.BlockSpec((2, n1h, jb, c), lambda i, j: (i, 0, j, 0)),
        compiler_params=_cparams(("parallel", "parallel")),
        name="fft_step_a_inv",
    )(b_re, b_im, gs, u4, gate4, skip)


def _hyena_long(xz, filt, skip, n_seq, L, passes=FFT_PASSES):
    n = 2 * L
    n1, f_n1, f_n2, tw_re, tw_im = _dft_consts(n)
    n1h = n1 // 2
    c = HY_W
    fs_full = _stack_dft(f_n1)
    fs_half = _stack_dft(f_n1[:, :n1h])
    gs = _stack_dft(np.conj(f_n1)[:n1h, :])
    f2s = _stack_dft(f_n2)
    kc4 = filt.reshape(1, n1, FFT_N2, 2 * c)
    k_re, k_im = _fft_step_a(kc4, fs_full, 0, 2 * c, True, FILTER_PASSES)
    kf = _fft_mid(k_re.reshape(1, n, 2 * c), k_im.reshape(1, n, 2 * c), tw_re, tw_im, f2s,
                  passes=FILTER_PASSES)
    kf = (kf[0].reshape(n, 2 * c), kf[1].reshape(n, 2 * c))
    xz4 = xz.reshape(n_seq, n1h, FFT_N2, 3 * c)
    z4, z_col = xz4, 2
    for order in range(2):
        a_re, a_im = _fft_step_a(z4, fs_half, z_col, c, False, passes)
        p = n_seq // 2
        b_re, b_im = _fft_mid(a_re.reshape(p, n, c), a_im.reshape(p, n, c), tw_re, tw_im, f2s,
                              kf=kf, order=order, passes=passes)
        z4 = _fft_step_a_inv(b_re.reshape(p, n1, FFT_N2, c), b_im.reshape(p, n1, FFT_N2, c), gs,
                             z4, z_col, xz4, order, skip[order:order + 1], passes)
        z_col = 0
    return z4.reshape(n_seq * L, c)


def _dense_spec_kernel(k_ref, fs_ref, ore_ref, oim_ref, *, passes):
    n = k_ref.shape[0]
    p = _apply_stack(fs_ref[...], k_ref[...], passes)
    ore_ref[...] = p[:n]
    oim_ref[...] = p[n:]


def _dense_conv_kernel(xz_ref, fs_ref, gs_ref, kr_ref, ki_ref, skip_ref, o_ref, *, passes):
    L = xz_ref.shape[1]
    n = 2 * L
    c = HY_W
    fs, gs = fs_ref[...], gs_ref[...]
    za, zb = xz_ref[0, :, 2 * c:], xz_ref[1, :, 2 * c:]
    for order in range(2):
        p = _apply_stack(fs, za, passes)
        q = _apply_stack(fs, zb, passes)
        sr, si = p[:n] - q[n:], p[n:] + q[:n]
        kr, ki = kr_ref[:, order * c:(order + 1) * c], ki_ref[:, order * c:(order + 1) * c]
        yr, yi = sr * kr - si * ki, sr * ki + si * kr
        p2 = _apply_stack(gs, yr, passes)
        q2 = _apply_stack(gs, yi, passes)
        ya = (p2[:L] - q2[L:]) * (1.0 / n)
        yb = (p2[L:] + q2[:L]) * (1.0 / n)
        skip = skip_ref[order:order + 1, :]
        za = xz_ref[0, :, order * c:(order + 1) * c] * (ya + za * skip)
        zb = xz_ref[1, :, order * c:(order + 1) * c] * (yb + zb * skip)
    o_ref[0] = za
    o_ref[1] = zb


def _hyena_short_seq(xz, filt, skip, n_seq, L, passes=FFT_PASSES):
    n = 2 * L
    c = HY_W
    k = np.arange(n)
    f_n = np.exp(-2j * np.pi * np.outer(k, k) / n)
    fs_full = _stack_dft(f_n)
    fs_half = _stack_dft(f_n[:, :L])
    gs = _stack_dft(np.conj(f_n)[:L, :])
    kc = filt
    shape = jax.ShapeDtypeStruct((n, 2 * c), f32)
    k_re, k_im = pl.pallas_call(
        functools.partial(_dense_spec_kernel, passes=FILTER_PASSES),
        out_shape=[shape, shape],
        compiler_params=_cparams(None),
        name="dense_filter_spectrum",
    )(kc, fs_full)
    skip_p = jnp.zeros((8, c), f32).at[:2].set(skip)
    full = lambda a: pl.BlockSpec(a.shape, lambda i: (0,) * a.ndim)
    out = pl.pallas_call(
        functools.partial(_dense_conv_kernel, passes=passes),
        out_shape=jax.ShapeDtypeStruct((n_seq, L, c), f32),
        grid=(n_seq // 2,),
        in_specs=[pl.BlockSpec((2, L, 3 * c), lambda i: (i, 0, 0)),
                  full(fs_half), full(gs), full(k_re), full(k_im), full(skip_p)],
        out_specs=pl.BlockSpec((2, L, c), lambda i: (i, 0, 0)),
        compiler_params=_cparams(("parallel",)),
        name="dense_long_conv",
    )(xz.reshape(n_seq, L, 3 * c), fs_half, gs, k_re, k_im, skip_p)
    return out.reshape(n_seq * L, c)


def _rope_tables(T, tm):
    n_rows = T // GRID_W
    row = np.repeat(np.arange(n_rows, dtype=np.float32), GRID_W)
    col = np.tile(np.arange(GRID_W, dtype=np.float32), n_rows)
    axis_dim = HEAD_DIM // 2
    inv_freq = np.float32(ROPE_THETA) ** (-np.arange(0, axis_dim, 2, dtype=np.float32) / np.float32(axis_dim))
    ang = np.concatenate([row[:, None] * inv_freq, col[:, None] * inv_freq], axis=-1).astype(np.float64)
    cos = np.zeros((T + tm, HEAD_PAD), np.float32)
    sin = np.zeros((T + tm, HEAD_PAD), np.float32)
    cos[:T, 0:HEAD_DIM:2] = np.cos(ang)
    cos[:T, 1:HEAD_DIM:2] = np.cos(ang)
    sin[:T, 0:HEAD_DIM:2] = -np.sin(ang)
    sin[:T, 1:HEAD_DIM:2] = np.sin(ang)
    cos[T:, :HEAD_DIM] = 1.0
    return jnp.asarray(cos), jnp.asarray(sin)


def _qkprep_kernel(q_ref, kv_ref, cos_ref, sin_ref, qg_ref, kg_ref, avg_ref, oq_ref, okv_ref):
    cos, sin = cos_ref[...], sin_ref[...]
    avg = avg_ref[...]
    lane = lax.broadcasted_iota(jnp.int32, cos.shape, 1)
    even = (lane % 2) == 0

    def norm_rope(x, gain, scale):
        sq_hi, sq_lo = _split(x * x)
        ms = _dot(sq_hi, avg) + _dot(sq_lo, avg)
        xn = x * lax.rsqrt(ms + QK_EPS) * gain
        swapped = jnp.where(even, pltpu.roll(xn, LANE - 1, axis=1), pltpu.roll(xn, 1, axis=1))
        return ((xn * cos + swapped * sin) * scale).astype(bf16)

    for h in range(N_HEADS):
        sl = slice(h * HEAD_PAD, (h + 1) * HEAD_PAD)
        oq_ref[:, sl] = norm_rope(q_ref[:, sl], qg_ref[...], HEAD_DIM ** -0.5)
    for g in range(N_KV_HEADS):
        sl = slice(g * HEAD_PAD, (g + 1) * HEAD_PAD)
        okv_ref[:, sl] = norm_rope(kv_ref[:, sl], kg_ref[...], 1.0)
    v = kv_ref[:, KVP_W:]
    v_lane = lax.broadcasted_iota(jnp.int32, v.shape, 1) % HEAD_PAD
    okv_ref[:, KVP_W:] = jnp.where(v_lane == HEAD_DIM, 1.0, v).astype(bf16)


def _qkprep(q, kv, cos, sin, q_gain, k_gain, pos_block, tm):
    n = q.shape[0]
    pad = lambda g: jnp.zeros((1, HEAD_PAD), f32).at[0, :HEAD_DIM].set(g)
    avg = jnp.full((HEAD_PAD, HEAD_PAD), 1.0 / HEAD_DIM, bf16)
    one = lambda shape: pl.BlockSpec(shape, lambda i: (0, 0))
    return pl.pallas_call(
        _qkprep_kernel,
        out_shape=[jax.ShapeDtypeStruct((n, QP_W), bf16), jax.ShapeDtypeStruct((n, 2 * KVP_W), bf16)],
        grid=(n // tm,),
        in_specs=[
            pl.BlockSpec((tm, QP_W), lambda i: (i, 0)),
            pl.BlockSpec((tm, 2 * KVP_W), lambda i: (i, 0)),
            pl.BlockSpec((tm, HEAD_PAD), lambda i: (pos_block(i), 0)),
            pl.BlockSpec((tm, HEAD_PAD), lambda i: (pos_block(i), 0)),
            one((1, HEAD_PAD)), one((1, HEAD_PAD)), one((HEAD_PAD, HEAD_PAD)),
        ],
        out_specs=[pl.BlockSpec((tm, QP_W), lambda i: (i, 0)),
                   pl.BlockSpec((tm, 2 * KVP_W), lambda i: (i, 0))],
        compiler_params=_cparams(("parallel",)),
        name="qk_prep",
    )(q, kv, cos, sin, pad(q_gain), pad(k_gain), avg)


def _attn_kernel(q_ref, kvl_ref, kvc_ref, o_ref, *, nq_lat):
    i = pl.program_id(1)
    dims = (((1,), (1,)), ((), ()))

    def scores(h, use_lat):
        g = h // N_GROUPS
        ks = slice(g * HEAD_PAD, (g + 1) * HEAD_PAD)
        q = q_ref[:, h * HEAD_PAD:(h + 1) * HEAD_PAD]
        sc = lax.dot_general(q, kvc_ref[:, ks], dims, preferred_element_type=f32).astype(bf16)
        sl = None
        if use_lat:
            sl = lax.dot_general(q, kvl_ref[:, ks], dims, preferred_element_type=f32).astype(bf16)
        return sl, sc

    def heads(use_lat):
        nxt = scores(0, use_lat)
        for h in range(N_HEADS):
            sl, sc = nxt
            if h + 1 < N_HEADS:
                nxt = scores(h + 1, use_lat)
            g = h // N_GROUPS
            vs = slice(KVP_W + g * HEAD_PAD, KVP_W + (g + 1) * HEAD_PAD)
            m = jnp.max(sc, axis=-1, keepdims=True)
            if use_lat:
                m = jnp.maximum(m, jnp.max(sl, axis=-1, keepdims=True))
                acc = _dot(jnp.exp(sl - m), kvl_ref[:, vs]) + _dot(jnp.exp(sc - m), kvc_ref[:, vs])
            else:
                acc = _dot(jnp.exp(sc - m), kvc_ref[:, vs])
            o_ref[:, h * HEAD_PAD:(h + 1) * HEAD_PAD] = (acc / acc[:, HEAD_DIM:HEAD_DIM + 1]).astype(bf16)

    @pl.when(i < nq_lat)
    def _():
        heads(True)

    @pl.when(i == nq_lat)
    def _():
        heads(False)


def _attention(qp, kvp, B, T, Lc):
    n = qp.shape[0]
    tq = Lc
    nq = T // tq
    q_idx = lambda b, i: (jnp.where(i < nq, b * nq + i, B * nq + b), 0)
    return pl.pallas_call(
        functools.partial(_attn_kernel, nq_lat=nq),
        out_shape=jax.ShapeDtypeStruct((n, QP_W), bf16),
        grid=(B, nq + 1),
        in_specs=[
            pl.BlockSpec((tq, QP_W), q_idx),
            pl.BlockSpec((T, 2 * KVP_W), lambda b, i: (b, 0)),
            pl.BlockSpec((Lc, 2 * KVP_W), lambda b, i: (B * nq + b, 0)),
        ],
        out_specs=pl.BlockSpec((tq, QP_W), q_idx),
        compiler_params=_cparams(("parallel", "arbitrary")),
        name="attention",
    )(qp, kvp, kvp)


def _scan_kernel(u_ref, wh_ref, wl_ref, bias_ref, lam_ref, h0_ref, *rest, reverse, final):
    if final:
        hprev_ref, g_ref, o_ref, hend_ref, carry_ref = rest
    else:
        o_ref, hend_ref, carry_ref = rest
    t = pl.program_id(1)

    @pl.when(t == 0)
    def _():
        carry_ref[...] = h0_ref[...]

    u = u_ref[...]
    tt = u.shape[0]
    gates = _dot3(u, wh_ref[...], wl_ref[...]) + bias_ref[...]
    r = jax.nn.sigmoid(gates[:, :RG_W])
    ig = jax.nn.sigmoid(gates[:, RG_W:])
    lam = lam_ref[...]
    softplus = jnp.maximum(-lam, 0.0) + jnp.log1p(jnp.exp(-jnp.abs(lam)))
    log_a = -RG_C * r * softplus
    a = jnp.exp(log_a)
    b = jnp.sqrt(-jnp.tanh(log_a) * (a * a + 1.0)) * (ig * u)
    row = lax.broadcasted_iota(jnp.int32, a.shape, 0)
    s = 1
    while s < tt:
        if reverse:
            a_s, b_s = pltpu.roll(a, tt - s, axis=0), pltpu.roll(b, tt - s, axis=0)
            valid = row < tt - s
        else:
            a_s, b_s = pltpu.roll(a, s, axis=0), pltpu.roll(b, s, axis=0)
            valid = row >= s
        b = a * jnp.where(valid, b_s, 0.0) + b
        a = a * jnp.where(valid, a_s, 1.0)
        s *= 2
    h = a * carry_ref[...] + b
    last = h[0:1, :] if reverse else h[tt - 1:tt, :]
    carry_ref[...] = last
    hend_ref[...] = last
    if final:
        o_ref[...] = ((hprev_ref[...] + h) * jax.nn.gelu(g_ref[...], approximate=True)).astype(o_ref.dtype)
    else:
        o_ref[...] = h


def _rg_scan(u, n_seq, L, w_hi, w_lo, bias, lam, h0, reverse, hprev=None, gate=None, gate_row0=0):
    tt = min(SCAN_TILE, L)
    nt = L // tt
    final = hprev is not None
    tidx = (lambda t: nt - 1 - t) if reverse else (lambda t: t)
    row = lambda b, t: (b * nt + tidx(t), 0)
    one = lambda shape: pl.BlockSpec(shape, lambda b, t: (0, 0))
    in_specs = [pl.BlockSpec((tt, RG_W), row), one(w_hi.shape), one(w_lo.shape), one((1, 2 * RG_W)),
                one((1, RG_W)), pl.BlockSpec((None, 1, RG_W), lambda b, t: (b, 0, 0))]
    args = [u, w_hi, w_lo, bias, lam, h0]
    if final:
        in_specs += [pl.BlockSpec((tt, RG_W), row),
                     pl.BlockSpec((tt, RG_W), lambda b, t: (gate_row0 // tt + b * nt + tidx(t), 1))]
        args += [hprev, gate]
    return pl.pallas_call(
        functools.partial(_scan_kernel, reverse=reverse, final=final),
        out_shape=[jax.ShapeDtypeStruct((n_seq * L, RG_W), bf16 if final else f32),
                   jax.ShapeDtypeStruct((n_seq, 1, RG_W), f32)],
        grid=(n_seq, nt),
        in_specs=in_specs,
        out_specs=[pl.BlockSpec((tt, RG_W), row), pl.BlockSpec((None, 1, RG_W), lambda b, t: (b, 0, 0))],
        scratch_shapes=[pltpu.VMEM((1, RG_W), f32)],
        compiler_params=_cparams(("parallel", "arbitrary")),
        name="rg_scan",
    )(*args)


def _block_diag(w):
    bw = w.shape[-1]
    out = jnp.zeros((RG_W, RG_W), f32)
    for h in range(RG_BLOCKS):
        out = out.at[h * bw:(h + 1) * bw, h * bw:(h + 1) * bw].set(w[h])
    return out


def _rglru(rg, B, T, Lc, conv_w, conv_b, lam, w_a, b_a, w_x, b_x):
    bt = B * T
    u_l = _dwconv(rg, conv_w, conv_b, T, 0, B, 0, 1)
    u_c = _dwconv(rg, conv_w, conv_b, Lc, bt // Lc, B, 0, 1)
    zeros = jnp.zeros((B, 1, RG_W), f32)
    prev_l = prev_c = None
    for d, rev in enumerate((False, True)):
        w_hi, w_lo = _split(jnp.concatenate([_block_diag(w_a[d]), _block_diag(w_x[d])], axis=1))
        bias = jnp.concatenate([b_a[d], b_x[d]])[None, :]
        lam_d = lam[d][None, :]
        last = d == 1
        kw_c = dict(hprev=prev_c, gate=rg, gate_row0=bt) if last else {}
        kw_l = dict(hprev=prev_l, gate=rg, gate_row0=0) if last else {}
        prev_c, h_end = _rg_scan(u_c, B, Lc, w_hi, w_lo, bias, lam_d, zeros, rev, **kw_c)
        prev_l, _ = _rg_scan(u_l, B, T, w_hi, w_lo, bias, lam_d, h_end, rev, **kw_l)
    return jnp.concatenate([prev_l, prev_c], axis=0)


def _layer_norm(v, g, b):
    mu = jnp.mean(v, axis=-1, keepdims=True)
    d = v - mu
    var = jnp.mean(d * d, axis=-1, keepdims=True)
    return d * lax.rsqrt(var + LN_EPS) * g + b


def _outproj_kernel(hy_ref, att_ref, rg_ref, h_ref, g1_ref, sh2_ref, sc2_ref, lng_ref, lnb_ref,
                    why_ref, watt_ref, wrg_ref, wrh_ref, wrl_ref, br_ref,
                    oh_ref, om_ref, og_ref, oi_ref, oc_ref, *, alpha):
    y = (_dot(hy_ref[...].astype(bf16), why_ref[...]) + _dot(att_ref[...], watt_ref[...])
         + _dot(rg_ref[...], wrg_ref[...]))
    hn = _layer_norm(alpha * h_ref[...] + g1_ref[...] * y, lng_ref[...], lnb_ref[...])
    oh_ref[...] = hn
    m = hn * (1.0 + sc2_ref[...]) + sh2_ref[...]
    om_ref[...] = m.astype(bf16)
    scores = jax.nn.sigmoid(_dot3(m, wrh_ref[...], wrl_ref[...]))
    sel = scores + br_ref[...]
    lane = lax.broadcasted_iota(jnp.int32, sel.shape, 1)
    picked = jnp.zeros_like(scores)
    chosen = jnp.zeros(sel.shape, jnp.int32)
    count = jnp.zeros((1, LANE), f32)
    for k in range(TOP_K):
        mx = jnp.max(sel, axis=-1, keepdims=True)
        first = jnp.min(jnp.where(sel == mx, lane, LANE), axis=-1, keepdims=True)
        hit = lane == first
        picked = jnp.where(hit, scores, picked)
        sel = jnp.where(hit, -jnp.inf, sel)
        chosen = jnp.where(lane == k, first, chosen)
        count = count + jnp.sum(jnp.where(hit, 1.0, 0.0), axis=0, keepdims=True)
    og_ref[...] = picked / jnp.sum(picked, axis=-1, keepdims=True) * ROUTED_SCALE
    oi_ref[...] = chosen
    oc_ref[...] = count


def _outproj(hy, att, rg, h, mods3, ln_g, ln_b, w_hy, w_att, w_rg, wr_hi, wr_lo, b_router, seg, tm, alpha):
    n = h.shape[0]
    row = lambda w: pl.BlockSpec((tm, w), lambda i: (i, 0))
    one = lambda a: pl.BlockSpec(a.shape, lambda i: (0,) * a.ndim)
    return pl.pallas_call(
        functools.partial(_outproj_kernel, alpha=alpha),
        out_shape=[jax.ShapeDtypeStruct((n, D_MODEL), f32), jax.ShapeDtypeStruct((n, D_MODEL), bf16),
                   jax.ShapeDtypeStruct((n, LANE), f32), jax.ShapeDtypeStruct((n, LANE), jnp.int32),
                   jax.ShapeDtypeStruct((n // tm, 1, LANE), f32)],
        grid=(n // tm,),
        in_specs=[row(HY_W), row(QP_W), row(RG_W), row(D_MODEL),
                  _mod_spec(seg, 2), _mod_spec(seg, 3), _mod_spec(seg, 4),
                  one(ln_g), one(ln_b), one(w_hy), one(w_att), one(w_rg), one(wr_hi), one(wr_lo),
                  one(b_router)],
        out_specs=[row(D_MODEL), row(D_MODEL), row(LANE), row(LANE),
                   pl.BlockSpec((None, 1, LANE), lambda i: (i, 0, 0))],
        compiler_params=_cparams(("parallel",)),
        name="outproj_router",
    )(hy, att, rg, h, mods3, mods3, mods3, ln_g, ln_b, w_hy, w_att, w_rg, wr_hi, wr_lo, b_router)


def _xs_rows(tile):
    return -(-(TOP_K * tile + N_EXPERTS * (RUN_ALIGN - 1)) // 256) * 256


def _perm_matrix(idx, runstart, gates):
    t = idx.shape[0]
    lane = lax.broadcasted_iota(jnp.int32, (t, LANE), 1)
    hits = [lane == idx[:, k:k + 1] for k in range(TOP_K)]
    sel = jnp.zeros((t, LANE), f32)
    for hit in hits:
        sel = jnp.where(hit, 1.0, sel)
    earlier = (lax.broadcasted_iota(jnp.int32, (t, t), 1) < lax.broadcasted_iota(jnp.int32, (t, t), 0))
    rank = _dot(jnp.where(earlier, 1.0, 0.0).astype(bf16), sel.astype(bf16))
    posmat = rank + runstart
    slot = lax.broadcasted_iota(jnp.int32, (t, _xs_rows(t)), 1).astype(jnp.int16)
    acc = jnp.zeros(slot.shape, bf16)
    for hit in hits:
        pos = jnp.sum(jnp.where(hit, posmat, 0.0), axis=-1, keepdims=True).astype(jnp.int32).astype(jnp.int16)
        if gates is None:
            val = jnp.ones((t, 1), bf16)
        else:
            val = jnp.sum(jnp.where(hit, gates, 0.0), axis=-1, keepdims=True).astype(bf16)
        acc = jnp.where(slot == pos, val, acc)
    return acc


def _for_pieces(plan_ref, fn):
    for k, size in enumerate(PIECE_SIZES):
        def body(j, carry, k=k, size=size):
            fn(pl.multiple_of(plan_ref[2 * k, j], RUN_ALIGN), pl.multiple_of(plan_ref[2 * k + 1, j], RUN_ALIGN),
               size)
            return carry

        lax.fori_loop(0, plan_ref[PLAN_COUNT_ROW, k], body, 0)


def _dispatch_kernel(gap_ref, reg_ref, x_ref, idx_ref, rsf_ref, plan_ref, prev_ref, s_ref,
                     xs_ref, zero_ref, sem):
    i = pl.program_id(0)
    slot = i % 2

    @pl.when(i == 0)
    def _():
        zero_ref[...] = jnp.zeros(zero_ref.shape, zero_ref.dtype)

        def fill(action):
            def body(e, carry):
                @pl.when(reg_ref[e] > 0)
                def _():
                    dst = pl.multiple_of(gap_ref[e], RUN_ALIGN)
                    action(pltpu.make_async_copy(zero_ref, s_ref.at[pl.ds(dst, FFN_TILE)], sem))
                return carry
            lax.fori_loop(0, N_EXPERTS, body, 0)

        fill(lambda cp: cp.start())
        fill(lambda cp: cp.wait())

    p_t = _perm_matrix(idx_ref[...], rsf_ref[...], None)
    xs_ref[slot] = lax.dot_general(p_t, x_ref[...], (((0,), (0,)), ((), ())),
                                   preferred_element_type=f32).astype(bf16)

    def copy(buf, src, dst, sz):
        return pltpu.make_async_copy(xs_ref.at[buf].at[pl.ds(src, sz)], s_ref.at[pl.ds(dst, sz)], sem)

    @pl.when(i > 0)
    def _():
        _for_pieces(prev_ref, lambda s, d, sz: copy(1 - slot, s, d, sz).wait())

    _for_pieces(plan_ref, lambda s, d, sz: copy(slot, s, d, sz).start())

    @pl.when(i == pl.num_programs(0) - 1)
    def _():
        _for_pieces(plan_ref, lambda s, d, sz: copy(slot, s, d, sz).wait())


def _ffn_kernel(te_ref, nu_ref, x_ref, wg_ref, wu_ref, wd_ref, o_ref):
    @pl.when(pl.program_id(0) < nu_ref[0])
    def _():
        x = x_ref[...]
        hid = jax.nn.silu(_dot(x, wg_ref[...].astype(bf16))) * _dot(x, wu_ref[...].astype(bf16))
        o_ref[...] = _dot(hid.astype(bf16), wd_ref[...].astype(bf16)).astype(bf16)


def _combine_kernel(z_ref, plan_ref, idx_ref, gate_ref, rsf_ref, x_ref, sgu_ref, sd_ref,
                    h_ref, g2_ref, lng_ref, lnb_ref, o_ref, zbuf_ref, sem, *, alpha):
    i = pl.program_id(0)

    @pl.when(i == 0)
    def _():
        zbuf_ref[...] = jnp.zeros(zbuf_ref.shape, zbuf_ref.dtype)

    def copy(src, dst, sz):
        return pltpu.make_async_copy(z_ref.at[pl.ds(dst, sz)], zbuf_ref.at[pl.ds(src, sz)], sem)

    _for_pieces(plan_ref, lambda s, d, sz: copy(s, d, sz).start())
    hs = _dot(x_ref[...], sgu_ref[...])
    shared = _dot((jax.nn.silu(hs[:, :EXPERT_FF]) * hs[:, EXPERT_FF:]).astype(bf16), sd_ref[...])
    w_t = _perm_matrix(idx_ref[...], rsf_ref[...], gate_ref[...])
    _for_pieces(plan_ref, lambda s, d, sz: copy(s, d, sz).wait())
    y = _dot(w_t, zbuf_ref[...]) + shared
    o_ref[...] = _layer_norm(alpha * h_ref[...] + g2_ref[...] * y, lng_ref[...], lnb_ref[...])


def _route_plan(counts, n_ffn_tiles):
    c = counts[:, 0, :N_EXPERTS].astype(jnp.int32)
    lens = (c + RUN_ALIGN - 1) // RUN_ALIGN * RUN_ALIGN
    runstart = jnp.cumsum(lens, axis=1) - lens
    region = (jnp.sum(lens, axis=0) + FFN_TILE - 1) // FFN_TILE * FFN_TILE
    region_end = jnp.cumsum(region)
    dest = (region_end - region)[None, :] + jnp.cumsum(lens, axis=0) - lens
    n_used = region_end[-1] // FFN_TILE
    tile_row = jnp.minimum(jnp.arange(n_ffn_tiles, dtype=jnp.int32), n_used - 1) * FFN_TILE
    tile_expert = jnp.minimum(jnp.sum(tile_row[:, None] >= region_end[None, :], axis=1), N_EXPERTS - 1)
    runstart_f = jnp.zeros((c.shape[0], 1, LANE), f32).at[:, 0, :N_EXPERTS].set(runstart.astype(f32))

    def compact(mask, *vals):
        pos = jnp.cumsum(mask, axis=1) - mask
        hit = jnp.logical_and(mask[:, :, None] > 0, pos[:, :, None] == jnp.arange(LANE, dtype=jnp.int32))
        return [jnp.sum(jnp.where(hit, v[:, :, None], 0), axis=1) for v in vals], jnp.sum(mask, axis=1)

    n_chunks = lens // RUN_CHUNK
    j = jnp.arange(DISPATCH_TILE // RUN_CHUNK, dtype=jnp.int32) * RUN_CHUNK
    wide = lambda v: (v[:, :, None] + j).reshape(v.shape[0], -1)
    chunk_mask = (j < (n_chunks * RUN_CHUNK)[:, :, None]).astype(jnp.int32).reshape(lens.shape[0], -1)
    (s64, d64), c64 = compact(chunk_mask, wide(runstart), wide(dest))
    off = n_chunks * RUN_CHUNK
    (s32, d32), c32 = compact((lens >> 5) & 1, runstart + off, dest + off)
    off = off + (lens & 32)
    (s16, d16), c16 = compact((lens >> 4) & 1, runstart + off, dest + off)
    count_row = jnp.zeros_like(s64).at[:, 0].set(c64).at[:, 1].set(c32).at[:, 2].set(c16)
    pieces = jnp.stack([s64, d64, s32, d32, s16, d16, count_row, jnp.zeros_like(s64)], axis=1)
    return dict(pieces=pieces.astype(jnp.int32), gap=region_end - FFN_TILE, region=region,
                n_used=n_used.reshape(1), tile_expert=tile_expert.astype(jnp.int32), runstart_f=runstart_f)


def _moe(m, gates, idx, counts, layer, w_gate, w_up, w_down, sgu, sd, h, mods3, ln_g, ln_b, seg, alpha):
    n = h.shape[0]
    t = DISPATCH_TILE
    nt = n // t
    xs_rows = _xs_rows(t)
    rows_max = TOP_K * n + nt * N_EXPERTS * (RUN_ALIGN - 1) + N_EXPERTS * (FFN_TILE - RUN_ALIGN)
    n_ffn_tiles = -(-rows_max // FFN_TILE)
    plan = _route_plan(counts, n_ffn_tiles)
    tile = lambda w: pl.BlockSpec((t, w), lambda i, *_: (i, 0))
    rsf_spec = pl.BlockSpec((None, 1, LANE), lambda i, *_: (i, 0, 0))
    one = lambda a: pl.BlockSpec(a.shape, lambda i, *_: (0,) * a.ndim)
    any_spec = pl.BlockSpec(memory_space=pl.ANY)

    pieces = plan["pieces"]
    plan_spec = pl.BlockSpec((None, PLAN_ROWS, LANE), lambda i, *_: (i, 0, 0), memory_space=pltpu.SMEM)
    prev_spec = pl.BlockSpec((None, PLAN_ROWS, LANE), lambda i, *_: (jnp.maximum(i - 1, 0), 0, 0),
                             memory_space=pltpu.SMEM)
    sorted_x = pl.pallas_call(
        _dispatch_kernel,
        out_shape=jax.ShapeDtypeStruct((n_ffn_tiles * FFN_TILE, D_MODEL), bf16),
        grid_spec=pltpu.PrefetchScalarGridSpec(
            num_scalar_prefetch=2, grid=(nt,),
            in_specs=[tile(D_MODEL), tile(LANE), rsf_spec, plan_spec, prev_spec],
            out_specs=any_spec,
            scratch_shapes=[pltpu.VMEM((2, xs_rows, D_MODEL), bf16), pltpu.VMEM((FFN_TILE, D_MODEL), bf16),
                            pltpu.SemaphoreType.DMA]),
        compiler_params=_cparams(("arbitrary",)),
        name="moe_dispatch",
    )(plan["gap"], plan["region"], m, idx, plan["runstart_f"], pieces, pieces)

    used = lambda i, te, nu: jnp.minimum(i, nu[0] - 1)
    sorted_z = pl.pallas_call(
        _ffn_kernel,
        out_shape=jax.ShapeDtypeStruct((n_ffn_tiles * FFN_TILE, D_MODEL), bf16),
        grid_spec=pltpu.PrefetchScalarGridSpec(
            num_scalar_prefetch=2, grid=(n_ffn_tiles,),
            in_specs=[pl.BlockSpec((FFN_TILE, D_MODEL), lambda i, te, nu: (used(i, te, nu), 0)),
                      pl.BlockSpec((None, None, D_MODEL, EXPERT_FF), lambda i, te, nu: (layer, te[i], 0, 0)),
                      pl.BlockSpec((None, None, D_MODEL, EXPERT_FF), lambda i, te, nu: (layer, te[i], 0, 0)),
                      pl.BlockSpec((None, None, EXPERT_FF, D_MODEL), lambda i, te, nu: (layer, te[i], 0, 0))],
            out_specs=pl.BlockSpec((FFN_TILE, D_MODEL), lambda i, te, nu: (used(i, te, nu), 0))),
        compiler_params=_cparams(("arbitrary",)),
        name="moe_ffn",
    )(plan["tile_expert"], plan["n_used"], sorted_x, w_gate, w_up, w_down)

    return pl.pallas_call(
        functools.partial(_combine_kernel, alpha=alpha),
        out_shape=jax.ShapeDtypeStruct((n, D_MODEL), f32),
        grid=(nt,),
        in_specs=[any_spec, plan_spec, tile(LANE), tile(LANE), rsf_spec, tile(D_MODEL), one(sgu), one(sd),
                  tile(D_MODEL), pl.BlockSpec((None, 1, D_MODEL), lambda i, *_: (seg(i), 0, 5)),
                  one(ln_g), one(ln_b)],
        out_specs=tile(D_MODEL),
        scratch_shapes=[pltpu.VMEM((xs_rows, D_MODEL), bf16), pltpu.SemaphoreType.DMA],
        compiler_params=_cparams(("arbitrary",)),
        name="moe_combine",
    )(sorted_z, pieces, idx, gates, plan["runstart_f"], m, sgu, sd, h, mods3, ln_g, ln_b)


def _pad_heads_cols(w, n_heads):
    lead = w.shape[:-1]
    w = w.reshape(*lead, n_heads, HEAD_DIM)
    w = jnp.concatenate([w, jnp.zeros_like(w)], axis=-1)
    return w.reshape(*lead, n_heads * HEAD_PAD)


def _pad_in_proj(w_in):
    hy, q, k, v, rgx, rgg = (w_in[..., :HY_END], w_in[..., HY_END:Q_END], w_in[..., Q_END:K_END],
                             w_in[..., K_END:V_END], w_in[..., V_END:RGX_END], w_in[..., RGX_END:])
    return jnp.concatenate([hy, _pad_heads_cols(q, N_HEADS), _pad_heads_cols(k, N_KV_HEADS),
                            _pad_heads_cols(v, N_KV_HEADS), rgx, rgg], axis=-1).astype(bf16)


def kernel(x, c, ctx, c_ctx, w_mod, b_mod, ln1_g, ln1_b, ln2_g, ln2_b, w_in, w_out,
           hy_short_w, hy_short_b, hy_f_w1, hy_f_b1, hy_f_freq, hy_f_w2, hy_f_b2, hy_f_w3, hy_skip,
           q_norm, k_norm, rg_conv_w, rg_conv_b, rg_lambda, rg_w_a, rg_b_a, rg_w_x, rg_b_x,
           w_router, b_router, w_gate, w_up, w_down, ws_gate, ws_up, ws_down):
    B, T, D = x.shape
    Lc = ctx.shape[1]
    depth = w_mod.shape[0]
    bt, bc = B * T, B * Lc
    n_tok = bt + bc
    tm = TOKEN_TILE
    assert D == D_MODEL and T % tm == 0 and bc % tm == 0 and B % 2 == 0
    assert T % ATT_Q_TILE == 0 and Lc == ATT_Q_TILE and T % GRID_W == 0
    alpha = (2 * depth) ** 0.25
    tiles_per_seq = T // tm
    seg = lambda i: jnp.minimum(i // tiles_per_seq, B)
    pos_block = lambda i: jnp.where(i < B * tiles_per_seq, i % tiles_per_seq, tiles_per_seq)
    moe_tm = DISPATCH_TILE
    assert T % moe_tm == 0 and bc % moe_tm == 0
    moe_seg = lambda i: jnp.minimum(i // (T // moe_tm), B)

    rows = -(-(B + 1) // 8) * 8
    cond = jnp.zeros((rows, D), f32).at[:B].set(c).at[B].set(c_ctx)
    mods = _modulation(cond, w_mod, b_mod)

    cos, sin = _rope_tables(T, tm)
    w_in_pad = _pad_in_proj(w_in)
    w_out_hy = w_out[:, :HY_W].astype(bf16)
    w_out_att = _pad_heads_cols(w_out[:, HY_W:HY_W + ATT_W].transpose(0, 2, 1), N_HEADS).transpose(0, 2, 1).astype(bf16)
    w_out_rg = w_out[:, HY_W + ATT_W:].astype(bf16)
    wr_pad = jnp.zeros((depth, D, LANE), f32).at[:, :, :N_EXPERTS].set(w_router)
    br_pad = jnp.full((depth, 1, LANE), -jnp.inf, f32).at[:, 0, :N_EXPERTS].set(b_router)
    sgu = jnp.concatenate([ws_gate, ws_up], axis=-1).astype(bf16)
    sd = ws_down.astype(bf16)

    h = jnp.concatenate([x.reshape(bt, D), ctx.reshape(bc, D)], axis=0)
    for l in range(depth):
        mods3 = mods[l].reshape(rows, 1, 6 * D)
        hy, q, kv, rg = _inproj(h, mods3, w_in_pad[l], seg, tm)

        xz_l = _dwconv(hy, hy_short_w[l], hy_short_b[l], T, 0, B, 0, 3)
        xz_c = _dwconv(hy, hy_short_w[l], hy_short_b[l], Lc, bt // Lc, B, 0, 3)
        fargs = (hy_f_w1[l], hy_f_b1[l], hy_f_freq[l], hy_f_w2[l], hy_f_b2[l], hy_f_w3[l])
        hy_l = _hyena_long(xz_l, _hyena_filters(T, *fargs), hy_skip[l], B, T)
        hy_c = _hyena_short_seq(xz_c, _hyena_filters(Lc, *fargs), hy_skip[l], B, Lc)
        hy_out = jnp.concatenate([hy_l, hy_c], axis=0)

        qp, kvp = _qkprep(q, kv, cos, sin, q_norm[l], k_norm[l], pos_block, tm)
        att = _attention(qp, kvp, B, T, Lc)

        rg_out = _rglru(rg, B, T, Lc, rg_conv_w[l], rg_conv_b[l], rg_lambda[l],
                        rg_w_a[l], rg_b_a[l], rg_w_x[l], rg_b_x[l])

        wr_hi, wr_lo = _split(wr_pad[l])
        h, m, gates, chosen, counts = _outproj(
            hy_out, att, rg_out, h, mods3, ln1_g[l][None, :], ln1_b[l][None, :],
            w_out_hy[l], w_out_att[l], w_out_rg[l], wr_hi, wr_lo, br_pad[l], moe_seg, moe_tm, alpha)
        h = _moe(m, gates, chosen, counts, l, w_gate, w_up, w_down, sgu[l], sd[l], h, mods3,
                 ln2_g[l][None, :], ln2_b[l][None, :], moe_seg, alpha)
    return h[:bt].reshape(B, T, D)
```

```python
import functools
import math

import numpy as np
import jax
import jax.numpy as jnp
from jax import lax
from jax.experimental import pallas as pl
from jax.experimental.pallas import tpu as pltpu

f32 = jnp.float32
bf16 = jnp.bfloat16

D_MODEL = 1024
GRID_W = 64
HY_W = 256
N_HEADS = 8
N_KV_HEADS = 2
N_GROUPS = N_HEADS // N_KV_HEADS
HEAD_DIM = 64
ATT_W = N_HEADS * HEAD_DIM
KV_W = N_KV_HEADS * HEAD_DIM
RG_W = 256
RG_BLOCKS = 4
HY_END = 3 * HY_W
Q_END = HY_END + ATT_W
K_END = Q_END + KV_W
V_END = K_END + KV_W
RGX_END = V_END + RG_W
PROJ_W = RGX_END + RG_W
HY_BANDS = 8
HY_EMB = 1 + 2 * HY_BANDS
HY_MAX_DECAY = math.log(1e-2) / 0.3
HY_MIN_DECAY = math.log(1e-2) / 1.5
ROPE_THETA = 10000.0
QK_EPS = 1e-6
RG_C = 8.0
N_EXPERTS = 64
TOP_K = 8
EXPERT_FF = 256
ROUTED_SCALE = 2.5
LN_EPS = 1e-6

LANE = 128
HEAD_PAD = LANE
QP_W = N_HEADS * HEAD_PAD
KVP_W = N_KV_HEADS * HEAD_PAD
PROJ_PAD_W = HY_END + QP_W + 2 * KVP_W + 2 * RG_W
VMEM_LIMIT = 52 * 1024 * 1024

FFT_N2 = 128
FFT_PASSES = 1
FILTER_PASSES = 3
TOKEN_TILE = 512
ATT_Q_TILE = 256
SCAN_TILE = 256
DISPATCH_TILE = 256
ROUTER_TILES = (1024, 512, 256)
RUN_ALIGN = 16
RUN_CHUNK = 64
PIECE_SIZES = (RUN_CHUNK, 32, 16)
PLAN_ROWS = 8
PLAN_COUNT_ROW = 6
FFN_TILE = 1024


def _cparams(sem):
    return pltpu.CompilerParams(dimension_semantics=sem, vmem_limit_bytes=VMEM_LIMIT)


def _dot(a, b):
    return jnp.dot(a, b, preferred_element_type=f32)


def _split(x):
    hi = x.astype(bf16)
    lo = (x - hi.astype(f32)).astype(bf16)
    return hi, lo


def _np_split(x):
    x = jnp.asarray(np.asarray(x, np.float32))
    return _split(x)


def _dot3(a, w_hi, w_lo):
    a_hi, a_lo = _split(a)
    return _dot(a_hi, w_hi) + _dot(a_lo, w_hi) + _dot(a_hi, w_lo)


def _stack_dft(mat):
    re_hi, re_lo = _np_split(mat.real)
    im_hi, im_lo = _np_split(mat.imag)
    return jnp.concatenate([re_hi, im_hi, re_lo, im_lo], axis=0)


def _apply_stack(stack, x, passes):
    m2 = stack.shape[0] // 2
    x_hi, x_lo = _split(x)
    if passes == 1:
        return _dot(stack[:m2], x_hi)
    r = _dot(stack, x_hi)
    out = r[:m2] + r[m2:]
    if passes >= 3:
        out = out + _dot(stack[:m2], x_lo)
    return out


def _mod_kernel(a_ref, w_ref, b_ref, o_ref):
    a = a_ref[...]
    a = a * jax.nn.sigmoid(a)
    w_hi, w_lo = _split(w_ref[...])
    o_ref[...] = _dot3(a, w_hi, w_lo) + b_ref[...]


def _modulation(cond, w_mod, b_mod):
    depth, d, n = w_mod.shape
    rows = cond.shape[0]
    tn = 512
    return pl.pallas_call(
        _mod_kernel,
        out_shape=jax.ShapeDtypeStruct((depth, rows, n), f32),
        grid=(depth, n // tn),
        in_specs=[
            pl.BlockSpec((rows, d), lambda l, j: (0, 0)),
            pl.BlockSpec((None, d, tn), lambda l, j: (l, 0, j)),
            pl.BlockSpec((None, 1, tn), lambda l, j: (l, 0, j)),
        ],
        out_specs=pl.BlockSpec((None, rows, tn), lambda l, j: (l, 0, j)),
        compiler_params=_cparams(("parallel", "parallel")),
        name="modulation",
    )(cond, w_mod, b_mod.reshape(depth, 1, n))


def _inproj_kernel(h_ref, sh_ref, sc_ref, w_ref, ohy_ref, oq_ref, okv_ref, org_ref):
    a = h_ref[...] * (1.0 + sc_ref[...]) + sh_ref[...]
    p = _dot(a.astype(bf16), w_ref[...])
    c0, c1, c2 = HY_END, HY_END + QP_W, HY_END + QP_W + 2 * KVP_W
    ohy_ref[...] = p[:, :c0]
    oq_ref[...] = p[:, c0:c1]
    okv_ref[...] = p[:, c1:c2]
    org_ref[...] = p[:, c2:]


def _mod_spec(seg, j):
    return pl.BlockSpec((None, 1, D_MODEL), lambda i: (seg(i), 0, j))


def _inproj(h, mods3, w_pad, seg, tm):
    n = h.shape[0]
    widths = (HY_END, QP_W, 2 * KVP_W, 2 * RG_W)
    return pl.pallas_call(
        _inproj_kernel,
        out_shape=[jax.ShapeDtypeStruct((n, w), f32) for w in widths],
        grid=(n // tm,),
        in_specs=[
            pl.BlockSpec((tm, D_MODEL), lambda i: (i, 0)),
            _mod_spec(seg, 0),
            _mod_spec(seg, 1),
            pl.BlockSpec((D_MODEL, PROJ_PAD_W), lambda i: (0, 0)),
        ],
        out_specs=[pl.BlockSpec((tm, w), lambda i: (i, 0)) for w in widths],
        compiler_params=_cparams(("parallel",)),
        name="inproj",
    )(h, mods3, mods3, w_pad)


def _dwconv_kernel(x_ref, w_ref, b_ref, o_ref, *, width, left):
    x = x_ref[...]
    n = x.shape[0]
    row = lax.broadcasted_iota(jnp.int32, x.shape, 0)
    acc = jnp.zeros_like(x) + b_ref[...]
    for j in range(width):
        off = j - left
        if off == 0:
            xs = x
        else:
            xs = pltpu.roll(x, (-off) % n, axis=0)
            valid = jnp.logical_and(row + off >= 0, row + off < n)
            xs = jnp.where(valid, xs, 0.0)
        acc = acc + xs * w_ref[j:j + 1, :]
    o_ref[...] = acc


def _dwconv(x, w, b, seq_len, row_block0, n_seq, col_block0, n_col_blocks):
    width = w.shape[0]
    wp = jnp.zeros((8, w.shape[1]), f32).at[:width].set(w)
    ct = 256
    return pl.pallas_call(
        functools.partial(_dwconv_kernel, width=width, left=(width - 1) // 2),
        out_shape=jax.ShapeDtypeStruct((n_seq * seq_len, n_col_blocks * ct), f32),
        grid=(n_seq, n_col_blocks),
        in_specs=[
            pl.BlockSpec((seq_len, ct), lambda s, c: (row_block0 + s, col_block0 + c)),
            pl.BlockSpec((8, ct), lambda s, c: (0, c)),
            pl.BlockSpec((1, ct), lambda s, c: (0, c)),
        ],
        out_specs=pl.BlockSpec((seq_len, ct), lambda s, c: (s, c)),
        compiler_params=_cparams(("parallel", "parallel")),
        name="dwconv",
    )(x, wp, b.reshape(1, -1))


def _filter_kernel(z_ref, w1h_ref, w1l_ref, b1_ref, fr_ref, w2h_ref, w2l_ref, b2_ref,
                   w3h_ref, w3l_ref, win_ref, o_ref):
    fr = fr_ref[...]
    f = jnp.sin(fr * (_dot3(z_ref[...], w1h_ref[...], w1l_ref[...]) + b1_ref[...]))
    f = jnp.sin(fr * (_dot3(f, w2h_ref[...], w2l_ref[...]) + b2_ref[...]))
    f = _dot3(f, w3h_ref[...], w3l_ref[...])
    win = win_ref[...]
    o_ref[...] = f * jnp.concatenate([win] * 2, axis=1)


def _filter_features(L):
    t = np.linspace(0.0, 1.0, L, dtype=np.float32)[:, None].astype(np.float64)
    w = np.float32(2.0 * math.pi) * np.arange(L, dtype=np.float32)[:, None] / np.float32(L)
    bands = np.linspace(1e-4, HY_BANDS - 1, HY_BANDS, dtype=np.float32)
    arg = (bands * w).astype(np.float64)
    z = np.concatenate([t, np.cos(arg), -np.sin(arg)], axis=-1)
    deltas = np.abs(np.linspace(HY_MIN_DECAY, HY_MAX_DECAY, HY_W, dtype=np.float32)).astype(np.float64)
    win = np.exp(-t * deltas)
    zp = np.zeros((2 * L, LANE), np.float32)
    zp[:L, :HY_EMB] = z
    zp[L + 1:, :HY_EMB] = z[:0:-1]
    win2 = np.zeros((2 * L, HY_W), np.float32)
    win2[:L] = win
    win2[L + 1:] = win[:0:-1]
    return jnp.asarray(zp), jnp.asarray(win2)


def _pad2(w, rows, cols):
    return jnp.zeros((rows, cols), f32).at[:w.shape[0], :w.shape[1]].set(w)


def _hyena_filters(L, w1, b1, freq, w2, b2, w3):
    zp, win = _filter_features(L)
    w1h, w1l = _split(_pad2(w1, LANE, LANE))
    w2h, w2l = _split(_pad2(w2, LANE, LANE))
    w3h, w3l = _split(_pad2(w3, LANE, 4 * HY_W))
    b1p = _pad2(b1[None, :], 1, LANE)
    b2p = _pad2(b2[None, :], 1, LANE)
    frp = _pad2(freq[None, :], 1, LANE)
    tl = min(L, 512)
    half = L // tl
    full = lambda shape: pl.BlockSpec(shape, lambda i: (0, 0))
    w3_spec = pl.BlockSpec((LANE, 2 * HY_W), lambda i: (0, i // half))
    return pl.pallas_call(
        _filter_kernel,
        out_shape=jax.ShapeDtypeStruct((2 * L, 2 * HY_W), f32),
        grid=(2 * half,),
        in_specs=[
            pl.BlockSpec((tl, LANE), lambda i: (i, 0)),
            full((LANE, LANE)), full((LANE, LANE)), full((1, LANE)), full((1, LANE)),
            full((LANE, LANE)), full((LANE, LANE)), full((1, LANE)),
            w3_spec, w3_spec,
            pl.BlockSpec((tl, HY_W), lambda i: (i, 0)),
        ],
        out_specs=pl.BlockSpec((tl, 2 * HY_W), lambda i: (i, 0)),
        compiler_params=_cparams(("parallel",)),
        name="hyena_filter",
    )(zp, w1h, w1l, b1p, frp, w2h, w2l, b2p, w3h, w3l, win)


def _dft_consts(n):
    n1 = n // FFT_N2
    k = np.arange(n1)
    f_n1 = np.exp(-2j * np.pi * np.outer(k, k) / n1)
    k2 = np.arange(FFT_N2)
    f_n2 = np.exp(-2j * np.pi * np.outer(k2, k2) / FFT_N2)
    tw = np.exp(-2j * np.pi * np.outer(k, k2) / n).reshape(n, 1)
    tw_re = jnp.asarray(np.broadcast_to(tw.real, (n, LANE)).astype(np.float32))
    tw_im = jnp.asarray(np.broadcast_to(tw.imag, (n, LANE)).astype(np.float32))
    return n1, f_n1, f_n2, tw_re, tw_im


def _fa_kernel(u_ref, fs_ref, are_ref, aim_ref, *, grp, passes, real_only):
    fs = fs_ref[...]
    n1 = fs.shape[0] // 4
    jb, c = u_ref.shape[2], u_ref.shape[3]
    for g in range(jb // grp):
        def gather(bi):
            return jnp.concatenate([u_ref[bi, :, g * grp + jj, :] for jj in range(grp)], axis=1)
        p = _apply_stack(fs, gather(0), passes)
        if real_only:
            re, im = p[:n1], p[n1:]
        else:
            q = _apply_stack(fs, gather(1), passes)
            re, im = p[:n1] - q[n1:], p[n1:] + q[:n1]
        for jj in range(grp):
            are_ref[:, g * grp + jj, :] = re[:, jj * c:(jj + 1) * c]
            aim_ref[:, g * grp + jj, :] = im[:, jj * c:(jj + 1) * c]


def _fft_step_a(u4, fs, col_block, c, real_only, passes):
    s, n1_in = u4.shape[0], u4.shape[1]
    n1 = fs.shape[0] // 4
    per = 1 if real_only else 2
    jb = 32
    grp = max(1, 1024 // c)
    shape = jax.ShapeDtypeStruct((s // per, n1, FFT_N2, c), f32)
    return pl.pallas_call(
        functools.partial(_fa_kernel, grp=grp, passes=passes, real_only=real_only),
        out_shape=[shape, shape],
        grid=(s // per, FFT_N2 // jb),
        in_specs=[
            pl.BlockSpec((per, n1_in, jb, c), lambda p, j: (p, 0, j, col_block)),
            pl.BlockSpec(fs.shape, lambda p, j: (0, 0)),
        ],
        out_specs=[pl.BlockSpec((None, n1, jb, c), lambda p, j: (p, 0, j, 0))] * 2,
        compiler_params=_cparams(("parallel", "parallel")),
        name="fft_step_a",
    )(u4, fs)


def _mid_kernel(are_ref, aim_ref, twr_ref, twi_ref, f2_ref, *rest, kb, passes, fwd_only, inv_n):
    if fwd_only:
        ore_ref, oim_ref = rest
    else:
        kr_ref, ki_ref, ore_ref, oim_ref = rest
    f2 = f2_ref[...]
    c = are_ref.shape[1]
    for kk in range(kb):
        rows = pl.ds(kk * FFT_N2, FFT_N2)
        ar, ai = are_ref[rows, :], aim_ref[rows, :]
        tr = jnp.concatenate([twr_ref[rows, :]] * (c // LANE), axis=1)
        ti = jnp.concatenate([twi_ref[rows, :]] * (c // LANE), axis=1)
        xr = ar * tr - ai * ti
        xi = ar * ti + ai * tr
        p = _apply_stack(f2, xr, passes)
        q = _apply_stack(f2, xi, passes)
        sr = p[:FFT_N2] - q[FFT_N2:]
        si = p[FFT_N2:] + q[:FFT_N2]
        if fwd_only:
            ore_ref[rows, :] = sr
            oim_ref[rows, :] = si
        else:
            kr, ki = kr_ref[rows, :], ki_ref[rows, :]
            yr = sr * kr - si * ki
            yi = sr * ki + si * kr
            p2 = _apply_stack(f2, yr, passes)
            q2 = _apply_stack(f2, yi, passes)
            br = p2[:FFT_N2] + q2[FFT_N2:]
            bi = q2[:FFT_N2] - p2[FFT_N2:]
            ore_ref[rows, :] = (br * tr + bi * ti) * inv_n
            oim_ref[rows, :] = (bi * tr - br * ti) * inv_n


def _fft_mid(a_re, a_im, tw_re, tw_im, f2s, kf=None, order=0, passes=FFT_PASSES):
    p, n, c = a_re.shape
    kb = min(4, n // FFT_N2)
    rb = kb * FFT_N2
    fwd_only = kf is None
    blk = pl.BlockSpec((None, rb, c), lambda i, k: (i, k, 0))
    in_specs = [blk, blk,
                pl.BlockSpec((rb, LANE), lambda i, k: (k, 0)),
                pl.BlockSpec((rb, LANE), lambda i, k: (k, 0)),
                pl.BlockSpec(f2s.shape, lambda i, k: (0, 0))]
    args = [a_re, a_im, tw_re, tw_im, f2s]
    if not fwd_only:
        in_specs += [pl.BlockSpec((rb, c), lambda i, k: (k, order))] * 2
        args += list(kf)
    shape = jax.ShapeDtypeStruct((p, n, c), f32)
    return pl.pallas_call(
        functools.partial(_mid_kernel, kb=kb, passes=passes, fwd_only=fwd_only, inv_n=1.0 / n),
        out_shape=[shape, shape],
        grid=(p, n // rb),
        in_specs=in_specs,
        out_specs=[blk, blk],
        compiler_params=_cparams(("parallel", "parallel")),
        name="fft_mid",
    )(*args)


def _fai_kernel(bre_ref, bim_ref, gs_ref, u_ref, gate_ref, skip_ref, o_ref, *, grp, passes):
    gs = gs_ref[...]
    n1h = gs.shape[0] // 4
    jb, c = u_ref.shape[2], u_ref.shape[3]
    skip = skip_ref[...]
    for g in range(jb // grp):
        def gather(ref):
            return jnp.concatenate([ref[:, g * grp + jj, :] for jj in range(grp)], axis=1)
        p = _apply_stack(gs, gather(bre_ref), passes)
        q = _apply_stack(gs, gather(bim_ref), passes)
        ya = p[:n1h] - q[n1h:]
        yb = p[n1h:] + q[:n1h]
        for jj in range(grp):
            j = g * grp + jj
            for bi, y in ((0, ya), (1, yb)):
                u = u_ref[bi, :, j, :]
                o_ref[bi, :, j, :] = gate_ref[bi, :, j, :] * (y[:, jj * c:(jj + 1) * c] + u * skip)


def _fft_step_a_inv(b_re, b_im, gs, u4, u_col, gate4, gate_col, skip, passes):
    p, n1, _, c = b_re.shape
    n1h = n1 // 2
    jb = 32
    grp = max(1, 1024 // c)
    bspec = pl.BlockSpec((None, n1, jb, c), lambda i, j: (i, 0, j, 0))
    return pl.pallas_call(
        functools.partial(_fai_kernel, grp=grp, passes=passes),
        out_shape=jax.ShapeDtypeStruct((2 * p, n1h, FFT_N2, c), f32),
        grid=(p, FFT_N2 // jb),
        in_specs=[
            bspec, bspec,
            pl.BlockSpec(gs.shape, lambda i, j: (0, 0)),
            pl.BlockSpec((2, n1h, jb, c), lambda i, j: (i, 0, j, u_col)),
            pl.BlockSpec((2, n1h, jb, c), lambda i, j: (i, 0, j, gate_col)),
            pl.BlockSpec((1, c), lambda i, j: (0, 0)),
        ],
        out_specs=pl.BlockSpec((2, n1h, jb, c), lambda i, j: (i, 0, j, 0)),
        compiler_params=_cparams(("parallel", "parallel")),
        name="fft_step_a_inv",
    )(b_re, b_im, gs, u4, gate4, skip)


def _hyena_long(xz, filt, skip, n_seq, L, passes=FFT_PASSES):
    n = 2 * L
    n1, f_n1, f_n2, tw_re, tw_im = _dft_consts(n)
    n1h = n1 // 2
    c = HY_W
    fs_full = _stack_dft(f_n1)
    fs_half = _stack_dft(f_n1[:, :n1h])
    gs = _stack_dft(np.conj(f_n1)[:n1h, :])
    f2s = _stack_dft(f_n2)
    kc4 = filt.reshape(1, n1, FFT_N2, 2 * c)
    k_re, k_im = _fft_step_a(kc4, fs_full, 0, 2 * c, True, FILTER_PASSES)
    kf = _fft_mid(k_re.reshape(1, n, 2 * c), k_im.reshape(1, n, 2 * c), tw_re, tw_im, f2s,
                  passes=FILTER_PASSES)
    kf = (kf[0].reshape(n, 2 * c), kf[1].reshape(n, 2 * c))
    xz4 = xz.reshape(n_seq, n1h, FFT_N2, 3 * c)
    z4, z_col = xz4, 2
    for order in range(2):
        a_re, a_im = _fft_step_a(z4, fs_half, z_col, c, False, passes)
        p = n_seq // 2
        b_re, b_im = _fft_mid(a_re.reshape(p, n, c), a_im.reshape(p, n, c), tw_re, tw_im, f2s,
                              kf=kf, order=order, passes=passes)
        z4 = _fft_step_a_inv(b_re.reshape(p, n1, FFT_N2, c), b_im.reshape(p, n1, FFT_N2, c), gs,
                             z4, z_col, xz4, order, skip[order:order + 1], passes)
        z_col = 0
    return z4.reshape(n_seq * L, c)


def _dense_spec_kernel(k_ref, fs_ref, ore_ref, oim_ref, *, passes):
    n = k_ref.shape[0]
    p = _apply_stack(fs_ref[...], k_ref[...], passes)
    ore_ref[...] = p[:n]
    oim_ref[...] = p[n:]


def _dense_conv_kernel(xz_ref, fs_ref, gs_ref, kr_ref, ki_ref, skip_ref, o_ref, *, passes):
    L = xz_ref.shape[1]
    n = 2 * L
    c = HY_W
    fs, gs = fs_ref[...], gs_ref[...]
    za, zb = xz_ref[0, :, 2 * c:], xz_ref[1, :, 2 * c:]
    for order in range(2):
        p = _apply_stack(fs, za, passes)
        q = _apply_stack(fs, zb, passes)
        sr, si = p[:n] - q[n:], p[n:] + q[:n]
        kr, ki = kr_ref[:, order * c:(order + 1) * c], ki_ref[:, order * c:(order + 1) * c]
        yr, yi = sr * kr - si * ki, sr * ki + si * kr
        p2 = _apply_stack(gs, yr, passes)
        q2 = _apply_stack(gs, yi, passes)
        ya = (p2[:L] - q2[L:]) * (1.0 / n)
        yb = (p2[L:] + q2[:L]) * (1.0 / n)
        skip = skip_ref[order:order + 1, :]
        za = xz_ref[0, :, order * c:(order + 1) * c] * (ya + za * skip)
        zb = xz_ref[1, :, order * c:(order + 1) * c] * (yb + zb * skip)
    o_ref[0] = za
    o_ref[1] = zb


def _hyena_short_seq(xz, filt, skip, n_seq, L, passes=FFT_PASSES):
    n = 2 * L
    c = HY_W
    k = np.arange(n)
    f_n = np.exp(-2j * np.pi * np.outer(k, k) / n)
    fs_full = _stack_dft(f_n)
    fs_half = _stack_dft(f_n[:, :L])
    gs = _stack_dft(np.conj(f_n)[:L, :])
    kc = filt
    shape = jax.ShapeDtypeStruct((n, 2 * c), f32)
    k_re, k_im = pl.pallas_call(
        functools.partial(_dense_spec_kernel, passes=FILTER_PASSES),
        out_shape=[shape, shape],
        compiler_params=_cparams(None),
        name="dense_filter_spectrum",
    )(kc, fs_full)
    skip_p = jnp.zeros((8, c), f32).at[:2].set(skip)
    full = lambda a: pl.BlockSpec(a.shape, lambda i: (0,) * a.ndim)
    out = pl.pallas_call(
        functools.partial(_dense_conv_kernel, passes=passes),
        out_shape=jax.ShapeDtypeStruct((n_seq, L, c), f32),
        grid=(n_seq // 2,),
        in_specs=[pl.BlockSpec((2, L, 3 * c), lambda i: (i, 0, 0)),
                  full(fs_half), full(gs), full(k_re), full(k_im), full(skip_p)],
        out_specs=pl.BlockSpec((2, L, c), lambda i: (i, 0, 0)),
        compiler_params=_cparams(("parallel",)),
        name="dense_long_conv",
    )(xz.reshape(n_seq, L, 3 * c), fs_half, gs, k_re, k_im, skip_p)
    return out.reshape(n_seq * L, c)


def _rope_tables(T, tm):
    n_rows = T // GRID_W
    row = np.repeat(np.arange(n_rows, dtype=np.float32), GRID_W)
    col = np.tile(np.arange(GRID_W, dtype=np.float32), n_rows)
    axis_dim = HEAD_DIM // 2
    inv_freq = np.float32(ROPE_THETA) ** (-np.arange(0, axis_dim, 2, dtype=np.float32) / np.float32(axis_dim))
    ang = np.concatenate([row[:, None] * inv_freq, col[:, None] * inv_freq], axis=-1).astype(np.float64)
    cos = np.zeros((T + tm, HEAD_PAD), np.float32)
    sin = np.zeros((T + tm, HEAD_PAD), np.float32)
    cos[:T, 0:HEAD_DIM:2] = np.cos(ang)
    cos[:T, 1:HEAD_DIM:2] = np.cos(ang)
    sin[:T, 0:HEAD_DIM:2] = -np.sin(ang)
    sin[:T, 1:HEAD_DIM:2] = np.sin(ang)
    cos[T:, :HEAD_DIM] = 1.0
    return jnp.asarray(cos), jnp.asarray(sin)


def _qkprep_kernel(q_ref, kv_ref, cos_ref, sin_ref, qg_ref, kg_ref, avg_ref, oq_ref, okv_ref):
    cos, sin = cos_ref[...], sin_ref[...]
    avg = avg_ref[...]
    lane = lax.broadcasted_iota(jnp.int32, cos.shape, 1)
    even = (lane % 2) == 0

    def norm_rope(x, gain, scale):
        sq_hi, sq_lo = _split(x * x)
        ms = _dot(sq_hi, avg) + _dot(sq_lo, avg)
        xn = x * lax.rsqrt(ms + QK_EPS) * gain
        swapped = jnp.where(even, pltpu.roll(xn, LANE - 1, axis=1), pltpu.roll(xn, 1, axis=1))
        return ((xn * cos + swapped * sin) * scale).astype(bf16)

    for h in range(N_HEADS):
        sl = slice(h * HEAD_PAD, (h + 1) * HEAD_PAD)
        oq_ref[:, sl] = norm_rope(q_ref[:, sl], qg_ref[...], HEAD_DIM ** -0.5)
    for g in range(N_KV_HEADS):
        sl = slice(g * HEAD_PAD, (g + 1) * HEAD_PAD)
        okv_ref[:, sl] = norm_rope(kv_ref[:, sl], kg_ref[...], 1.0)
    v = kv_ref[:, KVP_W:]
    v_lane = lax.broadcasted_iota(jnp.int32, v.shape, 1) % HEAD_PAD
    okv_ref[:, KVP_W:] = jnp.where(v_lane == HEAD_DIM, 1.0, v).astype(bf16)


def _qkprep(q, kv, cos, sin, q_gain, k_gain, pos_block, tm):
    n = q.shape[0]
    pad = lambda g: jnp.zeros((1, HEAD_PAD), f32).at[0, :HEAD_DIM].set(g)
    avg = jnp.full((HEAD_PAD, HEAD_PAD), 1.0 / HEAD_DIM, bf16)
    one = lambda shape: pl.BlockSpec(shape, lambda i: (0, 0))
    return pl.pallas_call(
        _qkprep_kernel,
        out_shape=[jax.ShapeDtypeStruct((n, QP_W), bf16), jax.ShapeDtypeStruct((n, 2 * KVP_W), bf16)],
        grid=(n // tm,),
        in_specs=[
            pl.BlockSpec((tm, QP_W), lambda i: (i, 0)),
            pl.BlockSpec((tm, 2 * KVP_W), lambda i: (i, 0)),
            pl.BlockSpec((tm, HEAD_PAD), lambda i: (pos_block(i), 0)),
            pl.BlockSpec((tm, HEAD_PAD), lambda i: (pos_block(i), 0)),
            one((1, HEAD_PAD)), one((1, HEAD_PAD)), one((HEAD_PAD, HEAD_PAD)),
        ],
        out_specs=[pl.BlockSpec((tm, QP_W), lambda i: (i, 0)),
                   pl.BlockSpec((tm, 2 * KVP_W), lambda i: (i, 0))],
        compiler_params=_cparams(("parallel",)),
        name="qk_prep",
    )(q, kv, cos, sin, pad(q_gain), pad(k_gain), avg)


def _attn_kernel(q_ref, kvl_ref, kvc_ref, o_ref, *, nq_lat):
    i = pl.program_id(1)
    dims = (((1,), (1,)), ((), ()))

    def scores(h, use_lat):
        g = h // N_GROUPS
        ks = slice(g * HEAD_PAD, (g + 1) * HEAD_PAD)
        q = q_ref[:, h * HEAD_PAD:(h + 1) * HEAD_PAD]
        sc = lax.dot_general(q, kvc_ref[:, ks], dims, preferred_element_type=f32).astype(bf16)
        sl = None
        if use_lat:
            sl = lax.dot_general(q, kvl_ref[:, ks], dims, preferred_element_type=f32).astype(bf16)
        return sl, sc

    def heads(use_lat):
        nxt = scores(0, use_lat)
        for h in range(N_HEADS):
            sl, sc = nxt
            if h + 1 < N_HEADS:
                nxt = scores(h + 1, use_lat)
            g = h // N_GROUPS
            vs = slice(KVP_W + g * HEAD_PAD, KVP_W + (g + 1) * HEAD_PAD)
            m = jnp.max(sc, axis=-1, keepdims=True)
            if use_lat:
                m = jnp.maximum(m, jnp.max(sl, axis=-1, keepdims=True))
                acc = _dot(jnp.exp(sl - m), kvl_ref[:, vs]) + _dot(jnp.exp(sc - m), kvc_ref[:, vs])
            else:
                acc = _dot(jnp.exp(sc - m), kvc_ref[:, vs])
            o_ref[:, h * HEAD_PAD:(h + 1) * HEAD_PAD] = (acc / acc[:, HEAD_DIM:HEAD_DIM + 1]).astype(bf16)

    @pl.when(i < nq_lat)
    def _():
        heads(True)

    @pl.when(i == nq_lat)
    def _():
        heads(False)


def _attention(qp, kvp, B, T, Lc):
    n = qp.shape[0]
    tq = Lc
    nq = T // tq
    q_idx = lambda b, i: (jnp.where(i < nq, b * nq + i, B * nq + b), 0)
    return pl.pallas_call(
        functools.partial(_attn_kernel, nq_lat=nq),
        out_shape=jax.ShapeDtypeStruct((n, QP_W), bf16),
        grid=(B, nq + 1),
        in_specs=[
            pl.BlockSpec((tq, QP_W), q_idx),
            pl.BlockSpec((T, 2 * KVP_W), lambda b, i: (b, 0)),
            pl.BlockSpec((Lc, 2 * KVP_W), lambda b, i: (B * nq + b, 0)),
        ],
        out_specs=pl.BlockSpec((tq, QP_W), q_idx),
        compiler_params=_cparams(("parallel", "arbitrary")),
        name="attention",
    )(qp, kvp, kvp)


def _scan_kernel(u_ref, wh_ref, wl_ref, bias_ref, lam_ref, h0_ref, *rest, reverse, final):
    if final:
        hprev_ref, g_ref, o_ref, hend_ref, carry_ref = rest
    else:
        o_ref, hend_ref, carry_ref = rest
    t = pl.program_id(1)

    @pl.when(t == 0)
    def _():
        carry_ref[...] = h0_ref[...]

    u = u_ref[...]
    tt = u.shape[0]
    gates = _dot3(u, wh_ref[...], wl_ref[...]) + bias_ref[...]
    r = jax.nn.sigmoid(gates[:, :RG_W])
    ig = jax.nn.sigmoid(gates[:, RG_W:])
    lam = lam_ref[...]
    softplus = jnp.maximum(-lam, 0.0) + jnp.log1p(jnp.exp(-jnp.abs(lam)))
    log_a = -RG_C * r * softplus
    a = jnp.exp(log_a)
    b = jnp.sqrt(-jnp.tanh(log_a) * (a * a + 1.0)) * (ig * u)
    row = lax.broadcasted_iota(jnp.int32, a.shape, 0)
    s = 1
    while s < tt:
        if reverse:
            a_s, b_s = pltpu.roll(a, tt - s, axis=0), pltpu.roll(b, tt - s, axis=0)
            valid = row < tt - s
        else:
            a_s, b_s = pltpu.roll(a, s, axis=0), pltpu.roll(b, s, axis=0)
            valid = row >= s
        b = a * jnp.where(valid, b_s, 0.0) + b
        a = a * jnp.where(valid, a_s, 1.0)
        s *= 2
    h = a * carry_ref[...] + b
    last = h[0:1, :] if reverse else h[tt - 1:tt, :]
    carry_ref[...] = last
    hend_ref[...] = last
    if final:
        o_ref[...] = ((hprev_ref[...] + h) * jax.nn.gelu(g_ref[...], approximate=True)).astype(o_ref.dtype)
    else:
        o_ref[...] = h


def _rg_scan(u, n_seq, L, w_hi, w_lo, bias, lam, h0, reverse, hprev=None, gate=None, gate_row0=0):
    tt = min(SCAN_TILE, L)
    nt = L // tt
    final = hprev is not None
    tidx = (lambda t: nt - 1 - t) if reverse else (lambda t: t)
    row = lambda b, t: (b * nt + tidx(t), 0)
    one = lambda shape: pl.BlockSpec(shape, lambda b, t: (0, 0))
    in_specs = [pl.BlockSpec((tt, RG_W), row), one(w_hi.shape), one(w_lo.shape), one((1, 2 * RG_W)),
                one((1, RG_W)), pl.BlockSpec((None, 1, RG_W), lambda b, t: (b, 0, 0))]
    args = [u, w_hi, w_lo, bias, lam, h0]
    if final:
        in_specs += [pl.BlockSpec((tt, RG_W), row),
                     pl.BlockSpec((tt, RG_W), lambda b, t: (gate_row0 // tt + b * nt + tidx(t), 1))]
        args += [hprev, gate]
    return pl.pallas_call(
        functools.partial(_scan_kernel, reverse=reverse, final=final),
        out_shape=[jax.ShapeDtypeStruct((n_seq * L, RG_W), bf16 if final else f32),
                   jax.ShapeDtypeStruct((n_seq, 1, RG_W), f32)],
        grid=(n_seq, nt),
        in_specs=in_specs,
        out_specs=[pl.BlockSpec((tt, RG_W), row), pl.BlockSpec((None, 1, RG_W), lambda b, t: (b, 0, 0))],
        scratch_shapes=[pltpu.VMEM((1, RG_W), f32)],
        compiler_params=_cparams(("parallel", "arbitrary")),
        name="rg_scan",
    )(*args)


def _block_diag(w):
    bw = w.shape[-1]
    out = jnp.zeros((RG_W, RG_W), f32)
    for h in range(RG_BLOCKS):
        out = out.at[h * bw:(h + 1) * bw, h * bw:(h + 1) * bw].set(w[h])
    return out


def _rglru(rg, B, T, Lc, conv_w, conv_b, lam, w_a, b_a, w_x, b_x):
    bt = B * T
    u_l = _dwconv(rg, conv_w, conv_b, T, 0, B, 0, 1)
    u_c = _dwconv(rg, conv_w, conv_b, Lc, bt // Lc, B, 0, 1)
    zeros = jnp.zeros((B, 1, RG_W), f32)
    prev_l = prev_c = None
    for d, rev in enumerate((False, True)):
        w_hi, w_lo = _split(jnp.concatenate([_block_diag(w_a[d]), _block_diag(w_x[d])], axis=1))
        bias = jnp.concatenate([b_a[d], b_x[d]])[None, :]
        lam_d = lam[d][None, :]
        last = d == 1
        kw_c = dict(hprev=prev_c, gate=rg, gate_row0=bt) if last else {}
        kw_l = dict(hprev=prev_l, gate=rg, gate_row0=0) if last else {}
        prev_c, h_end = _rg_scan(u_c, B, Lc, w_hi, w_lo, bias, lam_d, zeros, rev, **kw_c)
        prev_l, _ = _rg_scan(u_l, B, T, w_hi, w_lo, bias, lam_d, h_end, rev, **kw_l)
    return jnp.concatenate([prev_l, prev_c], axis=0)


def _layer_norm(v, g, b):
    mu = jnp.mean(v, axis=-1, keepdims=True)
    d = v - mu
    var = jnp.mean(d * d, axis=-1, keepdims=True)
    return d * lax.rsqrt(var + LN_EPS) * g + b


def _outproj_kernel(hy_ref, att_ref, rg_ref, h_ref, g1_ref, sh2_ref, sc2_ref, lng_ref, lnb_ref,
                    why_ref, watt_ref, wrg_ref, wrh_ref, wrl_ref, br_ref,
                    oh_ref, om_ref, og_ref, oi_ref, oc_ref, *, alpha):
    y = (_dot(hy_ref[...].astype(bf16), why_ref[...]) + _dot(att_ref[...], watt_ref[...])
         + _dot(rg_ref[...], wrg_ref[...]))
    hn = _layer_norm(alpha * h_ref[...] + g1_ref[...] * y, lng_ref[...], lnb_ref[...])
    oh_ref[...] = hn
    m = hn * (1.0 + sc2_ref[...]) + sh2_ref[...]
    om_ref[...] = m.astype(bf16)
    scores = jax.nn.sigmoid(_dot3(m, wrh_ref[...], wrl_ref[...]))
    sel = scores + br_ref[...]
    lane = lax.broadcasted_iota(jnp.int32, sel.shape, 1)
    picked = jnp.zeros_like(scores)
    chosen = jnp.zeros(sel.shape, jnp.int32)
    taken = jnp.zeros_like(scores)
    for k in range(TOP_K):
        mx = jnp.max(sel, axis=-1, keepdims=True)
        first = jnp.min(jnp.where(sel == mx, lane, LANE), axis=-1, keepdims=True)
        hit = lane == first
        picked = jnp.where(hit, scores, picked)
        taken = jnp.where(hit, 1.0, taken)
        sel = jnp.where(hit, -jnp.inf, sel)
        chosen = jnp.where(lane == k, first, chosen)
    og_ref[...] = picked / jnp.sum(picked, axis=-1, keepdims=True) * ROUTED_SCALE
    oi_ref[...] = chosen
    for s in range(oc_ref.shape[0]):
        oc_ref[s] = jnp.sum(taken[s * DISPATCH_TILE:(s + 1) * DISPATCH_TILE], axis=0, keepdims=True)


def _outproj(hy, att, rg, h, mods3, ln_g, ln_b, w_hy, w_att, w_rg, wr_hi, wr_lo, b_router, seg, tm, alpha):
    n = h.shape[0]
    row = lambda w: pl.BlockSpec((tm, w), lambda i: (i, 0))
    one = lambda a: pl.BlockSpec(a.shape, lambda i: (0,) * a.ndim)
    return pl.pallas_call(
        functools.partial(_outproj_kernel, alpha=alpha),
        out_shape=[jax.ShapeDtypeStruct((n, D_MODEL), f32), jax.ShapeDtypeStruct((n, D_MODEL), bf16),
                   jax.ShapeDtypeStruct((n, LANE), f32), jax.ShapeDtypeStruct((n, LANE), jnp.int32),
                   jax.ShapeDtypeStruct((n // DISPATCH_TILE, 1, LANE), f32)],
        grid=(n // tm,),
        in_specs=[row(HY_W), row(QP_W), row(RG_W), row(D_MODEL),
                  _mod_spec(seg, 2), _mod_spec(seg, 3), _mod_spec(seg, 4),
                  one(ln_g), one(ln_b), one(w_hy), one(w_att), one(w_rg), one(wr_hi), one(wr_lo),
                  one(b_router)],
        out_specs=[row(D_MODEL), row(D_MODEL), row(LANE), row(LANE),
                   pl.BlockSpec((tm // DISPATCH_TILE, 1, LANE), lambda i: (i, 0, 0))],
        compiler_params=_cparams(("parallel",)),
        name="outproj_router",
    )(hy, att, rg, h, mods3, mods3, mods3, ln_g, ln_b, w_hy, w_att, w_rg, wr_hi, wr_lo, b_router)


def _xs_rows(tile):
    return -(-(TOP_K * tile + N_EXPERTS * (RUN_ALIGN - 1)) // 256) * 256


def _perm_matrix(idx, runstart, gates):
    t = idx.shape[0]
    lane = lax.broadcasted_iota(jnp.int32, (t, LANE), 1)
    hits = [lane == idx[:, k:k + 1] for k in range(TOP_K)]
    sel = jnp.zeros((t, LANE), f32)
    for hit in hits:
        sel = jnp.where(hit, 1.0, sel)
    earlier = (lax.broadcasted_iota(jnp.int32, (t, t), 1) < lax.broadcasted_iota(jnp.int32, (t, t), 0))
    rank = _dot(jnp.where(earlier, 1.0, 0.0).astype(bf16), sel.astype(bf16))
    posmat = rank + runstart
    slot = lax.broadcasted_iota(jnp.int32, (t, _xs_rows(t)), 1).astype(jnp.int16)
    acc = jnp.zeros(slot.shape, bf16)
    for hit in hits:
        pos = jnp.sum(jnp.where(hit, posmat, 0.0), axis=-1, keepdims=True).astype(jnp.int32).astype(jnp.int16)
        if gates is None:
            val = jnp.ones((t, 1), bf16)
        else:
            val = jnp.sum(jnp.where(hit, gates, 0.0), axis=-1, keepdims=True).astype(bf16)
        acc = jnp.where(slot == pos, val, acc)
    return acc


def _for_pieces(plan_ref, fn):
    for k, size in enumerate(PIECE_SIZES):
        def body(j, carry, k=k, size=size):
            fn(pl.multiple_of(plan_ref[2 * k, j], RUN_ALIGN), pl.multiple_of(plan_ref[2 * k + 1, j], RUN_ALIGN),
               size)
            return carry

        lax.fori_loop(0, plan_ref[PLAN_COUNT_ROW, k], body, 0)


def _dispatch_kernel(gap_ref, reg_ref, x_ref, idx_ref, rsf_ref, plan_ref, prev_ref, s_ref,
                     xs_ref, zero_ref, sem):
    i = pl.program_id(0)
    slot = i % 2

    @pl.when(i == 0)
    def _():
        zero_ref[...] = jnp.zeros(zero_ref.shape, zero_ref.dtype)

        def fill(action):
            def body(e, carry):
                @pl.when(reg_ref[e] > 0)
                def _():
                    dst = pl.multiple_of(gap_ref[e], RUN_ALIGN)
                    action(pltpu.make_async_copy(zero_ref, s_ref.at[pl.ds(dst, FFN_TILE)], sem))
                return carry
            lax.fori_loop(0, N_EXPERTS, body, 0)

        fill(lambda cp: cp.start())
        fill(lambda cp: cp.wait())

    p_t = _perm_matrix(idx_ref[...], rsf_ref[...], None)
    xs_ref[slot] = lax.dot_general(p_t, x_ref[...], (((0,), (0,)), ((), ())),
                                   preferred_element_type=f32).astype(bf16)

    def copy(buf, src, dst, sz):
        return pltpu.make_async_copy(xs_ref.at[buf].at[pl.ds(src, sz)], s_ref.at[pl.ds(dst, sz)], sem)

    @pl.when(i > 0)
    def _():
        _for_pieces(prev_ref, lambda s, d, sz: copy(1 - slot, s, d, sz).wait())

    _for_pieces(plan_ref, lambda s, d, sz: copy(slot, s, d, sz).start())

    @pl.when(i == pl.num_programs(0) - 1)
    def _():
        _for_pieces(plan_ref, lambda s, d, sz: copy(slot, s, d, sz).wait())


def _ffn_kernel(te_ref, nu_ref, x_ref, wg_ref, wu_ref, wd_ref, o_ref):
    @pl.when(pl.program_id(0) < nu_ref[0])
    def _():
        x = x_ref[...]
        hid = jax.nn.silu(_dot(x, wg_ref[...].astype(bf16))) * _dot(x, wu_ref[...].astype(bf16))
        o_ref[...] = _dot(hid.astype(bf16), wd_ref[...].astype(bf16)).astype(bf16)


def _combine_kernel(z_ref, plan_ref, idx_ref, gate_ref, rsf_ref, x_ref, sgu_ref, sd_ref,
                    h_ref, g2_ref, lng_ref, lnb_ref, o_ref, zbuf_ref, sem, *, alpha):
    i = pl.program_id(0)

    @pl.when(i == 0)
    def _():
        zbuf_ref[...] = jnp.zeros(zbuf_ref.shape, zbuf_ref.dtype)

    def copy(src, dst, sz):
        return pltpu.make_async_copy(z_ref.at[pl.ds(dst, sz)], zbuf_ref.at[pl.ds(src, sz)], sem)

    _for_pieces(plan_ref, lambda s, d, sz: copy(s, d, sz).start())
    hs = _dot(x_ref[...], sgu_ref[...])
    shared = _dot((jax.nn.silu(hs[:, :EXPERT_FF]) * hs[:, EXPERT_FF:]).astype(bf16), sd_ref[...])
    w_t = _perm_matrix(idx_ref[...], rsf_ref[...], gate_ref[...])
    _for_pieces(plan_ref, lambda s, d, sz: copy(s, d, sz).wait())
    y = _dot(w_t, zbuf_ref[...]) + shared
    o_ref[...] = _layer_norm(alpha * h_ref[...] + g2_ref[...] * y, lng_ref[...], lnb_ref[...])


def _route_plan(counts, n_ffn_tiles):
    c = counts[:, 0, :N_EXPERTS].astype(jnp.int32)
    lens = (c + RUN_ALIGN - 1) // RUN_ALIGN * RUN_ALIGN
    runstart = jnp.cumsum(lens, axis=1) - lens
    region = (jnp.sum(lens, axis=0) + FFN_TILE - 1) // FFN_TILE * FFN_TILE
    region_end = jnp.cumsum(region)
    dest = (region_end - region)[None, :] + jnp.cumsum(lens, axis=0) - lens
    n_used = region_end[-1] // FFN_TILE
    tile_row = jnp.minimum(jnp.arange(n_ffn_tiles, dtype=jnp.int32), n_used - 1) * FFN_TILE
    tile_expert = jnp.minimum(jnp.sum(tile_row[:, None] >= region_end[None, :], axis=1), N_EXPERTS - 1)
    runstart_f = jnp.zeros((c.shape[0], 1, LANE), f32).at[:, 0, :N_EXPERTS].set(runstart.astype(f32))

    def compact(mask, *vals):
        pos = jnp.cumsum(mask, axis=1) - mask
        hit = jnp.logical_and(mask[:, :, None] > 0, pos[:, :, None] == jnp.arange(LANE, dtype=jnp.int32))
        return [jnp.sum(jnp.where(hit, v[:, :, None], 0), axis=1) for v in vals], jnp.sum(mask, axis=1)

    n_chunks = lens // RUN_CHUNK
    j = jnp.arange(DISPATCH_TILE // RUN_CHUNK, dtype=jnp.int32) * RUN_CHUNK
    wide = lambda v: (v[:, :, None] + j).reshape(v.shape[0], -1)
    chunk_mask = (j < (n_chunks * RUN_CHUNK)[:, :, None]).astype(jnp.int32).reshape(lens.shape[0], -1)
    (s64, d64), c64 = compact(chunk_mask, wide(runstart), wide(dest))
    off = n_chunks * RUN_CHUNK
    (s32, d32), c32 = compact((lens >> 5) & 1, runstart + off, dest + off)
    off = off + (lens & 32)
    (s16, d16), c16 = compact((lens >> 4) & 1, runstart + off, dest + off)
    count_row = jnp.zeros_like(s64).at[:, 0].set(c64).at[:, 1].set(c32).at[:, 2].set(c16)
    pieces = jnp.stack([s64, d64, s32, d32, s16, d16, count_row, jnp.zeros_like(s64)], axis=1)
    return dict(pieces=pieces.astype(jnp.int32), gap=region_end - FFN_TILE, region=region,
                n_used=n_used.reshape(1), tile_expert=tile_expert.astype(jnp.int32), runstart_f=runstart_f)


def _moe(m, gates, idx, counts, layer, w_gate, w_up, w_down, sgu, sd, h, mods3, ln_g, ln_b, seg, alpha):
    n = h.shape[0]
    t = DISPATCH_TILE
    nt = n // t
    xs_rows = _xs_rows(t)
    rows_max = TOP_K * n + nt * N_EXPERTS * (RUN_ALIGN - 1) + N_EXPERTS * (FFN_TILE - RUN_ALIGN)
    n_ffn_tiles = -(-rows_max // FFN_TILE)
    plan = _route_plan(counts, n_ffn_tiles)
    tile = lambda w: pl.BlockSpec((t, w), lambda i, *_: (i, 0))
    rsf_spec = pl.BlockSpec((None, 1, LANE), lambda i, *_: (i, 0, 0))
    one = lambda a: pl.BlockSpec(a.shape, lambda i, *_: (0,) * a.ndim)
    any_spec = pl.BlockSpec(memory_space=pl.ANY)

    pieces = plan["pieces"]
    plan_spec = pl.BlockSpec((None, PLAN_ROWS, LANE), lambda i, *_: (i, 0, 0), memory_space=pltpu.SMEM)
    prev_spec = pl.BlockSpec((None, PLAN_ROWS, LANE), lambda i, *_: (jnp.maximum(i - 1, 0), 0, 0),
                             memory_space=pltpu.SMEM)
    sorted_x = pl.pallas_call(
        _dispatch_kernel,
        out_shape=jax.ShapeDtypeStruct((n_ffn_tiles * FFN_TILE, D_MODEL), bf16),
        grid_spec=pltpu.PrefetchScalarGridSpec(
            num_scalar_prefetch=2, grid=(nt,),
            in_specs=[tile(D_MODEL), tile(LANE), rsf_spec, plan_spec, prev_spec],
            out_specs=any_spec,
            scratch_shapes=[pltpu.VMEM((2, xs_rows, D_MODEL), bf16), pltpu.VMEM((FFN_TILE, D_MODEL), bf16),
                            pltpu.SemaphoreType.DMA]),
        compiler_params=_cparams(("arbitrary",)),
        name="moe_dispatch",
    )(plan["gap"], plan["region"], m, idx, plan["runstart_f"], pieces, pieces)

    used = lambda i, te, nu: jnp.minimum(i, nu[0] - 1)
    sorted_z = pl.pallas_call(
        _ffn_kernel,
        out_shape=jax.ShapeDtypeStruct((n_ffn_tiles * FFN_TILE, D_MODEL), bf16),
        grid_spec=pltpu.PrefetchScalarGridSpec(
            num_scalar_prefetch=2, grid=(n_ffn_tiles,),
            in_specs=[pl.BlockSpec((FFN_TILE, D_MODEL), lambda i, te, nu: (used(i, te, nu), 0)),
                      pl.BlockSpec((None, None, D_MODEL, EXPERT_FF), lambda i, te, nu: (layer, te[i], 0, 0)),
                      pl.BlockSpec((None, None, D_MODEL, EXPERT_FF), lambda i, te, nu: (layer, te[i], 0, 0)),
                      pl.BlockSpec((None, None, EXPERT_FF, D_MODEL), lambda i, te, nu: (layer, te[i], 0, 0))],
            out_specs=pl.BlockSpec((FFN_TILE, D_MODEL), lambda i, te, nu: (used(i, te, nu), 0))),
        compiler_params=_cparams(("arbitrary",)),
        name="moe_ffn",
    )(plan["tile_expert"], plan["n_used"], sorted_x, w_gate, w_up, w_down)

    return pl.pallas_call(
        functools.partial(_combine_kernel, alpha=alpha),
        out_shape=jax.ShapeDtypeStruct((n, D_MODEL), f32),
        grid=(nt,),
        in_specs=[any_spec, plan_spec, tile(LANE), tile(LANE), rsf_spec, tile(D_MODEL), one(sgu), one(sd),
                  tile(D_MODEL), pl.BlockSpec((None, 1, D_MODEL), lambda i, *_: (seg(i), 0, 5)),
                  one(ln_g), one(ln_b)],
        out_specs=tile(D_MODEL),
        scratch_shapes=[pltpu.VMEM((xs_rows, D_MODEL), bf16), pltpu.SemaphoreType.DMA],
        compiler_params=_cparams(("arbitrary",)),
        name="moe_combine",
    )(sorted_z, pieces, idx, gates, plan["runstart_f"], m, sgu, sd, h, mods3, ln_g, ln_b)


def _pad_heads_cols(w, n_heads):
    lead = w.shape[:-1]
    w = w.reshape(*lead, n_heads, HEAD_DIM)
    w = jnp.concatenate([w, jnp.zeros_like(w)], axis=-1)
    return w.reshape(*lead, n_heads * HEAD_PAD)


def _pad_in_proj(w_in):
    hy, q, k, v, rgx, rgg = (w_in[..., :HY_END], w_in[..., HY_END:Q_END], w_in[..., Q_END:K_END],
                             w_in[..., K_END:V_END], w_in[..., V_END:RGX_END], w_in[..., RGX_END:])
    return jnp.concatenate([hy, _pad_heads_cols(q, N_HEADS), _pad_heads_cols(k, N_KV_HEADS),
                            _pad_heads_cols(v, N_KV_HEADS), rgx, rgg], axis=-1).astype(bf16)


def kernel(x, c, ctx, c_ctx, w_mod, b_mod, ln1_g, ln1_b, ln2_g, ln2_b, w_in, w_out,
           hy_short_w, hy_short_b, hy_f_w1, hy_f_b1, hy_f_freq, hy_f_w2, hy_f_b2, hy_f_w3, hy_skip,
           q_norm, k_norm, rg_conv_w, rg_conv_b, rg_lambda, rg_w_a, rg_b_a, rg_w_x, rg_b_x,
           w_router, b_router, w_gate, w_up, w_down, ws_gate, ws_up, ws_down):
    B, T, D = x.shape
    Lc = ctx.shape[1]
    depth = w_mod.shape[0]
    bt, bc = B * T, B * Lc
    n_tok = bt + bc
    tm = TOKEN_TILE
    assert D == D_MODEL and T % tm == 0 and bc % tm == 0 and B % 2 == 0
    assert T % ATT_Q_TILE == 0 and Lc == ATT_Q_TILE and T % GRID_W == 0
    alpha = (2 * depth) ** 0.25
    tiles_per_seq = T // tm
    seg = lambda i: jnp.minimum(i // tiles_per_seq, B)
    pos_block = lambda i: jnp.where(i < B * tiles_per_seq, i % tiles_per_seq, tiles_per_seq)
    moe_tm = DISPATCH_TILE
    assert T % moe_tm == 0 and bc % moe_tm == 0
    moe_seg = lambda i: jnp.minimum(i // (T // moe_tm), B)
    out_tm = max(t for t in ROUTER_TILES if T % t == 0 and bc % t == 0)
    out_seg = lambda i: jnp.minimum(i // (T // out_tm), B)

    rows = -(-(B + 1) // 8) * 8
    cond = jnp.zeros((rows, D), f32).at[:B].set(c).at[B].set(c_ctx)
    mods = _modulation(cond, w_mod, b_mod)

    cos, sin = _rope_tables(T, tm)
    w_in_pad = _pad_in_proj(w_in)
    w_out_hy = w_out[:, :HY_W].astype(bf16)
    w_out_att = _pad_heads_cols(w_out[:, HY_W:HY_W + ATT_W].transpose(0, 2, 1), N_HEADS).transpose(0, 2, 1).astype(bf16)
    w_out_rg = w_out[:, HY_W + ATT_W:].astype(bf16)
    wr_pad = jnp.zeros((depth, D, LANE), f32).at[:, :, :N_EXPERTS].set(w_router)
    br_pad = jnp.full((depth, 1, LANE), -jnp.inf, f32).at[:, 0, :N_EXPERTS].set(b_router)
    sgu = jnp.concatenate([ws_gate, ws_up], axis=-1).astype(bf16)
    sd = ws_down.astype(bf16)

    h = jnp.concatenate([x.reshape(bt, D), ctx.reshape(bc, D)], axis=0)
    for l in range(depth):
        mods3 = mods[l].reshape(rows, 1, 6 * D)
        hy, q, kv, rg = _inproj(h, mods3, w_in_pad[l], seg, tm)

        xz_l = _dwconv(hy, hy_short_w[l], hy_short_b[l], T, 0, B, 0, 3)
        xz_c = _dwconv(hy, hy_short_w[l], hy_short_b[l], Lc, bt // Lc, B, 0, 3)
        fargs = (hy_f_w1[l], hy_f_b1[l], hy_f_freq[l], hy_f_w2[l], hy_f_b2[l], hy_f_w3[l])
        hy_l = _hyena_long(xz_l, _hyena_filters(T, *fargs), hy_skip[l], B, T)
        hy_c = _hyena_short_seq(xz_c, _hyena_filters(Lc, *fargs), hy_skip[l], B, Lc)
        hy_out = jnp.concatenate([hy_l, hy_c], axis=0)

        qp, kvp = _qkprep(q, kv, cos, sin, q_norm[l], k_norm[l], pos_block, tm)
        att = _attention(qp, kvp, B, T, Lc)

        rg_out = _rglru(rg, B, T, Lc, rg_conv_w[l], rg_conv_b[l], rg_lambda[l],
                        rg_w_a[l], rg_b_a[l], rg_w_x[l], rg_b_x[l])

        wr_hi, wr_lo = _split(wr_pad[l])
        h, m, gates, chosen, counts = _outproj(
            hy_out, att, rg_out, h, mods3, ln1_g[l][None, :], ln1_b[l][None, :],
            w_out_hy[l], w_out_att[l], w_out_rg[l], wr_hi, wr_lo, br_pad[l], out_seg, out_tm, alpha)
        h = _moe(m, gates, chosen, counts, l, w_gate, w_up, w_down, sgu[l], sd[l], h, mods3,
                 ln2_g[l][None, :], ln2_b[l][None, :], moe_seg, alpha)
    return h[:bt].reshape(B, T, D)
```

```python
import functools
import math

import numpy as np
import jax
import jax.numpy as jnp
from jax import lax
from jax.experimental import pallas as pl
from jax.experimental.pallas import tpu as pltpu

f32 = jnp.float32
bf16 = jnp.bfloat16

D_MODEL = 1024
GRID_W = 64
HY_W = 256
N_HEADS = 8
N_KV_HEADS = 2
N_GROUPS = N_HEADS // N_KV_HEADS
HEAD_DIM = 64
ATT_W = N_HEADS * HEAD_DIM
KV_W = N_KV_HEADS * HEAD_DIM
RG_W = 256
RG_BLOCKS = 4
HY_END = 3 * HY_W
Q_END = HY_END + ATT_W
K_END = Q_END + KV_W
V_END = K_END + KV_W
RGX_END = V_END + RG_W
PROJ_W = RGX_END + RG_W
HY_BANDS = 8
HY_EMB = 1 + 2 * HY_BANDS
HY_MAX_DECAY = math.log(1e-2) / 0.3
HY_MIN_DECAY = math.log(1e-2) / 1.5
ROPE_THETA = 10000.0
QK_EPS = 1e-6
RG_C = 8.0
N_EXPERTS = 64
TOP_K = 8
EXPERT_FF = 256
ROUTED_SCALE = 2.5
LN_EPS = 1e-6

LANE = 128
HEAD_PAD = LANE
QP_W = N_HEADS * HEAD_PAD
KVP_W = N_KV_HEADS * HEAD_PAD
PROJ_PAD_W = HY_END + QP_W + 2 * KVP_W + 2 * RG_W
VMEM_LIMIT = 52 * 1024 * 1024

FFT_N2 = 128
FFT_PASSES = 1
FILTER_PASSES = 3
TOKEN_TILE = 512
ATT_Q_TILE = 256
SCAN_TILE = 256
DISPATCH_TILE = 256
ROUTER_TILES = (1024, 512, 256)
RUN_ALIGN = 16
RUN_CHUNK = 64
PIECE_SIZES = (RUN_CHUNK, 32, 16)
PLAN_ROWS = 8
PLAN_COUNT_ROW = 6
FFN_TILE = 1024


def _cparams(sem):
    return pltpu.CompilerParams(dimension_semantics=sem, vmem_limit_bytes=VMEM_LIMIT)


def _dot(a, b):
    return jnp.dot(a, b, preferred_element_type=f32)


def _split(x):
    hi = x.astype(bf16)
    lo = (x - hi.astype(f32)).astype(bf16)
    return hi, lo


def _np_split(x):
    x = jnp.asarray(np.asarray(x, np.float32))
    return _split(x)


def _dot3(a, w_hi, w_lo):
    a_hi, a_lo = _split(a)
    return _dot(a_hi, w_hi) + _dot(a_lo, w_hi) + _dot(a_hi, w_lo)


def _stack_dft(mat):
    re_hi, re_lo = _np_split(mat.real)
    im_hi, im_lo = _np_split(mat.imag)
    return jnp.concatenate([re_hi, im_hi, re_lo, im_lo], axis=0)


def _apply_stack(stack, x, passes):
    m2 = stack.shape[0] // 2
    x_hi, x_lo = _split(x)
    if passes == 1:
        return _dot(stack[:m2], x_hi)
    r = _dot(stack, x_hi)
    out = r[:m2] + r[m2:]
    if passes >= 3:
        out = out + _dot(stack[:m2], x_lo)
    return out


def _mod_kernel(a_ref, w_ref, b_ref, o_ref):
    a = a_ref[...]
    a = a * jax.nn.sigmoid(a)
    w_hi, w_lo = _split(w_ref[...])
    o_ref[...] = _dot3(a, w_hi, w_lo) + b_ref[...]


def _modulation(cond, w_mod, b_mod):
    depth, d, n = w_mod.shape
    rows = cond.shape[0]
    tn = 512
    return pl.pallas_call(
        _mod_kernel,
        out_shape=jax.ShapeDtypeStruct((depth, rows, n), f32),
        grid=(depth, n // tn),
        in_specs=[
            pl.BlockSpec((rows, d), lambda l, j: (0, 0)),
            pl.BlockSpec((None, d, tn), lambda l, j: (l, 0, j)),
            pl.BlockSpec((None, 1, tn), lambda l, j: (l, 0, j)),
        ],
        out_specs=pl.BlockSpec((None, rows, tn), lambda l, j: (l, 0, j)),
        compiler_params=_cparams(("parallel", "parallel")),
        name="modulation",
    )(cond, w_mod, b_mod.reshape(depth, 1, n))


def _inproj_kernel(h_ref, sh_ref, sc_ref, w_ref, ohy_ref, oq_ref, okv_ref, org_ref):
    a = h_ref[...] * (1.0 + sc_ref[...]) + sh_ref[...]
    p = _dot(a.astype(bf16), w_ref[...])
    c0, c1, c2 = HY_END, HY_END + QP_W, HY_END + QP_W + 2 * KVP_W
    ohy_ref[...] = p[:, :c0]
    oq_ref[...] = p[:, c0:c1]
    okv_ref[...] = p[:, c1:c2]
    org_ref[...] = p[:, c2:]


def _mod_spec(seg, j):
    return pl.BlockSpec((None, 1, D_MODEL), lambda i: (seg(i), 0, j))


def _inproj(h, mods3, w_pad, seg, tm):
    n = h.shape[0]
    widths = (HY_END, QP_W, 2 * KVP_W, 2 * RG_W)
    return pl.pallas_call(
        _inproj_kernel,
        out_shape=[jax.ShapeDtypeStruct((n, w), f32) for w in widths],
        grid=(n // tm,),
        in_specs=[
            pl.BlockSpec((tm, D_MODEL), lambda i: (i, 0)),
            _mod_spec(seg, 0),
            _mod_spec(seg, 1),
            pl.BlockSpec((D_MODEL, PROJ_PAD_W), lambda i: (0, 0)),
        ],
        out_specs=[pl.BlockSpec((tm, w), lambda i: (i, 0)) for w in widths],
        compiler_params=_cparams(("parallel",)),
        name="inproj",
    )(h, mods3, mods3, w_pad)


def _dwconv_kernel(x_ref, w_ref, b_ref, o_ref, *, width, left):
    x = x_ref[...]
    n = x.shape[0]
    row = lax.broadcasted_iota(jnp.int32, x.shape, 0)
    acc = jnp.zeros_like(x) + b_ref[...]
    for j in range(width):
        off = j - left
        if off == 0:
            xs = x
        else:
            xs = pltpu.roll(x, (-off) % n, axis=0)
            valid = jnp.logical_and(row + off >= 0, row + off < n)
            xs = jnp.where(valid, xs, 0.0)
        acc = acc + xs * w_ref[j:j + 1, :]
    o_ref[...] = acc


def _dwconv(x, w, b, seq_len, row_block0, n_seq, col_block0, n_col_blocks):
    width = w.shape[0]
    wp = jnp.zeros((8, w.shape[1]), f32).at[:width].set(w)
    ct = 256
    return pl.pallas_call(
        functools.partial(_dwconv_kernel, width=width, left=(width - 1) // 2),
        out_shape=jax.ShapeDtypeStruct((n_seq * seq_len, n_col_blocks * ct), f32),
        grid=(n_seq, n_col_blocks),
        in_specs=[
            pl.BlockSpec((seq_len, ct), lambda s, c: (row_block0 + s, col_block0 + c)),
            pl.BlockSpec((8, ct), lambda s, c: (0, c)),
            pl.BlockSpec((1, ct), lambda s, c: (0, c)),
        ],
        out_specs=pl.BlockSpec((seq_len, ct), lambda s, c: (s, c)),
        compiler_params=_cparams(("parallel", "parallel")),
        name="dwconv",
    )(x, wp, b.reshape(1, -1))


def _filter_kernel(z_ref, w1h_ref, w1l_ref, b1_ref, fr_ref, w2h_ref, w2l_ref, b2_ref,
                   w3h_ref, w3l_ref, win_ref, o_ref):
    fr = fr_ref[...]
    f = jnp.sin(fr * (_dot3(z_ref[...], w1h_ref[...], w1l_ref[...]) + b1_ref[...]))
    f = jnp.sin(fr * (_dot3(f, w2h_ref[...], w2l_ref[...]) + b2_ref[...]))
    f = _dot3(f, w3h_ref[...], w3l_ref[...])
    win = win_ref[...]
    o_ref[...] = f * jnp.concatenate([win] * 2, axis=1)


def _filter_features(L):
    t = np.linspace(0.0, 1.0, L, dtype=np.float32)[:, None].astype(np.float64)
    w = np.float32(2.0 * math.pi) * np.arange(L, dtype=np.float32)[:, None] / np.float32(L)
    bands = np.linspace(1e-4, HY_BANDS - 1, HY_BANDS, dtype=np.float32)
    arg = (bands * w).astype(np.float64)
    z = np.concatenate([t, np.cos(arg), -np.sin(arg)], axis=-1)
    deltas = np.abs(np.linspace(HY_MIN_DECAY, HY_MAX_DECAY, HY_W, dtype=np.float32)).astype(np.float64)
    win = np.exp(-t * deltas)
    zp = np.zeros((2 * L, LANE), np.float32)
    zp[:L, :HY_EMB] = z
    zp[L + 1:, :HY_EMB] = z[:0:-1]
    win2 = np.zeros((2 * L, HY_W), np.float32)
    win2[:L] = win
    win2[L + 1:] = win[:0:-1]
    return jnp.asarray(zp), jnp.asarray(win2)


def _pad2(w, rows, cols):
    return jnp.zeros((rows, cols), f32).at[:w.shape[0], :w.shape[1]].set(w)


def _hyena_filters(L, w1, b1, freq, w2, b2, w3):
    zp, win = _filter_features(L)
    w1h, w1l = _split(_pad2(w1, LANE, LANE))
    w2h, w2l = _split(_pad2(w2, LANE, LANE))
    w3h, w3l = _split(_pad2(w3, LANE, 4 * HY_W))
    b1p = _pad2(b1[None, :], 1, LANE)
    b2p = _pad2(b2[None, :], 1, LANE)
    frp = _pad2(freq[None, :], 1, LANE)
    tl = min(L, 512)
    half = L // tl
    full = lambda shape: pl.BlockSpec(shape, lambda i: (0, 0))
    w3_spec = pl.BlockSpec((LANE, 2 * HY_W), lambda i: (0, i // half))
    return pl.pallas_call(
        _filter_kernel,
        out_shape=jax.ShapeDtypeStruct((2 * L, 2 * HY_W), f32),
        grid=(2 * half,),
        in_specs=[
            pl.BlockSpec((tl, LANE), lambda i: (i, 0)),
            full((LANE, LANE)), full((LANE, LANE)), full((1, LANE)), full((1, LANE)),
            full((LANE, LANE)), full((LANE, LANE)), full((1, LANE)),
            w3_spec, w3_spec,
            pl.BlockSpec((tl, HY_W), lambda i: (i, 0)),
        ],
        out_specs=pl.BlockSpec((tl, 2 * HY_W), lambda i: (i, 0)),
        compiler_params=_cparams(("parallel",)),
        name="hyena_filter",
    )(zp, w1h, w1l, b1p, frp, w2h, w2l, b2p, w3h, w3l, win)


def _dft_consts(n):
    n1 = n // FFT_N2
    k = np.arange(n1)
    f_n1 = np.exp(-2j * np.pi * np.outer(k, k) / n1)
    k2 = np.arange(FFT_N2)
    f_n2 = np.exp(-2j * np.pi * np.outer(k2, k2) / FFT_N2)
    tw = np.exp(-2j * np.pi * np.outer(k, k2) / n).reshape(n, 1)
    tw_re = jnp.asarray(np.broadcast_to(tw.real, (n, LANE)).astype(np.float32))
    tw_im = jnp.asarray(np.broadcast_to(tw.imag, (n, LANE)).astype(np.float32))
    return n1, f_n1, f_n2, tw_re, tw_im


def _fa_kernel(u_ref, fs_ref, are_ref, aim_ref, *, grp, passes, real_only):
    fs = fs_ref[...]
    n1 = fs.shape[0] // 4
    jb, c = u_ref.shape[2], u_ref.shape[3]
    for g in range(jb // grp):
        def gather(bi):
            return jnp.concatenate([u_ref[bi, :, g * grp + jj, :] for jj in range(grp)], axis=1)
        p = _apply_stack(fs, gather(0), passes)
        if real_only:
            re, im = p[:n1], p[n1:]
        else:
            q = _apply_stack(fs, gather(1), passes)
            re, im = p[:n1] - q[n1:], p[n1:] + q[:n1]
        for jj in range(grp):
            are_ref[:, g * grp + jj, :] = re[:, jj * c:(jj + 1) * c]
            aim_ref[:, g * grp + jj, :] = im[:, jj * c:(jj + 1) * c]


def _fft_step_a(u4, fs, col_block, c, real_only, passes):
    s, n1_in = u4.shape[0], u4.shape[1]
    n1 = fs.shape[0] // 4
    per = 1 if real_only else 2
    jb = 32
    grp = max(1, 1024 // c)
    shape = jax.ShapeDtypeStruct((s // per, n1, FFT_N2, c), f32)
    return pl.pallas_call(
        functools.partial(_fa_kernel, grp=grp, passes=passes, real_only=real_only),
        out_shape=[shape, shape],
        grid=(s // per, FFT_N2 // jb),
        in_specs=[
            pl.BlockSpec((per, n1_in, jb, c), lambda p, j: (p, 0, j, col_block)),
            pl.BlockSpec(fs.shape, lambda p, j: (0, 0)),
        ],
        out_specs=[pl.BlockSpec((None, n1, jb, c), lambda p, j: (p, 0, j, 0))] * 2,
        compiler_params=_cparams(("parallel", "parallel")),
        name="fft_step_a",
    )(u4, fs)


def _mid_kernel(are_ref, aim_ref, twr_ref, twi_ref, f2_ref, *rest, kb, passes, fwd_only, inv_n):
    if fwd_only:
        ore_ref, oim_ref = rest
    else:
        kr_ref, ki_ref, ore_ref, oim_ref = rest
    f2 = f2_ref[...]
    c = are_ref.shape[1]
    for kk in range(kb):
        rows = pl.ds(kk * FFT_N2, FFT_N2)
        ar, ai = are_ref[rows, :], aim_ref[rows, :]
        tr = jnp.concatenate([twr_ref[rows, :]] * (c // LANE), axis=1)
        ti = jnp.concatenate([twi_ref[rows, :]] * (c // LANE), axis=1)
        xr = ar * tr - ai * ti
        xi = ar * ti + ai * tr
        p = _apply_stack(f2, xr, passes)
        q = _apply_stack(f2, xi, passes)
        sr = p[:FFT_N2] - q[FFT_N2:]
        si = p[FFT_N2:] + q[:FFT_N2]
        if fwd_only:
            ore_ref[rows, :] = sr
            oim_ref[rows, :] = si
        else:
            kr, ki = kr_ref[rows, :], ki_ref[rows, :]
            yr = sr * kr - si * ki
            yi = sr * ki + si * kr
            p2 = _apply_stack(f2, yr, passes)
            q2 = _apply_stack(f2, yi, passes)
            br = p2[:FFT_N2] + q2[FFT_N2:]
            bi = q2[:FFT_N2] - p2[FFT_N2:]
            ore_ref[rows, :] = (br * tr + bi * ti) * inv_n
            oim_ref[rows, :] = (bi * tr - br * ti) * inv_n


def _fft_mid(a_re, a_im, tw_re, tw_im, f2s, kf=None, order=0, passes=FFT_PASSES):
    p, n, c = a_re.shape
    kb = min(4, n // FFT_N2)
    rb = kb * FFT_N2
    fwd_only = kf is None
    blk = pl.BlockSpec((None, rb, c), lambda i, k: (i, k, 0))
    in_specs = [blk, blk,
                pl.BlockSpec((rb, LANE), lambda i, k: (k, 0)),
                pl.BlockSpec((rb, LANE), lambda i, k: (k, 0)),
                pl.BlockSpec(f2s.shape, lambda i, k: (0, 0))]
    args = [a_re, a_im, tw_re, tw_im, f2s]
    if not fwd_only:
        in_specs += [pl.BlockSpec((rb, c), lambda i, k: (k, order))] * 2
        args += list(kf)
    shape = jax.ShapeDtypeStruct((p, n, c), f32)
    return pl.pallas_call(
        functools.partial(_mid_kernel, kb=kb, passes=passes, fwd_only=fwd_only, inv_n=1.0 / n),
        out_shape=[shape, shape],
        grid=(p, n // rb),
        in_specs=in_specs,
        out_specs=[blk, blk],
        compiler_params=_cparams(("parallel", "parallel")),
        name="fft_mid",
    )(*args)


def _fai_kernel(bre_ref, bim_ref, gs_ref, u_ref, gate_ref, skip_ref, o_ref, *, grp, passes):
    gs = gs_ref[...]
    n1h = gs.shape[0] // 4
    jb, c = u_ref.shape[2], u_ref.shape[3]
    skip = skip_ref[...]
    for g in range(jb // grp):
        def gather(ref):
            return jnp.concatenate([ref[:, g * grp + jj, :] for jj in range(grp)], axis=1)
        p = _apply_stack(gs, gather(bre_ref), passes)
        q = _apply_stack(gs, gather(bim_ref), passes)
        ya = p[:n1h] - q[n1h:]
        yb = p[n1h:] + q[:n1h]
        for jj in range(grp):
            j = g * grp + jj
            for bi, y in ((0, ya), (1, yb)):
                u = u_ref[bi, :, j, :]
                o_ref[bi, :, j, :] = gate_ref[bi, :, j, :] * (y[:, jj * c:(jj + 1) * c] + u * skip)


def _fft_step_a_inv(b_re, b_im, gs, u4, u_col, gate4, gate_col, skip, passes):
    p, n1, _, c = b_re.shape
    n1h = n1 // 2
    jb = 32
    grp = max(1, 1024 // c)
    bspec = pl.BlockSpec((None, n1, jb, c), lambda i, j: (i, 0, j, 0))
    return pl.pallas_call(
        functools.partial(_fai_kernel, grp=grp, passes=passes),
        out_shape=jax.ShapeDtypeStruct((2 * p, n1h, FFT_N2, c), f32),
        grid=(p, FFT_N2 // jb),
        in_specs=[
            bspec, bspec,
            pl.BlockSpec(gs.shape, lambda i, j: (0, 0)),
            pl.BlockSpec((2, n1h, jb, c), lambda i, j: (i, 0, j, u_col)),
            pl.BlockSpec((2, n1h, jb, c), lambda i, j: (i, 0, j, gate_col)),
            pl.BlockSpec((1, c), lambda i, j: (0, 0)),
        ],
        out_specs=pl.BlockSpec((2, n1h, jb, c), lambda i, j: (i, 0, j, 0)),
        compiler_params=_cparams(("parallel", "parallel")),
        name="fft_step_a_inv",
    )(b_re, b_im, gs, u4, gate4, skip)


def _hyena_long(xz, filt, skip, n_seq, L, passes=FFT_PASSES):
    n = 2 * L
    n1, f_n1, f_n2, tw_re, tw_im = _dft_consts(n)
    n1h = n1 // 2
    c = HY_W
    fs_full = _stack_dft(f_n1)
    fs_half = _stack_dft(f_n1[:, :n1h])
    gs = _stack_dft(np.conj(f_n1)[:n1h, :])
    f2s = _stack_dft(f_n2)
    kc4 = filt.reshape(1, n1, FFT_N2, 2 * c)
    k_re, k_im = _fft_step_a(kc4, fs_full, 0, 2 * c, True, FILTER_PASSES)
    kf = _fft_mid(k_re.reshape(1, n, 2 * c), k_im.reshape(1, n, 2 * c), tw_re, tw_im, f2s,
                  passes=FILTER_PASSES)
    kf = (kf[0].reshape(n, 2 * c), kf[1].reshape(n, 2 * c))
    xz4 = xz.reshape(n_seq, n1h, FFT_N2, 3 * c)
    z4, z_col = xz4, 2
    for order in range(2):
        a_re, a_im = _fft_step_a(z4, fs_half, z_col, c, False, passes)
        p = n_seq // 2
        b_re, b_im = _fft_mid(a_re.reshape(p, n, c), a_im.reshape(p, n, c), tw_re, tw_im, f2s,
                              kf=kf, order=order, passes=passes)
        z4 = _fft_step_a_inv(b_re.reshape(p, n1, FFT_N2, c), b_im.reshape(p, n1, FFT_N2, c), gs,
                             z4, z_col, xz4, order, skip[order:order + 1], passes)
        z_col = 0
    return z4.reshape(n_seq * L, c)


def _dense_spec_kernel(k_ref, fs_ref, ore_ref, oim_ref, *, passes):
    n = k_ref.shape[0]
    p = _apply_stack(fs_ref[...], k_ref[...], passes)
    ore_ref[...] = p[:n]
    oim_ref[...] = p[n:]


def _dense_conv_kernel(xz_ref, fs_ref, gs_ref, kr_ref, ki_ref, skip_ref, o_ref, *, passes):
    L = xz_ref.shape[1]
    n = 2 * L
    c = HY_W
    fs, gs = fs_ref[...], gs_ref[...]
    za, zb = xz_ref[0, :, 2 * c:], xz_ref[1, :, 2 * c:]
    for order in range(2):
        p = _apply_stack(fs, za, passes)
        q = _apply_stack(fs, zb, passes)
        sr, si = p[:n] - q[n:], p[n:] + q[:n]
        kr, ki = kr_ref[:, order * c:(order + 1) * c], ki_ref[:, order * c:(order + 1) * c]
        yr, yi = sr * kr - si * ki, sr * ki + si * kr
        p2 = _apply_stack(gs, yr, passes)
        q2 = _apply_stack(gs, yi, passes)
        ya = (p2[:L] - q2[L:]) * (1.0 / n)
        yb = (p2[L:] + q2[:L]) * (1.0 / n)
        skip = skip_ref[order:order + 1, :]
        za = xz_ref[0, :, order * c:(order + 1) * c] * (ya + za * skip)
        zb = xz_ref[1, :, order * c:(order + 1) * c] * (yb + zb * skip)
    o_ref[0] = za
    o_ref[1] = zb


def _hyena_short_seq(xz, filt, skip, n_seq, L, passes=FFT_PASSES):
    n = 2 * L
    c = HY_W
    k = np.arange(n)
    f_n = np.exp(-2j * np.pi * np.outer(k, k) / n)
    fs_full = _stack_dft(f_n)
    fs_half = _stack_dft(f_n[:, :L])
    gs = _stack_dft(np.conj(f_n)[:L, :])
    kc = filt
    shape = jax.ShapeDtypeStruct((n, 2 * c), f32)
    k_re, k_im = pl.pallas_call(
        functools.partial(_dense_spec_kernel, passes=FILTER_PASSES),
        out_shape=[shape, shape],
        compiler_params=_cparams(None),
        name="dense_filter_spectrum",
    )(kc, fs_full)
    skip_p = jnp.zeros((8, c), f32).at[:2].set(skip)
    full = lambda a: pl.BlockSpec(a.shape, lambda i: (0,) * a.ndim)
    out = pl.pallas_call(
        functools.partial(_dense_conv_kernel, passes=passes),
        out_shape=jax.ShapeDtypeStruct((n_seq, L, c), f32),
        grid=(n_seq // 2,),
        in_specs=[pl.BlockSpec((2, L, 3 * c), lambda i: (i, 0, 0)),
                  full(fs_half), full(gs), full(k_re), full(k_im), full(skip_p)],
        out_specs=pl.BlockSpec((2, L, c), lambda i: (i, 0, 0)),
        compiler_params=_cparams(("parallel",)),
        name="dense_long_conv",
    )(xz.reshape(n_seq, L, 3 * c), fs_half, gs, k_re, k_im, skip_p)
    return out.reshape(n_seq * L, c)


def _rope_tables(T, tm):
    n_rows = T // GRID_W
    row = np.repeat(np.arange(n_rows, dtype=np.float32), GRID_W)
    col = np.tile(np.arange(GRID_W, dtype=np.float32), n_rows)
    axis_dim = HEAD_DIM // 2
    inv_freq = np.float32(ROPE_THETA) ** (-np.arange(0, axis_dim, 2, dtype=np.float32) / np.float32(axis_dim))
    ang = np.concatenate([row[:, None] * inv_freq, col[:, None] * inv_freq], axis=-1).astype(np.float64)
    cos = np.zeros((T + tm, HEAD_PAD), np.float32)
    sin = np.zeros((T + tm, HEAD_PAD), np.float32)
    cos[:T, 0:HEAD_DIM:2] = np.cos(ang)
    cos[:T, 1:HEAD_DIM:2] = np.cos(ang)
    sin[:T, 0:HEAD_DIM:2] = -np.sin(ang)
    sin[:T, 1:HEAD_DIM:2] = np.sin(ang)
    cos[T:, :HEAD_DIM] = 1.0
    return jnp.asarray(cos), jnp.asarray(sin)


def _qkprep_kernel(q_ref, kv_ref, cos_ref, sin_ref, qg_ref, kg_ref, avg_ref, oq_ref, okv_ref):
    cos, sin = cos_ref[...], sin_ref[...]
    avg = avg_ref[...]
    lane = lax.broadcasted_iota(jnp.int32, cos.shape, 1)
    even = (lane % 2) == 0

    def norm_rope(x, gain, scale):
        sq_hi, sq_lo = _split(x * x)
        ms = _dot(sq_hi, avg) + _dot(sq_lo, avg)
        xn = x * lax.rsqrt(ms + QK_EPS) * gain
        swapped = jnp.where(even, pltpu.roll(xn, LANE - 1, axis=1), pltpu.roll(xn, 1, axis=1))
        return ((xn * cos + swapped * sin) * scale).astype(bf16)

    for h in range(N_HEADS):
        sl = slice(h * HEAD_PAD, (h + 1) * HEAD_PAD)
        oq_ref[:, sl] = norm_rope(q_ref[:, sl], qg_ref[...], HEAD_DIM ** -0.5)
    for g in range(N_KV_HEADS):
        sl = slice(g * HEAD_PAD, (g + 1) * HEAD_PAD)
        okv_ref[:, sl] = norm_rope(kv_ref[:, sl], kg_ref[...], 1.0)
    v = kv_ref[:, KVP_W:]
    v_lane = lax.broadcasted_iota(jnp.int32, v.shape, 1) % HEAD_PAD
    okv_ref[:, KVP_W:] = jnp.where(v_lane == HEAD_DIM, 1.0, v).astype(bf16)


def _qkprep(q, kv, cos, sin, q_gain, k_gain, pos_block, tm):
    n = q.shape[0]
    pad = lambda g: jnp.zeros((1, HEAD_PAD), f32).at[0, :HEAD_DIM].set(g)
    avg = jnp.full((HEAD_PAD, HEAD_PAD), 1.0 / HEAD_DIM, bf16)
    one = lambda shape: pl.BlockSpec(shape, lambda i: (0, 0))
    return pl.pallas_call(
        _qkprep_kernel,
        out_shape=[jax.ShapeDtypeStruct((n, QP_W), bf16), jax.ShapeDtypeStruct((n, 2 * KVP_W), bf16)],
        grid=(n // tm,),
        in_specs=[
            pl.BlockSpec((tm, QP_W), lambda i: (i, 0)),
            pl.BlockSpec((tm, 2 * KVP_W), lambda i: (i, 0)),
            pl.BlockSpec((tm, HEAD_PAD), lambda i: (pos_block(i), 0)),
            pl.BlockSpec((tm, HEAD_PAD), lambda i: (pos_block(i), 0)),
            one((1, HEAD_PAD)), one((1, HEAD_PAD)), one((HEAD_PAD, HEAD_PAD)),
        ],
        out_specs=[pl.BlockSpec((tm, QP_W), lambda i: (i, 0)),
                   pl.BlockSpec((tm, 2 * KVP_W), lambda i: (i, 0))],
        compiler_params=_cparams(("parallel",)),
        name="qk_prep",
    )(q, kv, cos, sin, pad(q_gain), pad(k_gain), avg)


def _attn_kernel(q_ref, kvl_ref, kvc_ref, o_ref, *, nq_lat):
    i = pl.program_id(1)
    dims = (((1,), (1,)), ((), ()))

    def scores(h, use_lat):
        g = h // N_GROUPS
        ks = slice(g * HEAD_PAD, (g + 1) * HEAD_PAD)
        q = q_ref[:, h * HEAD_PAD:(h + 1) * HEAD_PAD]
        sc = lax.dot_general(q, kvc_ref[:, ks], dims, preferred_element_type=f32).astype(bf16)
        sl = None
        if use_lat:
            sl = lax.dot_general(q, kvl_ref[:, ks], dims, preferred_element_type=f32).astype(bf16)
        return sl, sc

    def heads(use_lat):
        nxt = scores(0, use_lat)
        for h in range(N_HEADS):
            sl, sc = nxt
            if h + 1 < N_HEADS:
                nxt = scores(h + 1, use_lat)
            g = h // N_GROUPS
            vs = slice(KVP_W + g * HEAD_PAD, KVP_W + (g + 1) * HEAD_PAD)
            m = jnp.max(sc, axis=-1, keepdims=True)
            if use_lat:
                m = jnp.maximum(m, jnp.max(sl, axis=-1, keepdims=True))
                acc = _dot(jnp.exp(sl - m), kvl_ref[:, vs]) + _dot(jnp.exp(sc - m), kvc_ref[:, vs])
            else:
                acc = _dot(jnp.exp(sc - m), kvc_ref[:, vs])
            o_ref[:, h * HEAD_PAD:(h + 1) * HEAD_PAD] = (acc / acc[:, HEAD_DIM:HEAD_DIM + 1]).astype(bf16)

    @pl.when(i < nq_lat)
    def _():
        heads(True)

    @pl.when(i == nq_lat)
    def _():
        heads(False)


def _attention(qp, kvp, B, T, Lc):
    n = qp.shape[0]
    tq = Lc
    nq = T // tq
    q_idx = lambda b, i: (jnp.where(i < nq, b * nq + i, B * nq + b), 0)
    return pl.pallas_call(
        functools.partial(_attn_kernel, nq_lat=nq),
        out_shape=jax.ShapeDtypeStruct((n, QP_W), bf16),
        grid=(B, nq + 1),
        in_specs=[
            pl.BlockSpec((tq, QP_W), q_idx),
            pl.BlockSpec((T, 2 * KVP_W), lambda b, i: (b, 0)),
            pl.BlockSpec((Lc, 2 * KVP_W), lambda b, i: (B * nq + b, 0)),
        ],
        out_specs=pl.BlockSpec((tq, QP_W), q_idx),
        compiler_params=_cparams(("parallel", "arbitrary")),
        name="attention",
    )(qp, kvp, kvp)


def _scan_kernel(u_ref, wh_ref, wl_ref, bias_ref, lam_ref, h0_ref, *rest, reverse, final):
    if final:
        hprev_ref, g_ref, o_ref, hend_ref, carry_ref = rest
    else:
        o_ref, hend_ref, carry_ref = rest
    t = pl.program_id(1)

    @pl.when(t == 0)
    def _():
        carry_ref[...] = h0_ref[...]

    u = u_ref[...]
    tt = u.shape[0]
    gates = _dot3(u, wh_ref[...], wl_ref[...]) + bias_ref[...]
    r = jax.nn.sigmoid(gates[:, :RG_W])
    ig = jax.nn.sigmoid(gates[:, RG_W:])
    lam = lam_ref[...]
    softplus = jnp.maximum(-lam, 0.0) + jnp.log1p(jnp.exp(-jnp.abs(lam)))
    log_a = -RG_C * r * softplus
    a = jnp.exp(log_a)
    b = jnp.sqrt(-jnp.tanh(log_a) * (a * a + 1.0)) * (ig * u)
    row = lax.broadcasted_iota(jnp.int32, a.shape, 0)
    s = 1
    while s < tt:
        if reverse:
            a_s, b_s = pltpu.roll(a, tt - s, axis=0), pltpu.roll(b, tt - s, axis=0)
            valid = row < tt - s
        else:
            a_s, b_s = pltpu.roll(a, s, axis=0), pltpu.roll(b, s, axis=0)
            valid = row >= s
        b = a * jnp.where(valid, b_s, 0.0) + b
        a = a * jnp.where(valid, a_s, 1.0)
        s *= 2
    h = a * carry_ref[...] + b
    last = h[0:1, :] if reverse else h[tt - 1:tt, :]
    carry_ref[...] = last
    hend_ref[...] = last
    if final:
        o_ref[...] = ((hprev_ref[...] + h) * jax.nn.gelu(g_ref[...], approximate=True)).astype(o_ref.dtype)
    else:
        o_ref[...] = h


def _rg_scan(u, n_seq, L, w_hi, w_lo, bias, lam, h0, reverse, hprev=None, gate=None, gate_row0=0):
    tt = min(SCAN_TILE, L)
    nt = L // tt
    final = hprev is not None
    tidx = (lambda t: nt - 1 - t) if reverse else (lambda t: t)
    row = lambda b, t: (b * nt + tidx(t), 0)
    one = lambda shape: pl.BlockSpec(shape, lambda b, t: (0, 0))
    in_specs = [pl.BlockSpec((tt, RG_W), row), one(w_hi.shape), one(w_lo.shape), one((1, 2 * RG_W)),
                one((1, RG_W)), pl.BlockSpec((None, 1, RG_W), lambda b, t: (b, 0, 0))]
    args = [u, w_hi, w_lo, bias, lam, h0]
    if final:
        in_specs += [pl.BlockSpec((tt, RG_W), row),
                     pl.BlockSpec((tt, RG_W), lambda b, t: (gate_row0 // tt + b * nt + tidx(t), 1))]
        args += [hprev, gate]
    return pl.pallas_call(
        functools.partial(_scan_kernel, reverse=reverse, final=final),
        out_shape=[jax.ShapeDtypeStruct((n_seq * L, RG_W), bf16 if final else f32),
                   jax.ShapeDtypeStruct((n_seq, 1, RG_W), f32)],
        grid=(n_seq, nt),
        in_specs=in_specs,
        out_specs=[pl.BlockSpec((tt, RG_W), row), pl.BlockSpec((None, 1, RG_W), lambda b, t: (b, 0, 0))],
        scratch_shapes=[pltpu.VMEM((1, RG_W), f32)],
        compiler_params=_cparams(("parallel", "arbitrary")),
        name="rg_scan",
    )(*args)


def _block_diag(w):
    bw = w.shape[-1]
    out = jnp.zeros((RG_W, RG_W), f32)
    for h in range(RG_BLOCKS):
        out = out.at[h * bw:(h + 1) * bw, h * bw:(h + 1) * bw].set(w[h])
    return out


def _rglru(rg, B, T, Lc, conv_w, conv_b, lam, w_a, b_a, w_x, b_x):
    bt = B * T
    u_l = _dwconv(rg, conv_w, conv_b, T, 0, B, 0, 1)
    u_c = _dwconv(rg, conv_w, conv_b, Lc, bt // Lc, B, 0, 1)
    zeros = jnp.zeros((B, 1, RG_W), f32)
    prev_l = prev_c = None
    for d, rev in enumerate((False, True)):
        w_hi, w_lo = _split(jnp.concatenate([_block_diag(w_a[d]), _block_diag(w_x[d])], axis=1))
        bias = jnp.concatenate([b_a[d], b_x[d]])[None, :]
        lam_d = lam[d][None, :]
        last = d == 1
        kw_c = dict(hprev=prev_c, gate=rg, gate_row0=bt) if last else {}
        kw_l = dict(hprev=prev_l, gate=rg, gate_row0=0) if last else {}
        prev_c, h_end = _rg_scan(u_c, B, Lc, w_hi, w_lo, bias, lam_d, zeros, rev, **kw_c)
        prev_l, _ = _rg_scan(u_l, B, T, w_hi, w_lo, bias, lam_d, h_end, rev, **kw_l)
    return prev_l, prev_c


def _layer_norm(v, g, b):
    mu = jnp.mean(v, axis=-1, keepdims=True)
    d = v - mu
    var = jnp.mean(d * d, axis=-1, keepdims=True)
    return d * lax.rsqrt(var + LN_EPS) * g + b


def _outproj_kernel(hyl_ref, hyc_ref, att_ref, rgl_ref, rgc_ref, h_ref, g1_ref, sh2_ref, sc2_ref,
                    lng_ref, lnb_ref, why_ref, watt_ref, wrg_ref, wrh_ref, wrl_ref, br_ref,
                    oh_ref, om_ref, og_ref, oi_ref, oc_ref, *, alpha, n_lat_tiles):
    is_ctx = pl.program_id(0) >= n_lat_tiles
    hy = jnp.where(is_ctx, hyc_ref[...], hyl_ref[...])
    rg = jnp.where(is_ctx, rgc_ref[...], rgl_ref[...])
    y = _dot(hy.astype(bf16), why_ref[...]) + _dot(att_ref[...], watt_ref[...]) + _dot(rg, wrg_ref[...])
    hn = _layer_norm(alpha * h_ref[...] + g1_ref[...] * y, lng_ref[...], lnb_ref[...])
    oh_ref[...] = hn
    m = hn * (1.0 + sc2_ref[...]) + sh2_ref[...]
    om_ref[...] = m.astype(bf16)
    scores = jax.nn.sigmoid(_dot3(m, wrh_ref[...], wrl_ref[...]))
    sel = scores + br_ref[...]
    lane = lax.broadcasted_iota(jnp.int32, sel.shape, 1)
    picked = jnp.zeros_like(scores)
    chosen = jnp.zeros(sel.shape, jnp.int32)
    taken = jnp.zeros_like(scores)
    for k in range(TOP_K):
        mx = jnp.max(sel, axis=-1, keepdims=True)
        first = jnp.min(jnp.where(sel == mx, lane, LANE), axis=-1, keepdims=True)
        hit = lane == first
        picked = jnp.where(hit, scores, picked)
        taken = jnp.where(hit, 1.0, taken)
        sel = jnp.where(hit, -jnp.inf, sel)
        chosen = jnp.where(lane == k, first, chosen)
    og_ref[...] = picked / jnp.sum(picked, axis=-1, keepdims=True) * ROUTED_SCALE
    oi_ref[...] = chosen
    for s in range(oc_ref.shape[0]):
        oc_ref[s] = jnp.sum(taken[s * DISPATCH_TILE:(s + 1) * DISPATCH_TILE], axis=0, keepdims=True)


def _outproj(hy_l, hy_c, att, rg_l, rg_c, h, mods3, ln_g, ln_b, w_hy, w_att, w_rg, wr_hi, wr_lo, b_router,
             seg, tm, alpha):
    n = h.shape[0]
    nl = hy_l.shape[0] // tm
    row = lambda w: pl.BlockSpec((tm, w), lambda i: (i, 0))
    lat = lambda w: pl.BlockSpec((tm, w), lambda i: (jnp.minimum(i, nl - 1), 0))
    ctx = lambda w: pl.BlockSpec((tm, w), lambda i: (jnp.maximum(i - nl, 0), 0))
    one = lambda a: pl.BlockSpec(a.shape, lambda i: (0,) * a.ndim)
    return pl.pallas_call(
        functools.partial(_outproj_kernel, alpha=alpha, n_lat_tiles=nl),
        out_shape=[jax.ShapeDtypeStruct((n, D_MODEL), f32), jax.ShapeDtypeStruct((n, D_MODEL), bf16),
                   jax.ShapeDtypeStruct((n, LANE), f32), jax.ShapeDtypeStruct((n, LANE), jnp.int32),
                   jax.ShapeDtypeStruct((n // DISPATCH_TILE, 1, LANE), f32)],
        grid=(n // tm,),
        in_specs=[lat(HY_W), ctx(HY_W), row(QP_W), lat(RG_W), ctx(RG_W), row(D_MODEL),
                  _mod_spec(seg, 2), _mod_spec(seg, 3), _mod_spec(seg, 4),
                  one(ln_g), one(ln_b), one(w_hy), one(w_att), one(w_rg), one(wr_hi), one(wr_lo),
                  one(b_router)],
        out_specs=[row(D_MODEL), row(D_MODEL), row(LANE), row(LANE),
                   pl.BlockSpec((tm // DISPATCH_TILE, 1, LANE), lambda i: (i, 0, 0))],
        compiler_params=_cparams(("parallel",)),
        name="outproj_router",
    )(hy_l, hy_c, att, rg_l, rg_c, h, mods3, mods3, mods3, ln_g, ln_b, w_hy, w_att, w_rg, wr_hi, wr_lo,
      b_router)


def _xs_rows(tile):
    return -(-(TOP_K * tile + N_EXPERTS * (RUN_ALIGN - 1)) // 256) * 256


def _perm_matrix(idx, runstart, gates):
    t = idx.shape[0]
    lane = lax.broadcasted_iota(jnp.int32, (t, LANE), 1)
    hits = [lane == idx[:, k:k + 1] for k in range(TOP_K)]
    sel = jnp.zeros((t, LANE), f32)
    for hit in hits:
        sel = jnp.where(hit, 1.0, sel)
    earlier = (lax.broadcasted_iota(jnp.int32, (t, t), 1) < lax.broadcasted_iota(jnp.int32, (t, t), 0))
    rank = _dot(jnp.where(earlier, 1.0, 0.0).astype(bf16), sel.astype(bf16))
    posmat = rank + runstart
    slot = lax.broadcasted_iota(jnp.int32, (t, _xs_rows(t)), 1).astype(jnp.int16)
    acc = jnp.zeros(slot.shape, bf16)
    for hit in hits:
        pos = jnp.sum(jnp.where(hit, posmat, 0.0), axis=-1, keepdims=True).astype(jnp.int32).astype(jnp.int16)
        if gates is None:
            val = jnp.ones((t, 1), bf16)
        else:
            val = jnp.sum(jnp.where(hit, gates, 0.0), axis=-1, keepdims=True).astype(bf16)
        acc = jnp.where(slot == pos, val, acc)
    return acc


def _for_pieces(plan_ref, fn):
    for k, size in enumerate(PIECE_SIZES):
        def body(j, carry, k=k, size=size):
            fn(pl.multiple_of(plan_ref[2 * k, j], RUN_ALIGN), pl.multiple_of(plan_ref[2 * k + 1, j], RUN_ALIGN),
               size)
            return carry

        lax.fori_loop(0, plan_ref[PLAN_COUNT_ROW, k], body, 0)


def _dispatch_kernel(gap_ref, reg_ref, x_ref, idx_ref, rsf_ref, plan_ref, prev_ref, s_ref,
                     xs_ref, zero_ref, sem):
    i = pl.program_id(0)
    slot = i % 2

    @pl.when(i == 0)
    def _():
        zero_ref[...] = jnp.zeros(zero_ref.shape, zero_ref.dtype)

        def fill(action):
            def body(e, carry):
                @pl.when(reg_ref[e] > 0)
                def _():
                    dst = pl.multiple_of(gap_ref[e], RUN_ALIGN)
                    action(pltpu.make_async_copy(zero_ref, s_ref.at[pl.ds(dst, FFN_TILE)], sem))
                return carry
            lax.fori_loop(0, N_EXPERTS, body, 0)

        fill(lambda cp: cp.start())
        fill(lambda cp: cp.wait())

    p_t = _perm_matrix(idx_ref[...], rsf_ref[...], None)
    xs_ref[slot] = lax.dot_general(p_t, x_ref[...], (((0,), (0,)), ((), ())),
                                   preferred_element_type=f32).astype(bf16)

    def copy(buf, src, dst, sz):
        return pltpu.make_async_copy(xs_ref.at[buf].at[pl.ds(src, sz)], s_ref.at[pl.ds(dst, sz)], sem)

    @pl.when(i > 0)
    def _():
        _for_pieces(prev_ref, lambda s, d, sz: copy(1 - slot, s, d, sz).wait())

    _for_pieces(plan_ref, lambda s, d, sz: copy(slot, s, d, sz).start())

    @pl.when(i == pl.num_programs(0) - 1)
    def _():
        _for_pieces(plan_ref, lambda s, d, sz: copy(slot, s, d, sz).wait())


def _ffn_kernel(te_ref, nu_ref, x_ref, wg_ref, wu_ref, wd_ref, o_ref):
    @pl.when(pl.program_id(0) < nu_ref[0])
    def _():
        x = x_ref[...]
        hid = jax.nn.silu(_dot(x, wg_ref[...].astype(bf16))) * _dot(x, wu_ref[...].astype(bf16))
        o_ref[...] = _dot(hid.astype(bf16), wd_ref[...].astype(bf16)).astype(bf16)


def _combine_kernel(z_ref, plan_ref, idx_ref, gate_ref, rsf_ref, x_ref, sgu_ref, sd_ref,
                    h_ref, g2_ref, lng_ref, lnb_ref, o_ref, zbuf_ref, sem, *, alpha):
    i = pl.program_id(0)

    @pl.when(i == 0)
    def _():
        zbuf_ref[...] = jnp.zeros(zbuf_ref.shape, zbuf_ref.dtype)

    def copy(src, dst, sz):
        return pltpu.make_async_copy(z_ref.at[pl.ds(dst, sz)], zbuf_ref.at[pl.ds(src, sz)], sem)

    _for_pieces(plan_ref, lambda s, d, sz: copy(s, d, sz).start())
    hs = _dot(x_ref[...], sgu_ref[...])
    shared = _dot((jax.nn.silu(hs[:, :EXPERT_FF]) * hs[:, EXPERT_FF:]).astype(bf16), sd_ref[...])
    w_t = _perm_matrix(idx_ref[...], rsf_ref[...], gate_ref[...])
    _for_pieces(plan_ref, lambda s, d, sz: copy(s, d, sz).wait())
    y = _dot(w_t, zbuf_ref[...]) + shared
    o_ref[...] = _layer_norm(alpha * h_ref[...] + g2_ref[...] * y, lng_ref[...], lnb_ref[...])


def _route_plan(counts, n_ffn_tiles):
    c = counts[:, 0, :N_EXPERTS].astype(jnp.int32)
    lens = (c + RUN_ALIGN - 1) // RUN_ALIGN * RUN_ALIGN
    runstart = jnp.cumsum(lens, axis=1) - lens
    region = (jnp.sum(lens, axis=0) + FFN_TILE - 1) // FFN_TILE * FFN_TILE
    region_end = jnp.cumsum(region)
    dest = (region_end - region)[None, :] + jnp.cumsum(lens, axis=0) - lens
    n_used = region_end[-1] // FFN_TILE
    tile_row = jnp.minimum(jnp.arange(n_ffn_tiles, dtype=jnp.int32), n_used - 1) * FFN_TILE
    tile_expert = jnp.minimum(jnp.sum(tile_row[:, None] >= region_end[None, :], axis=1), N_EXPERTS - 1)
    runstart_f = jnp.zeros((c.shape[0], 1, LANE), f32).at[:, 0, :N_EXPERTS].set(runstart.astype(f32))

    def compact(mask, *vals):
        pos = jnp.cumsum(mask, axis=1) - mask
        hit = jnp.logical_and(mask[:, :, None] > 0, pos[:, :, None] == jnp.arange(LANE, dtype=jnp.int32))
        return [jnp.sum(jnp.where(hit, v[:, :, None], 0), axis=1) for v in vals], jnp.sum(mask, axis=1)

    n_chunks = lens // RUN_CHUNK
    j = jnp.arange(DISPATCH_TILE // RUN_CHUNK, dtype=jnp.int32) * RUN_CHUNK
    wide = lambda v: (v[:, :, None] + j).reshape(v.shape[0], -1)
    chunk_mask = (j < (n_chunks * RUN_CHUNK)[:, :, None]).astype(jnp.int32).reshape(lens.shape[0], -1)
    (s64, d64), c64 = compact(chunk_mask, wide(runstart), wide(dest))
    off = n_chunks * RUN_CHUNK
    (s32, d32), c32 = compact((lens >> 5) & 1, runstart + off, dest + off)
    off = off + (lens & 32)
    (s16, d16), c16 = compact((lens >> 4) & 1, runstart + off, dest + off)
    count_row = jnp.zeros_like(s64).at[:, 0].set(c64).at[:, 1].set(c32).at[:, 2].set(c16)
    pieces = jnp.stack([s64, d64, s32, d32, s16, d16, count_row, jnp.zeros_like(s64)], axis=1)
    return dict(pieces=pieces.astype(jnp.int32), gap=region_end - FFN_TILE, region=region,
                n_used=n_used.reshape(1), tile_expert=tile_expert.astype(jnp.int32), runstart_f=runstart_f)


def _moe(m, gates, idx, counts, layer, w_gate, w_up, w_down, sgu, sd, h, mods3, ln_g, ln_b, seg, alpha):
    n = h.shape[0]
    t = DISPATCH_TILE
    nt = n // t
    xs_rows = _xs_rows(t)
    rows_max = TOP_K * n + nt * N_EXPERTS * (RUN_ALIGN - 1) + N_EXPERTS * (FFN_TILE - RUN_ALIGN)
    n_ffn_tiles = -(-rows_max // FFN_TILE)
    plan = _route_plan(counts, n_ffn_tiles)
    tile = lambda w: pl.BlockSpec((t, w), lambda i, *_: (i, 0))
    rsf_spec = pl.BlockSpec((None, 1, LANE), lambda i, *_: (i, 0, 0))
    one = lambda a: pl.BlockSpec(a.shape, lambda i, *_: (0,) * a.ndim)
    any_spec = pl.BlockSpec(memory_space=pl.ANY)

    pieces = plan["pieces"]
    plan_spec = pl.BlockSpec((None, PLAN_ROWS, LANE), lambda i, *_: (i, 0, 0), memory_space=pltpu.SMEM)
    prev_spec = pl.BlockSpec((None, PLAN_ROWS, LANE), lambda i, *_: (jnp.maximum(i - 1, 0), 0, 0),
                             memory_space=pltpu.SMEM)
    sorted_x = pl.pallas_call(
        _dispatch_kernel,
        out_shape=jax.ShapeDtypeStruct((n_ffn_tiles * FFN_TILE, D_MODEL), bf16),
        grid_spec=pltpu.PrefetchScalarGridSpec(
            num_scalar_prefetch=2, grid=(nt,),
            in_specs=[tile(D_MODEL), tile(LANE), rsf_spec, plan_spec, prev_spec],
            out_specs=any_spec,
            scratch_shapes=[pltpu.VMEM((2, xs_rows, D_MODEL), bf16), pltpu.VMEM((FFN_TILE, D_MODEL), bf16),
                            pltpu.SemaphoreType.DMA]),
        compiler_params=_cparams(("arbitrary",)),
        name="moe_dispatch",
    )(plan["gap"], plan["region"], m, idx, plan["runstart_f"], pieces, pieces)

    used = lambda i, te, nu: jnp.minimum(i, nu[0] - 1)
    sorted_z = pl.pallas_call(
        _ffn_kernel,
        out_shape=jax.ShapeDtypeStruct((n_ffn_tiles * FFN_TILE, D_MODEL), bf16),
        grid_spec=pltpu.PrefetchScalarGridSpec(
            num_scalar_prefetch=2, grid=(n_ffn_tiles,),
            in_specs=[pl.BlockSpec((FFN_TILE, D_MODEL), lambda i, te, nu: (used(i, te, nu), 0)),
                      pl.BlockSpec((None, None, D_MODEL, EXPERT_FF), lambda i, te, nu: (layer, te[i], 0, 0)),
                      pl.BlockSpec((None, None, D_MODEL, EXPERT_FF), lambda i, te, nu: (layer, te[i], 0, 0)),
                      pl.BlockSpec((None, None, EXPERT_FF, D_MODEL), lambda i, te, nu: (layer, te[i], 0, 0))],
            out_specs=pl.BlockSpec((FFN_TILE, D_MODEL), lambda i, te, nu: (used(i, te, nu), 0))),
        compiler_params=_cparams(("arbitrary",)),
        name="moe_ffn",
    )(plan["tile_expert"], plan["n_used"], sorted_x, w_gate, w_up, w_down)

    return pl.pallas_call(
        functools.partial(_combine_kernel, alpha=alpha),
        out_shape=jax.ShapeDtypeStruct((n, D_MODEL), f32),
        grid=(nt,),
        in_specs=[any_spec, plan_spec, tile(LANE), tile(LANE), rsf_spec, tile(D_MODEL), one(sgu), one(sd),
                  tile(D_MODEL), pl.BlockSpec((None, 1, D_MODEL), lambda i, *_: (seg(i), 0, 5)),
                  one(ln_g), one(ln_b)],
        out_specs=tile(D_MODEL),
        scratch_shapes=[pltpu.VMEM((xs_rows, D_MODEL), bf16), pltpu.SemaphoreType.DMA],
        compiler_params=_cparams(("arbitrary",)),
        name="moe_combine",
    )(sorted_z, pieces, idx, gates, plan["runstart_f"], m, sgu, sd, h, mods3, ln_g, ln_b)


def _pad_heads_cols(w, n_heads):
    lead = w.shape[:-1]
    w = w.reshape(*lead, n_heads, HEAD_DIM)
    w = jnp.concatenate([w, jnp.zeros_like(w)], axis=-1)
    return w.reshape(*lead, n_heads * HEAD_PAD)


def _pad_in_proj(w_in):
    hy, q, k, v, rgx, rgg = (w_in[..., :HY_END], w_in[..., HY_END:Q_END], w_in[..., Q_END:K_END],
                             w_in[..., K_END:V_END], w_in[..., V_END:RGX_END], w_in[..., RGX_END:])
    return jnp.concatenate([hy, _pad_heads_cols(q, N_HEADS), _pad_heads_cols(k, N_KV_HEADS),
                            _pad_heads_cols(v, N_KV_HEADS), rgx, rgg], axis=-1).astype(bf16)


def kernel(x, c, ctx, c_ctx, w_mod, b_mod, ln1_g, ln1_b, ln2_g, ln2_b, w_in, w_out,
           hy_short_w, hy_short_b, hy_f_w1, hy_f_b1, hy_f_freq, hy_f_w2, hy_f_b2, hy_f_w3, hy_skip,
           q_norm, k_norm, rg_conv_w, rg_conv_b, rg_lambda, rg_w_a, rg_b_a, rg_w_x, rg_b_x,
           w_router, b_router, w_gate, w_up, w_down, ws_gate, ws_up, ws_down):
    B, T, D = x.shape
    Lc = ctx.shape[1]
    depth = w_mod.shape[0]
    bt, bc = B * T, B * Lc
    n_tok = bt + bc
    tm = TOKEN_TILE
    assert D == D_MODEL and T % tm == 0 and bc % tm == 0 and B % 2 == 0
    assert T % ATT_Q_TILE == 0 and Lc == ATT_Q_TILE and T % GRID_W == 0
    alpha = (2 * depth) ** 0.25
    tiles_per_seq = T // tm
    seg = lambda i: jnp.minimum(i // tiles_per_seq, B)
    pos_block = lambda i: jnp.where(i < B * tiles_per_seq, i % tiles_per_seq, tiles_per_seq)
    moe_tm = DISPATCH_TILE
    assert T % moe_tm == 0 and bc % moe_tm == 0
    moe_seg = lambda i: jnp.minimum(i // (T // moe_tm), B)
    out_tm = max(t for t in ROUTER_TILES if T % t == 0 and bc % t == 0)
    out_seg = lambda i: jnp.minimum(i // (T // out_tm), B)

    rows = -(-(B + 1) // 8) * 8
    cond = jnp.zeros((rows, D), f32).at[:B].set(c).at[B].set(c_ctx)
    mods = _modulation(cond, w_mod, b_mod)

    cos, sin = _rope_tables(T, tm)
    w_in_pad = _pad_in_proj(w_in)
    w_out_hy = w_out[:, :HY_W].astype(bf16)
    w_out_att = _pad_heads_cols(w_out[:, HY_W:HY_W + ATT_W].transpose(0, 2, 1), N_HEADS).transpose(0, 2, 1).astype(bf16)
    w_out_rg = w_out[:, HY_W + ATT_W:].astype(bf16)
    wr_pad = jnp.zeros((depth, D, LANE), f32).at[:, :, :N_EXPERTS].set(w_router)
    br_pad = jnp.full((depth, 1, LANE), -jnp.inf, f32).at[:, 0, :N_EXPERTS].set(b_router)
    sgu = jnp.concatenate([ws_gate, ws_up], axis=-1).astype(bf16)
    sd = ws_down.astype(bf16)

    h = jnp.concatenate([x.reshape(bt, D), ctx.reshape(bc, D)], axis=0)
    for l in range(depth):
        mods3 = mods[l].reshape(rows, 1, 6 * D)
        hy, q, kv, rg = _inproj(h, mods3, w_in_pad[l], seg, tm)

        xz_l = _dwconv(hy, hy_short_w[l], hy_short_b[l], T, 0, B, 0, 3)
        xz_c = _dwconv(hy, hy_short_w[l], hy_short_b[l], Lc, bt // Lc, B, 0, 3)
        fargs = (hy_f_w1[l], hy_f_b1[l], hy_f_freq[l], hy_f_w2[l], hy_f_b2[l], hy_f_w3[l])
        hy_l = _hyena_long(xz_l, _hyena_filters(T, *fargs), hy_skip[l], B, T)
        hy_c = _hyena_short_seq(xz_c, _hyena_filters(Lc, *fargs), hy_skip[l], B, Lc)

        qp, kvp = _qkprep(q, kv, cos, sin, q_norm[l], k_norm[l], pos_block, tm)
        att = _attention(qp, kvp, B, T, Lc)

        rg_l, rg_c = _rglru(rg, B, T, Lc, rg_conv_w[l], rg_conv_b[l], rg_lambda[l],
                            rg_w_a[l], rg_b_a[l], rg_w_x[l], rg_b_x[l])

        wr_hi, wr_lo = _split(wr_pad[l])
        h, m, gates, chosen, counts = _outproj(
            hy_l, hy_c, att, rg_l, rg_c, h, mods3, ln1_g[l][None, :], ln1_b[l][None, :],
            w_out_hy[l], w_out_att[l], w_out_rg[l], wr_hi, wr_lo, br_pad[l], out_seg, out_tm, alpha)
        h = _moe(m, gates, chosen, counts, l, w_gate, w_up, w_down, sgu[l], sd[l], h, mods3,
                 ln2_g[l][None, :], ln2_b[l][None, :], moe_seg, alpha)
    return h[:bt].reshape(B, T, D)
```

```python
import functools
import math

import numpy as np
import jax
import jax.numpy as jnp
from jax import lax
from jax.experimental import pallas as pl
from jax.experimental.pallas import tpu as pltpu

f32 = jnp.float32
bf16 = jnp.bfloat16

D_MODEL = 1024
GRID_W = 64
HY_W = 256
N_HEADS = 8
N_KV_HEADS = 2
N_GROUPS = N_HEADS // N_KV_HEADS
HEAD_DIM = 64
ATT_W = N_HEADS * HEAD_DIM
KV_W = N_KV_HEADS * HEAD_DIM
RG_W = 256
RG_BLOCKS = 4
HY_END = 3 * HY_W
Q_END = HY_END + ATT_W
K_END = Q_END + KV_W
V_END = K_END + KV_W
RGX_END = V_END + RG_W
PROJ_W = RGX_END + RG_W
HY_BANDS = 8
HY_EMB = 1 + 2 * HY_BANDS
HY_MAX_DECAY = math.log(1e-2) / 0.3
HY_MIN_DECAY = math.log(1e-2) / 1.5
ROPE_THETA = 10000.0
QK_EPS = 1e-6
RG_C = 8.0
N_EXPERTS = 64
TOP_K = 8
EXPERT_FF = 256
ROUTED_SCALE = 2.5
LN_EPS = 1e-6

LANE = 128
HEAD_PAD = LANE
QP_W = N_HEADS * HEAD_PAD
KVP_W = N_KV_HEADS * HEAD_PAD
PROJ_PAD_W = HY_END + QP_W + 2 * KVP_W + 2 * RG_W
VMEM_LIMIT = 52 * 1024 * 1024

FFT_N2 = 128
FFT_PASSES = 1
FILTER_PASSES = 3
TOKEN_TILE = 512
ATT_Q_TILE = 256
SCAN_TILE = 256
DISPATCH_TILE = 256
ROUTER_TILES = (1024, 512, 256)
RUN_ALIGN = 16
RUN_CHUNK = 64
PIECE_SIZES = (RUN_CHUNK, 32, 16)
PLAN_ROWS = 8
PLAN_COUNT_ROW = 6
FFN_TILE = 2048


def _cparams(sem):
    return pltpu.CompilerParams(dimension_semantics=sem, vmem_limit_bytes=VMEM_LIMIT)


def _dot(a, b):
    return jnp.dot(a, b, preferred_element_type=f32)


def _split(x):
    hi = x.astype(bf16)
    lo = (x - hi.astype(f32)).astype(bf16)
    return hi, lo


def _np_split(x):
    x = jnp.asarray(np.asarray(x, np.float32))
    return _split(x)


def _dot3(a, w_hi, w_lo):
    a_hi, a_lo = _split(a)
    return _dot(a_hi, w_hi) + _dot(a_lo, w_hi) + _dot(a_hi, w_lo)


def _stack_dft(mat):
    re_hi, re_lo = _np_split(mat.real)
    im_hi, im_lo = _np_split(mat.imag)
    return jnp.concatenate([re_hi, im_hi, re_lo, im_lo], axis=0)


def _apply_stack(stack, x, passes):
    m2 = stack.shape[0] // 2
    x_hi, x_lo = _split(x)
    if passes == 1:
        return _dot(stack[:m2], x_hi)
    r = _dot(stack, x_hi)
    out = r[:m2] + r[m2:]
    if passes >= 3:
        out = out + _dot(stack[:m2], x_lo)
    return out


def _mod_kernel(a_ref, w_ref, b_ref, o_ref):
    a = a_ref[...]
    a = a * jax.nn.sigmoid(a)
    w_hi, w_lo = _split(w_ref[...])
    o_ref[...] = _dot3(a, w_hi, w_lo) + b_ref[...]


def _modulation(cond, w_mod, b_mod):
    depth, d, n = w_mod.shape
    rows = cond.shape[0]
    tn = 512
    return pl.pallas_call(
        _mod_kernel,
        out_shape=jax.ShapeDtypeStruct((depth, rows, n), f32),
        grid=(depth, n // tn),
        in_specs=[
            pl.BlockSpec((rows, d), lambda l, j: (0, 0)),
            pl.BlockSpec((None, d, tn), lambda l, j: (l, 0, j)),
            pl.BlockSpec((None, 1, tn), lambda l, j: (l, 0, j)),
        ],
        out_specs=pl.BlockSpec((None, rows, tn), lambda l, j: (l, 0, j)),
        compiler_params=_cparams(("parallel", "parallel")),
        name="modulation",
    )(cond, w_mod, b_mod.reshape(depth, 1, n))


def _inproj_kernel(h_ref, sh_ref, sc_ref, w_ref, ohy_ref, oq_ref, okv_ref, org_ref):
    a = h_ref[...] * (1.0 + sc_ref[...]) + sh_ref[...]
    p = _dot(a.astype(bf16), w_ref[...])
    c0, c1, c2 = HY_END, HY_END + QP_W, HY_END + QP_W + 2 * KVP_W
    ohy_ref[...] = p[:, :c0]
    oq_ref[...] = p[:, c0:c1]
    okv_ref[...] = p[:, c1:c2]
    org_ref[...] = p[:, c2:]


def _mod_spec(seg, j):
    return pl.BlockSpec((None, 1, D_MODEL), lambda i: (seg(i), 0, j))


def _inproj(h, mods3, w_pad, seg, tm):
    n = h.shape[0]
    widths = (HY_END, QP_W, 2 * KVP_W, 2 * RG_W)
    return pl.pallas_call(
        _inproj_kernel,
        out_shape=[jax.ShapeDtypeStruct((n, w), f32) for w in widths],
        grid=(n // tm,),
        in_specs=[
            pl.BlockSpec((tm, D_MODEL), lambda i: (i, 0)),
            _mod_spec(seg, 0),
            _mod_spec(seg, 1),
            pl.BlockSpec((D_MODEL, PROJ_PAD_W), lambda i: (0, 0)),
        ],
        out_specs=[pl.BlockSpec((tm, w), lambda i: (i, 0)) for w in widths],
        compiler_params=_cparams(("parallel",)),
        name="inproj",
    )(h, mods3, mods3, w_pad)


def _dwconv_kernel(x_ref, w_ref, b_ref, o_ref, *, width, left):
    x = x_ref[...]
    n = x.shape[0]
    row = lax.broadcasted_iota(jnp.int32, x.shape, 0)
    acc = jnp.zeros_like(x) + b_ref[...]
    for j in range(width):
        off = j - left
        if off == 0:
            xs = x
        else:
            xs = pltpu.roll(x, (-off) % n, axis=0)
            valid = jnp.logical_and(row + off >= 0, row + off < n)
            xs = jnp.where(valid, xs, 0.0)
        acc = acc + xs * w_ref[j:j + 1, :]
    o_ref[...] = acc


def _dwconv(x, w, b, seq_len, row_block0, n_seq, col_block0, n_col_blocks):
    width = w.shape[0]
    wp = jnp.zeros((8, w.shape[1]), f32).at[:width].set(w)
    ct = 256
    return pl.pallas_call(
        functools.partial(_dwconv_kernel, width=width, left=(width - 1) // 2),
        out_shape=jax.ShapeDtypeStruct((n_seq * seq_len, n_col_blocks * ct), f32),
        grid=(n_seq, n_col_blocks),
        in_specs=[
            pl.BlockSpec((seq_len, ct), lambda s, c: (row_block0 + s, col_block0 + c)),
            pl.BlockSpec((8, ct), lambda s, c: (0, c)),
            pl.BlockSpec((1, ct), lambda s, c: (0, c)),
        ],
        out_specs=pl.BlockSpec((seq_len, ct), lambda s, c: (s, c)),
        compiler_params=_cparams(("parallel", "parallel")),
        name="dwconv",
    )(x, wp, b.reshape(1, -1))


def _filter_kernel(z_ref, w1h_ref, w1l_ref, b1_ref, fr_ref, w2h_ref, w2l_ref, b2_ref,
                   w3h_ref, w3l_ref, win_ref, o_ref):
    fr = fr_ref[...]
    f = jnp.sin(fr * (_dot3(z_ref[...], w1h_ref[...], w1l_ref[...]) + b1_ref[...]))
    f = jnp.sin(fr * (_dot3(f, w2h_ref[...], w2l_ref[...]) + b2_ref[...]))
    f = _dot3(f, w3h_ref[...], w3l_ref[...])
    win = win_ref[...]
    o_ref[...] = f * jnp.concatenate([win] * 2, axis=1)


def _filter_features(L):
    t = np.linspace(0.0, 1.0, L, dtype=np.float32)[:, None].astype(np.float64)
    w = np.float32(2.0 * math.pi) * np.arange(L, dtype=np.float32)[:, None] / np.float32(L)
    bands = np.linspace(1e-4, HY_BANDS - 1, HY_BANDS, dtype=np.float32)
    arg = (bands * w).astype(np.float64)
    z = np.concatenate([t, np.cos(arg), -np.sin(arg)], axis=-1)
    deltas = np.abs(np.linspace(HY_MIN_DECAY, HY_MAX_DECAY, HY_W, dtype=np.float32)).astype(np.float64)
    win = np.exp(-t * deltas)
    zp = np.zeros((2 * L, LANE), np.float32)
    zp[:L, :HY_EMB] = z
    zp[L + 1:, :HY_EMB] = z[:0:-1]
    win2 = np.zeros((2 * L, HY_W), np.float32)
    win2[:L] = win
    win2[L + 1:] = win[:0:-1]
    return jnp.asarray(zp), jnp.asarray(win2)


def _pad2(w, rows, cols):
    return jnp.zeros((rows, cols), f32).at[:w.shape[0], :w.shape[1]].set(w)


def _hyena_filters(L, w1, b1, freq, w2, b2, w3):
    zp, win = _filter_features(L)
    w1h, w1l = _split(_pad2(w1, LANE, LANE))
    w2h, w2l = _split(_pad2(w2, LANE, LANE))
    w3h, w3l = _split(_pad2(w3, LANE, 4 * HY_W))
    b1p = _pad2(b1[None, :], 1, LANE)
    b2p = _pad2(b2[None, :], 1, LANE)
    frp = _pad2(freq[None, :], 1, LANE)
    tl = min(L, 512)
    half = L // tl
    full = lambda shape: pl.BlockSpec(shape, lambda i: (0, 0))
    w3_spec = pl.BlockSpec((LANE, 2 * HY_W), lambda i: (0, i // half))
    return pl.pallas_call(
        _filter_kernel,
        out_shape=jax.ShapeDtypeStruct((2 * L, 2 * HY_W), f32),
        grid=(2 * half,),
        in_specs=[
            pl.BlockSpec((tl, LANE), lambda i: (i, 0)),
            full((LANE, LANE)), full((LANE, LANE)), full((1, LANE)), full((1, LANE)),
            full((LANE, LANE)), full((LANE, LANE)), full((1, LANE)),
            w3_spec, w3_spec,
            pl.BlockSpec((tl, HY_W), lambda i: (i, 0)),
        ],
        out_specs=pl.BlockSpec((tl, 2 * HY_W), lambda i: (i, 0)),
        compiler_params=_cparams(("parallel",)),
        name="hyena_filter",
    )(zp, w1h, w1l, b1p, frp, w2h, w2l, b2p, w3h, w3l, win)


def _dft_consts(n):
    n1 = n // FFT_N2
    k = np.arange(n1)
    f_n1 = np.exp(-2j * np.pi * np.outer(k, k) / n1)
    k2 = np.arange(FFT_N2)
    f_n2 = np.exp(-2j * np.pi * np.outer(k2, k2) / FFT_N2)
    tw = np.exp(-2j * np.pi * np.outer(k, k2) / n).reshape(n, 1)
    tw_re = jnp.asarray(np.broadcast_to(tw.real, (n, LANE)).astype(np.float32))
    tw_im = jnp.asarray(np.broadcast_to(tw.imag, (n, LANE)).astype(np.float32))
    return n1, f_n1, f_n2, tw_re, tw_im


def _fa_kernel(u_ref, fs_ref, are_ref, aim_ref, *, grp, passes, real_only):
    fs = fs_ref[...]
    n1 = fs.shape[0] // 4
    jb, c = u_ref.shape[2], u_ref.shape[3]
    for g in range(jb // grp):
        def gather(bi):
            return jnp.concatenate([u_ref[bi, :, g * grp + jj, :] for jj in range(grp)], axis=1)
        p = _apply_stack(fs, gather(0), passes)
        if real_only:
            re, im = p[:n1], p[n1:]
        else:
            q = _apply_stack(fs, gather(1), passes)
            re, im = p[:n1] - q[n1:], p[n1:] + q[:n1]
        for jj in range(grp):
            are_ref[:, g * grp + jj, :] = re[:, jj * c:(jj + 1) * c]
            aim_ref[:, g * grp + jj, :] = im[:, jj * c:(jj + 1) * c]


def _fft_step_a(u4, fs, col_block, c, real_only, passes):
    s, n1_in = u4.shape[0], u4.shape[1]
    n1 = fs.shape[0] // 4
    per = 1 if real_only else 2
    jb = 32
    grp = max(1, 1024 // c)
    shape = jax.ShapeDtypeStruct((s // per, n1, FFT_N2, c), f32)
    return pl.pallas_call(
        functools.partial(_fa_kernel, grp=grp, passes=passes, real_only=real_only),
        out_shape=[shape, shape],
        grid=(s // per, FFT_N2 // jb),
        in_specs=[
            pl.BlockSpec((per, n1_in, jb, c), lambda p, j: (p, 0, j, col_block)),
            pl.BlockSpec(fs.shape, lambda p, j: (0, 0)),
        ],
        out_specs=[pl.BlockSpec((None, n1, jb, c), lambda p, j: (p, 0, j, 0))] * 2,
        compiler_params=_cparams(("parallel", "parallel")),
        name="fft_step_a",
    )(u4, fs)


def _mid_kernel(are_ref, aim_ref, twr_ref, twi_ref, f2_ref, *rest, kb, passes, fwd_only, inv_n):
    if fwd_only:
        ore_ref, oim_ref = rest
    else:
        kr_ref, ki_ref, ore_ref, oim_ref = rest
    f2 = f2_ref[...]
    c = are_ref.shape[1]
    for kk in range(kb):
        rows = pl.ds(kk * FFT_N2, FFT_N2)
        ar, ai = are_ref[rows, :], aim_ref[rows, :]
        tr = jnp.concatenate([twr_ref[rows, :]] * (c // LANE), axis=1)
        ti = jnp.concatenate([twi_ref[rows, :]] * (c // LANE), axis=1)
        xr = ar * tr - ai * ti
        xi = ar * ti + ai * tr
        p = _apply_stack(f2, xr, passes)
        q = _apply_stack(f2, xi, passes)
        sr = p[:FFT_N2] - q[FFT_N2:]
        si = p[FFT_N2:] + q[:FFT_N2]
        if fwd_only:
            ore_ref[rows, :] = sr
            oim_ref[rows, :] = si
        else:
            kr, ki = kr_ref[rows, :], ki_ref[rows, :]
            yr = sr * kr - si * ki
            yi = sr * ki + si * kr
            p2 = _apply_stack(f2, yr, passes)
            q2 = _apply_stack(f2, yi, passes)
            br = p2[:FFT_N2] + q2[FFT_N2:]
            bi = q2[:FFT_N2] - p2[FFT_N2:]
            ore_ref[rows, :] = (br * tr + bi * ti) * inv_n
            oim_ref[rows, :] = (bi * tr - br * ti) * inv_n


def _fft_mid(a_re, a_im, tw_re, tw_im, f2s, kf=None, order=0, passes=FFT_PASSES):
    p, n, c = a_re.shape
    kb = min(4, n // FFT_N2)
    rb = kb * FFT_N2
    fwd_only = kf is None
    blk = pl.BlockSpec((None, rb, c), lambda i, k: (i, k, 0))
    in_specs = [blk, blk,
                pl.BlockSpec((rb, LANE), lambda i, k: (k, 0)),
                pl.BlockSpec((rb, LANE), lambda i, k: (k, 0)),
                pl.BlockSpec(f2s.shape, lambda i, k: (0, 0))]
    args = [a_re, a_im, tw_re, tw_im, f2s]
    if not fwd_only:
        in_specs += [pl.BlockSpec((rb, c), lambda i, k: (k, order))] * 2
        args += list(kf)
    shape = jax.ShapeDtypeStruct((p, n, c), f32)
    return pl.pallas_call(
        functools.partial(_mid_kernel, kb=kb, passes=passes, fwd_only=fwd_only, inv_n=1.0 / n),
        out_shape=[shape, shape],
        grid=(p, n // rb),
        in_specs=in_specs,
        out_specs=[blk, blk],
        compiler_params=_cparams(("parallel", "parallel")),
        name="fft_mid",
    )(*args)


def _fai_kernel(bre_ref, bim_ref, gs_ref, u_ref, gate_ref, skip_ref, o_ref, *, grp, passes):
    gs = gs_ref[...]
    n1h = gs.shape[0] // 4
    jb, c = u_ref.shape[2], u_ref.shape[3]
    skip = skip_ref[...]
    for g in range(jb // grp):
        def gather(ref):
            return jnp.concatenate([ref[:, g * grp + jj, :] for jj in range(grp)], axis=1)
        p = _apply_stack(gs, gather(bre_ref), passes)
        q = _apply_stack(gs, gather(bim_ref), passes)
        ya = p[:n1h] - q[n1h:]
        yb = p[n1h:] + q[:n1h]
        for jj in range(grp):
            j = g * grp + jj
            for bi, y in ((0, ya), (1, yb)):
                u = u_ref[bi, :, j, :]
                o_ref[bi, :, j, :] = gate_ref[bi, :, j, :] * (y[:, jj * c:(jj + 1) * c] + u * skip)


def _fft_step_a_inv(b_re, b_im, gs, u4, u_col, gate4, gate_col, skip, passes):
    p, n1, _, c = b_re.shape
    n1h = n1 // 2
    jb = 32
    grp = max(1, 1024 // c)
    bspec = pl.BlockSpec((None, n1, jb, c), lambda i, j: (i, 0, j, 0))
    return pl.pallas_call(
        functools.partial(_fai_kernel, grp=grp, passes=passes),
        out_shape=jax.ShapeDtypeStruct((2 * p, n1h, FFT_N2, c), f32),
        grid=(p, FFT_N2 // jb),
        in_specs=[
            bspec, bspec,
            pl.BlockSpec(gs.shape, lambda i, j: (0, 0)),
            pl.BlockSpec((2, n1h, jb, c), lambda i, j: (i, 0, j, u_col)),
            pl.BlockSpec((2, n1h, jb, c), lambda i, j: (i, 0, j, gate_col)),
            pl.BlockSpec((1, c), lambda i, j: (0, 0)),
        ],
        out_specs=pl.BlockSpec((2, n1h, jb, c), lambda i, j: (i, 0, j, 0)),
        compiler_params=_cparams(("parallel", "parallel")),
        name="fft_step_a_inv",
    )(b_re, b_im, gs, u4, gate4, skip)


def _hyena_long(xz, filt, skip, n_seq, L, passes=FFT_PASSES):
    n = 2 * L
    n1, f_n1, f_n2, tw_re, tw_im = _dft_consts(n)
    n1h = n1 // 2
    c = HY_W
    fs_full = _stack_dft(f_n1)
    fs_half = _stack_dft(f_n1[:, :n1h])
    gs = _stack_dft(np.conj(f_n1)[:n1h, :])
    f2s = _stack_dft(f_n2)
    kc4 = filt.reshape(1, n1, FFT_N2, 2 * c)
    k_re, k_im = _fft_step_a(kc4, fs_full, 0, 2 * c, True, FILTER_PASSES)
    kf = _fft_mid(k_re.reshape(1, n, 2 * c), k_im.reshape(1, n, 2 * c), tw_re, tw_im, f2s,
                  passes=FILTER_PASSES)
    kf = (kf[0].reshape(n, 2 * c), kf[1].reshape(n, 2 * c))
    xz4 = xz.reshape(n_seq, n1h, FFT_N2, 3 * c)
    z4, z_col = xz4, 2
    for order in range(2):
        a_re, a_im = _fft_step_a(z4, fs_half, z_col, c, False, passes)
        p = n_seq // 2
        b_re, b_im = _fft_mid(a_re.reshape(p, n, c), a_im.reshape(p, n, c), tw_re, tw_im, f2s,
                              kf=kf, order=order, passes=passes)
        z4 = _fft_step_a_inv(b_re.reshape(p, n1, FFT_N2, c), b_im.reshape(p, n1, FFT_N2, c), gs,
                             z4, z_col, xz4, order, skip[order:order + 1], passes)
        z_col = 0
    return z4.reshape(n_seq * L, c)


def _dense_spec_kernel(k_ref, fs_ref, ore_ref, oim_ref, *, passes):
    n = k_ref.shape[0]
    p = _apply_stack(fs_ref[...], k_ref[...], passes)
    ore_ref[...] = p[:n]
    oim_ref[...] = p[n:]


def _dense_conv_kernel(xz_ref, fs_ref, gs_ref, kr_ref, ki_ref, skip_ref, o_ref, *, passes):
    L = xz_ref.shape[1]
    n = 2 * L
    c = HY_W
    fs, gs = fs_ref[...], gs_ref[...]
    za, zb = xz_ref[0, :, 2 * c:], xz_ref[1, :, 2 * c:]
    for order in range(2):
        p = _apply_stack(fs, za, passes)
        q = _apply_stack(fs, zb, passes)
        sr, si = p[:n] - q[n:], p[n:] + q[:n]
        kr, ki = kr_ref[:, order * c:(order + 1) * c], ki_ref[:, order * c:(order + 1) * c]
        yr, yi = sr * kr - si * ki, sr * ki + si * kr
        p2 = _apply_stack(gs, yr, passes)
        q2 = _apply_stack(gs, yi, passes)
        ya = (p2[:L] - q2[L:]) * (1.0 / n)
        yb = (p2[L:] + q2[:L]) * (1.0 / n)
        skip = skip_ref[order:order + 1, :]
        za = xz_ref[0, :, order * c:(order + 1) * c] * (ya + za * skip)
        zb = xz_ref[1, :, order * c:(order + 1) * c] * (yb + zb * skip)
    o_ref[0] = za
    o_ref[1] = zb


def _hyena_short_seq(xz, filt, skip, n_seq, L, passes=FFT_PASSES):
    n = 2 * L
    c = HY_W
    k = np.arange(n)
    f_n = np.exp(-2j * np.pi * np.outer(k, k) / n)
    fs_full = _stack_dft(f_n)
    fs_half = _stack_dft(f_n[:, :L])
    gs = _stack_dft(np.conj(f_n)[:L, :])
    kc = filt
    shape = jax.ShapeDtypeStruct((n, 2 * c), f32)
    k_re, k_im = pl.pallas_call(
        functools.partial(_dense_spec_kernel, passes=FILTER_PASSES),
        out_shape=[shape, shape],
        compiler_params=_cparams(None),
        name="dense_filter_spectrum",
    )(kc, fs_full)
    skip_p = jnp.zeros((8, c), f32).at[:2].set(skip)
    full = lambda a: pl.BlockSpec(a.shape, lambda i: (0,) * a.ndim)
    out = pl.pallas_call(
        functools.partial(_dense_conv_kernel, passes=passes),
        out_shape=jax.ShapeDtypeStruct((n_seq, L, c), f32),
        grid=(n_seq // 2,),
        in_specs=[pl.BlockSpec((2, L, 3 * c), lambda i: (i, 0, 0)),
                  full(fs_half), full(gs), full(k_re), full(k_im), full(skip_p)],
        out_specs=pl.BlockSpec((2, L, c), lambda i: (i, 0, 0)),
        compiler_params=_cparams(("parallel",)),
        name="dense_long_conv",
    )(xz.reshape(n_seq, L, 3 * c), fs_half, gs, k_re, k_im, skip_p)
    return out.reshape(n_seq * L, c)


def _rope_tables(T, tm):
    n_rows = T // GRID_W
    row = np.repeat(np.arange(n_rows, dtype=np.float32), GRID_W)
    col = np.tile(np.arange(GRID_W, dtype=np.float32), n_rows)
    axis_dim = HEAD_DIM // 2
    inv_freq = np.float32(ROPE_THETA) ** (-np.arange(0, axis_dim, 2, dtype=np.float32) / np.float32(axis_dim))
    ang = np.concatenate([row[:, None] * inv_freq, col[:, None] * inv_freq], axis=-1).astype(np.float64)
    cos = np.zeros((T + tm, HEAD_PAD), np.float32)
    sin = np.zeros((T + tm, HEAD_PAD), np.float32)
    cos[:T, 0:HEAD_DIM:2] = np.cos(ang)
    cos[:T, 1:HEAD_DIM:2] = np.cos(ang)
    sin[:T, 0:HEAD_DIM:2] = -np.sin(ang)
    sin[:T, 1:HEAD_DIM:2] = np.sin(ang)
    cos[T:, :HEAD_DIM] = 1.0
    return jnp.asarray(cos), jnp.asarray(sin)


def _qkprep_kernel(q_ref, kv_ref, cos_ref, sin_ref, qg_ref, kg_ref, avg_ref, oq_ref, okv_ref):
    cos, sin = cos_ref[...], sin_ref[...]
    avg = avg_ref[...]
    lane = lax.broadcasted_iota(jnp.int32, cos.shape, 1)
    even = (lane % 2) == 0

    def norm_rope(x, gain, scale):
        sq_hi, sq_lo = _split(x * x)
        ms = _dot(sq_hi, avg) + _dot(sq_lo, avg)
        xn = x * lax.rsqrt(ms + QK_EPS) * gain
        swapped = jnp.where(even, pltpu.roll(xn, LANE - 1, axis=1), pltpu.roll(xn, 1, axis=1))
        return ((xn * cos + swapped * sin) * scale).astype(bf16)

    for h in range(N_HEADS):
        sl = slice(h * HEAD_PAD, (h + 1) * HEAD_PAD)
        oq_ref[:, sl] = norm_rope(q_ref[:, sl], qg_ref[...], HEAD_DIM ** -0.5)
    for g in range(N_KV_HEADS):
        sl = slice(g * HEAD_PAD, (g + 1) * HEAD_PAD)
        okv_ref[:, sl] = norm_rope(kv_ref[:, sl], kg_ref[...], 1.0)
    v = kv_ref[:, KVP_W:]
    v_lane = lax.broadcasted_iota(jnp.int32, v.shape, 1) % HEAD_PAD
    okv_ref[:, KVP_W:] = jnp.where(v_lane == HEAD_DIM, 1.0, v).astype(bf16)


def _qkprep(q, kv, cos, sin, q_gain, k_gain, pos_block, tm):
    n = q.shape[0]
    pad = lambda g: jnp.zeros((1, HEAD_PAD), f32).at[0, :HEAD_DIM].set(g)
    avg = jnp.full((HEAD_PAD, HEAD_PAD), 1.0 / HEAD_DIM, bf16)
    one = lambda shape: pl.BlockSpec(shape, lambda i: (0, 0))
    return pl.pallas_call(
        _qkprep_kernel,
        out_shape=[jax.ShapeDtypeStruct((n, QP_W), bf16), jax.ShapeDtypeStruct((n, 2 * KVP_W), bf16)],
        grid=(n // tm,),
        in_specs=[
            pl.BlockSpec((tm, QP_W), lambda i: (i, 0)),
            pl.BlockSpec((tm, 2 * KVP_W), lambda i: (i, 0)),
            pl.BlockSpec((tm, HEAD_PAD), lambda i: (pos_block(i), 0)),
            pl.BlockSpec((tm, HEAD_PAD), lambda i: (pos_block(i), 0)),
            one((1, HEAD_PAD)), one((1, HEAD_PAD)), one((HEAD_PAD, HEAD_PAD)),
        ],
        out_specs=[pl.BlockSpec((tm, QP_W), lambda i: (i, 0)),
                   pl.BlockSpec((tm, 2 * KVP_W), lambda i: (i, 0))],
        compiler_params=_cparams(("parallel",)),
        name="qk_prep",
    )(q, kv, cos, sin, pad(q_gain), pad(k_gain), avg)


def _attn_kernel(q_ref, kvl_ref, kvc_ref, o_ref, *, nq_lat):
    i = pl.program_id(1)
    dims = (((1,), (1,)), ((), ()))

    def scores(h, use_lat):
        g = h // N_GROUPS
        ks = slice(g * HEAD_PAD, (g + 1) * HEAD_PAD)
        q = q_ref[:, h * HEAD_PAD:(h + 1) * HEAD_PAD]
        sc = lax.dot_general(q, kvc_ref[:, ks], dims, preferred_element_type=f32).astype(bf16)
        sl = None
        if use_lat:
            sl = lax.dot_general(q, kvl_ref[:, ks], dims, preferred_element_type=f32).astype(bf16)
        return sl, sc

    def heads(use_lat):
        nxt = scores(0, use_lat)
        for h in range(N_HEADS):
            sl, sc = nxt
            if h + 1 < N_HEADS:
                nxt = scores(h + 1, use_lat)
            g = h // N_GROUPS
            vs = slice(KVP_W + g * HEAD_PAD, KVP_W + (g + 1) * HEAD_PAD)
            m = jnp.max(sc, axis=-1, keepdims=True)
            if use_lat:
                m = jnp.maximum(m, jnp.max(sl, axis=-1, keepdims=True))
                acc = _dot(jnp.exp(sl - m), kvl_ref[:, vs]) + _dot(jnp.exp(sc - m), kvc_ref[:, vs])
            else:
                acc = _dot(jnp.exp(sc - m), kvc_ref[:, vs])
            o_ref[:, h * HEAD_PAD:(h + 1) * HEAD_PAD] = (acc / acc[:, HEAD_DIM:HEAD_DIM + 1]).astype(bf16)

    @pl.when(i < nq_lat)
    def _():
        heads(True)

    @pl.when(i == nq_lat)
    def _():
        heads(False)


def _attention(qp, kvp, B, T, Lc):
    n = qp.shape[0]
    tq = Lc
    nq = T // tq
    q_idx = lambda b, i: (jnp.where(i < nq, b * nq + i, B * nq + b), 0)
    return pl.pallas_call(
        functools.partial(_attn_kernel, nq_lat=nq),
        out_shape=jax.ShapeDtypeStruct((n, QP_W), bf16),
        grid=(B, nq + 1),
        in_specs=[
            pl.BlockSpec((tq, QP_W), q_idx),
            pl.BlockSpec((T, 2 * KVP_W), lambda b, i: (b, 0)),
            pl.BlockSpec((Lc, 2 * KVP_W), lambda b, i: (B * nq + b, 0)),
        ],
        out_specs=pl.BlockSpec((tq, QP_W), q_idx),
        compiler_params=_cparams(("parallel", "arbitrary")),
        name="attention",
    )(qp, kvp, kvp)


def _scan_kernel(u_ref, wh_ref, wl_ref, bias_ref, lam_ref, h0_ref, *rest, reverse, final):
    if final:
        hprev_ref, g_ref, o_ref, hend_ref, carry_ref = rest
    else:
        o_ref, hend_ref, carry_ref = rest
    t = pl.program_id(1)

    @pl.when(t == 0)
    def _():
        carry_ref[...] = h0_ref[...]

    u = u_ref[...]
    tt = u.shape[0]
    gates = _dot3(u, wh_ref[...], wl_ref[...]) + bias_ref[...]
    r = jax.nn.sigmoid(gates[:, :RG_W])
    ig = jax.nn.sigmoid(gates[:, RG_W:])
    lam = lam_ref[...]
    softplus = jnp.maximum(-lam, 0.0) + jnp.log1p(jnp.exp(-jnp.abs(lam)))
    log_a = -RG_C * r * softplus
    a = jnp.exp(log_a)
    b = jnp.sqrt(-jnp.tanh(log_a) * (a * a + 1.0)) * (ig * u)
    row = lax.broadcasted_iota(jnp.int32, a.shape, 0)
    s = 1
    while s < tt:
        if reverse:
            a_s, b_s = pltpu.roll(a, tt - s, axis=0), pltpu.roll(b, tt - s, axis=0)
            valid = row < tt - s
        else:
            a_s, b_s = pltpu.roll(a, s, axis=0), pltpu.roll(b, s, axis=0)
            valid = row >= s
        b = a * jnp.where(valid, b_s, 0.0) + b
        a = a * jnp.where(valid, a_s, 1.0)
        s *= 2
    h = a * carry_ref[...] + b
    last = h[0:1, :] if reverse else h[tt - 1:tt, :]
    carry_ref[...] = last
    hend_ref[...] = last
    if final:
        o_ref[...] = ((hprev_ref[...] + h) * jax.nn.gelu(g_ref[...], approximate=True)).astype(o_ref.dtype)
    else:
        o_ref[...] = h


def _rg_scan(u, n_seq, L, w_hi, w_lo, bias, lam, h0, reverse, hprev=None, gate=None, gate_row0=0):
    tt = min(SCAN_TILE, L)
    nt = L // tt
    final = hprev is not None
    tidx = (lambda t: nt - 1 - t) if reverse else (lambda t: t)
    row = lambda b, t: (b * nt + tidx(t), 0)
    one = lambda shape: pl.BlockSpec(shape, lambda b, t: (0, 0))
    in_specs = [pl.BlockSpec((tt, RG_W), row), one(w_hi.shape), one(w_lo.shape), one((1, 2 * RG_W)),
                one((1, RG_W)), pl.BlockSpec((None, 1, RG_W), lambda b, t: (b, 0, 0))]
    args = [u, w_hi, w_lo, bias, lam, h0]
    if final:
        in_specs += [pl.BlockSpec((tt, RG_W), row),
                     pl.BlockSpec((tt, RG_W), lambda b, t: (gate_row0 // tt + b * nt + tidx(t), 1))]
        args += [hprev, gate]
    return pl.pallas_call(
        functools.partial(_scan_kernel, reverse=reverse, final=final),
        out_shape=[jax.ShapeDtypeStruct((n_seq * L, RG_W), bf16 if final else f32),
                   jax.ShapeDtypeStruct((n_seq, 1, RG_W), f32)],
        grid=(n_seq, nt),
        in_specs=in_specs,
        out_specs=[pl.BlockSpec((tt, RG_W), row), pl.BlockSpec((None, 1, RG_W), lambda b, t: (b, 0, 0))],
        scratch_shapes=[pltpu.VMEM((1, RG_W), f32)],
        compiler_params=_cparams(("parallel", "arbitrary")),
        name="rg_scan",
    )(*args)


def _block_diag(w):
    bw = w.shape[-1]
    out = jnp.zeros((RG_W, RG_W), f32)
    for h in range(RG_BLOCKS):
        out = out.at[h * bw:(h + 1) * bw, h * bw:(h + 1) * bw].set(w[h])
    return out


def _rglru(rg, B, T, Lc, conv_w, conv_b, lam, w_a, b_a, w_x, b_x):
    bt = B * T
    u_l = _dwconv(rg, conv_w, conv_b, T, 0, B, 0, 1)
    u_c = _dwconv(rg, conv_w, conv_b, Lc, bt // Lc, B, 0, 1)
    zeros = jnp.zeros((B, 1, RG_W), f32)
    prev_l = prev_c = None
    for d, rev in enumerate((False, True)):
        w_hi, w_lo = _split(jnp.concatenate([_block_diag(w_a[d]), _block_diag(w_x[d])], axis=1))
        bias = jnp.concatenate([b_a[d], b_x[d]])[None, :]
        lam_d = lam[d][None, :]
        last = d == 1
        kw_c = dict(hprev=prev_c, gate=rg, gate_row0=bt) if last else {}
        kw_l = dict(hprev=prev_l, gate=rg, gate_row0=0) if last else {}
        prev_c, h_end = _rg_scan(u_c, B, Lc, w_hi, w_lo, bias, lam_d, zeros, rev, **kw_c)
        prev_l, _ = _rg_scan(u_l, B, T, w_hi, w_lo, bias, lam_d, h_end, rev, **kw_l)
    return prev_l, prev_c


def _layer_norm(v, g, b):
    mu = jnp.mean(v, axis=-1, keepdims=True)
    d = v - mu
    var = jnp.mean(d * d, axis=-1, keepdims=True)
    return d * lax.rsqrt(var + LN_EPS) * g + b


def _outproj_kernel(hyl_ref, hyc_ref, att_ref, rgl_ref, rgc_ref, h_ref, g1_ref, sh2_ref, sc2_ref,
                    lng_ref, lnb_ref, why_ref, watt_ref, wrg_ref, wrh_ref, wrl_ref, br_ref,
                    oh_ref, om_ref, og_ref, oi_ref, oc_ref, *, alpha, n_lat_tiles):
    is_ctx = pl.program_id(0) >= n_lat_tiles
    hy = jnp.where(is_ctx, hyc_ref[...], hyl_ref[...])
    rg = jnp.where(is_ctx, rgc_ref[...], rgl_ref[...])
    y = _dot(hy.astype(bf16), why_ref[...]) + _dot(att_ref[...], watt_ref[...]) + _dot(rg, wrg_ref[...])
    hn = _layer_norm(alpha * h_ref[...] + g1_ref[...] * y, lng_ref[...], lnb_ref[...])
    oh_ref[...] = hn
    m = hn * (1.0 + sc2_ref[...]) + sh2_ref[...]
    om_ref[...] = m.astype(bf16)
    scores = jax.nn.sigmoid(_dot3(m, wrh_ref[...], wrl_ref[...]))
    sel = scores + br_ref[...]
    lane = lax.broadcasted_iota(jnp.int32, sel.shape, 1)
    picked = jnp.zeros_like(scores)
    chosen = jnp.zeros(sel.shape, jnp.int32)
    taken = jnp.zeros_like(scores)
    for k in range(TOP_K):
        mx = jnp.max(sel, axis=-1, keepdims=True)
        first = jnp.min(jnp.where(sel == mx, lane, LANE), axis=-1, keepdims=True)
        hit = lane == first
        picked = jnp.where(hit, scores, picked)
        taken = jnp.where(hit, 1.0, taken)
        sel = jnp.where(hit, -jnp.inf, sel)
        chosen = jnp.where(lane == k, first, chosen)
    og_ref[...] = picked / jnp.sum(picked, axis=-1, keepdims=True) * ROUTED_SCALE
    oi_ref[...] = chosen
    for s in range(oc_ref.shape[0]):
        oc_ref[s] = jnp.sum(taken[s * DISPATCH_TILE:(s + 1) * DISPATCH_TILE], axis=0, keepdims=True)


def _outproj(hy_l, hy_c, att, rg_l, rg_c, h, mods3, ln_g, ln_b, w_hy, w_att, w_rg, wr_hi, wr_lo, b_router,
             seg, tm, alpha):
    n = h.shape[0]
    nl = hy_l.shape[0] // tm
    row = lambda w: pl.BlockSpec((tm, w), lambda i: (i, 0))
    lat = lambda w: pl.BlockSpec((tm, w), lambda i: (jnp.minimum(i, nl - 1), 0))
    ctx = lambda w: pl.BlockSpec((tm, w), lambda i: (jnp.maximum(i - nl, 0), 0))
    one = lambda a: pl.BlockSpec(a.shape, lambda i: (0,) * a.ndim)
    return pl.pallas_call(
        functools.partial(_outproj_kernel, alpha=alpha, n_lat_tiles=nl),
        out_shape=[jax.ShapeDtypeStruct((n, D_MODEL), f32), jax.ShapeDtypeStruct((n, D_MODEL), bf16),
                   jax.ShapeDtypeStruct((n, LANE), f32), jax.ShapeDtypeStruct((n, LANE), jnp.int32),
                   jax.ShapeDtypeStruct((n // DISPATCH_TILE, 1, LANE), f32)],
        grid=(n // tm,),
        in_specs=[lat(HY_W), ctx(HY_W), row(QP_W), lat(RG_W), ctx(RG_W), row(D_MODEL),
                  _mod_spec(seg, 2), _mod_spec(seg, 3), _mod_spec(seg, 4),
                  one(ln_g), one(ln_b), one(w_hy), one(w_att), one(w_rg), one(wr_hi), one(wr_lo),
                  one(b_router)],
        out_specs=[row(D_MODEL), row(D_MODEL), row(LANE), row(LANE),
                   pl.BlockSpec((tm // DISPATCH_TILE, 1, LANE), lambda i: (i, 0, 0))],
        compiler_params=_cparams(("parallel",)),
        name="outproj_router",
    )(hy_l, hy_c, att, rg_l, rg_c, h, mods3, mods3, mods3, ln_g, ln_b, w_hy, w_att, w_rg, wr_hi, wr_lo,
      b_router)


def _xs_rows(tile):
    return -(-(TOP_K * tile + N_EXPERTS * (RUN_ALIGN - 1)) // 256) * 256


def _perm_matrix(idx, runstart, gates):
    t = idx.shape[0]
    lane = lax.broadcasted_iota(jnp.int32, (t, LANE), 1)
    hits = [lane == idx[:, k:k + 1] for k in range(TOP_K)]
    sel = jnp.zeros((t, LANE), f32)
    for hit in hits:
        sel = jnp.where(hit, 1.0, sel)
    earlier = (lax.broadcasted_iota(jnp.int32, (t, t), 1) < lax.broadcasted_iota(jnp.int32, (t, t), 0))
    rank = _dot(jnp.where(earlier, 1.0, 0.0).astype(bf16), sel.astype(bf16))
    posmat = rank + runstart
    slot = lax.broadcasted_iota(jnp.int32, (t, _xs_rows(t)), 1).astype(jnp.int16)
    acc = jnp.zeros(slot.shape, bf16)
    for hit in hits:
        pos = jnp.sum(jnp.where(hit, posmat, 0.0), axis=-1, keepdims=True).astype(jnp.int32).astype(jnp.int16)
        if gates is None:
            val = jnp.ones((t, 1), bf16)
        else:
            val = jnp.sum(jnp.where(hit, gates, 0.0), axis=-1, keepdims=True).astype(bf16)
        acc = jnp.where(slot == pos, val, acc)
    return acc


def _for_pieces(plan_ref, fn):
    for k, size in enumerate(PIECE_SIZES):
        def body(j, carry, k=k, size=size):
            fn(pl.multiple_of(plan_ref[2 * k, j], RUN_ALIGN), pl.multiple_of(plan_ref[2 * k + 1, j], RUN_ALIGN),
               size)
            return carry

        lax.fori_loop(0, plan_ref[PLAN_COUNT_ROW, k], body, 0)


def _dispatch_kernel(gap_ref, reg_ref, x_ref, idx_ref, rsf_ref, plan_ref, prev_ref, s_ref,
                     xs_ref, zero_ref, sem):
    i = pl.program_id(0)
    slot = i % 2

    @pl.when(i == 0)
    def _():
        zero_ref[...] = jnp.zeros(zero_ref.shape, zero_ref.dtype)

        def fill(action):
            def body(e, carry):
                @pl.when(reg_ref[e] > 0)
                def _():
                    dst = pl.multiple_of(gap_ref[e], RUN_ALIGN)
                    action(pltpu.make_async_copy(zero_ref, s_ref.at[pl.ds(dst, FFN_TILE)], sem))
                return carry
            lax.fori_loop(0, N_EXPERTS, body, 0)

        fill(lambda cp: cp.start())
        fill(lambda cp: cp.wait())

    p_t = _perm_matrix(idx_ref[...], rsf_ref[...], None)
    xs_ref[slot] = lax.dot_general(p_t, x_ref[...], (((0,), (0,)), ((), ())),
                                   preferred_element_type=f32).astype(bf16)

    def copy(buf, src, dst, sz):
        return pltpu.make_async_copy(xs_ref.at[buf].at[pl.ds(src, sz)], s_ref.at[pl.ds(dst, sz)], sem)

    @pl.when(i > 0)
    def _():
        _for_pieces(prev_ref, lambda s, d, sz: copy(1 - slot, s, d, sz).wait())

    _for_pieces(plan_ref, lambda s, d, sz: copy(slot, s, d, sz).start())

    @pl.when(i == pl.num_programs(0) - 1)
    def _():
        _for_pieces(plan_ref, lambda s, d, sz: copy(slot, s, d, sz).wait())


def _ffn_kernel(te_ref, nu_ref, x_ref, wg_ref, wu_ref, wd_ref, o_ref):
    @pl.when(pl.program_id(0) < nu_ref[0])
    def _():
        x = x_ref[...]
        hid = jax.nn.silu(_dot(x, wg_ref[...].astype(bf16))) * _dot(x, wu_ref[...].astype(bf16))
        o_ref[...] = _dot(hid.astype(bf16), wd_ref[...].astype(bf16)).astype(bf16)


def _combine_kernel(z_ref, plan_ref, idx_ref, gate_ref, rsf_ref, x_ref, sgu_ref, sd_ref,
                    h_ref, g2_ref, lng_ref, lnb_ref, o_ref, zbuf_ref, sem, *, alpha):
    i = pl.program_id(0)

    @pl.when(i == 0)
    def _():
        zbuf_ref[...] = jnp.zeros(zbuf_ref.shape, zbuf_ref.dtype)

    def copy(src, dst, sz):
        return pltpu.make_async_copy(z_ref.at[pl.ds(dst, sz)], zbuf_ref.at[pl.ds(src, sz)], sem)

    _for_pieces(plan_ref, lambda s, d, sz: copy(s, d, sz).start())
    hs = _dot(x_ref[...], sgu_ref[...])
    shared = _dot((jax.nn.silu(hs[:, :EXPERT_FF]) * hs[:, EXPERT_FF:]).astype(bf16), sd_ref[...])
    w_t = _perm_matrix(idx_ref[...], rsf_ref[...], gate_ref[...])
    _for_pieces(plan_ref, lambda s, d, sz: copy(s, d, sz).wait())
    y = _dot(w_t, zbuf_ref[...]) + shared
    o_ref[...] = _layer_norm(alpha * h_ref[...] + g2_ref[...] * y, lng_ref[...], lnb_ref[...])


def _route_plan(counts, n_ffn_tiles):
    c = counts[:, 0, :N_EXPERTS].astype(jnp.int32)
    lens = (c + RUN_ALIGN - 1) // RUN_ALIGN * RUN_ALIGN
    runstart = jnp.cumsum(lens, axis=1) - lens
    region = (jnp.sum(lens, axis=0) + FFN_TILE - 1) // FFN_TILE * FFN_TILE
    region_end = jnp.cumsum(region)
    dest = (region_end - region)[None, :] + jnp.cumsum(lens, axis=0) - lens
    n_used = region_end[-1] // FFN_TILE
    tile_row = jnp.minimum(jnp.arange(n_ffn_tiles, dtype=jnp.int32), n_used - 1) * FFN_TILE
    tile_expert = jnp.minimum(jnp.sum(tile_row[:, None] >= region_end[None, :], axis=1), N_EXPERTS - 1)
    runstart_f = jnp.zeros((c.shape[0], 1, LANE), f32).at[:, 0, :N_EXPERTS].set(runstart.astype(f32))

    def compact(mask, *vals):
        pos = jnp.cumsum(mask, axis=1) - mask
        hit = jnp.logical_and(mask[:, :, None] > 0, pos[:, :, None] == jnp.arange(LANE, dtype=jnp.int32))
        return [jnp.sum(jnp.where(hit, v[:, :, None], 0), axis=1) for v in vals], jnp.sum(mask, axis=1)

    n_chunks = lens // RUN_CHUNK
    j = jnp.arange(DISPATCH_TILE // RUN_CHUNK, dtype=jnp.int32) * RUN_CHUNK
    wide = lambda v: (v[:, :, None] + j).reshape(v.shape[0], -1)
    chunk_mask = (j < (n_chunks * RUN_CHUNK)[:, :, None]).astype(jnp.int32).reshape(lens.shape[0], -1)
    (s64, d64), c64 = compact(chunk_mask, wide(runstart), wide(dest))
    off = n_chunks * RUN_CHUNK
    (s32, d32), c32 = compact((lens >> 5) & 1, runstart + off, dest + off)
    off = off + (lens & 32)
    (s16, d16), c16 = compact((lens >> 4) & 1, runstart + off, dest + off)
    count_row = jnp.zeros_like(s64).at[:, 0].set(c64).at[:, 1].set(c32).at[:, 2].set(c16)
    pieces = jnp.stack([s64, d64, s32, d32, s16, d16, count_row, jnp.zeros_like(s64)], axis=1)
    return dict(pieces=pieces.astype(jnp.int32), gap=region_end - FFN_TILE, region=region,
                n_used=n_used.reshape(1), tile_expert=tile_expert.astype(jnp.int32), runstart_f=runstart_f)


def _moe(m, gates, idx, counts, layer, w_gate, w_up, w_down, sgu, sd, h, mods3, ln_g, ln_b, seg, alpha):
    n = h.shape[0]
    t = DISPATCH_TILE
    nt = n // t
    xs_rows = _xs_rows(t)
    rows_max = TOP_K * n + nt * N_EXPERTS * (RUN_ALIGN - 1) + N_EXPERTS * (FFN_TILE - RUN_ALIGN)
    n_ffn_tiles = -(-rows_max // FFN_TILE)
    plan = _route_plan(counts, n_ffn_tiles)
    tile = lambda w: pl.BlockSpec((t, w), lambda i, *_: (i, 0))
    rsf_spec = pl.BlockSpec((None, 1, LANE), lambda i, *_: (i, 0, 0))
    one = lambda a: pl.BlockSpec(a.shape, lambda i, *_: (0,) * a.ndim)
    any_spec = pl.BlockSpec(memory_space=pl.ANY)

    pieces = plan["pieces"]
    plan_spec = pl.BlockSpec((None, PLAN_ROWS, LANE), lambda i, *_: (i, 0, 0), memory_space=pltpu.SMEM)
    prev_spec = pl.BlockSpec((None, PLAN_ROWS, LANE), lambda i, *_: (jnp.maximum(i - 1, 0), 0, 0),
                             memory_space=pltpu.SMEM)
    sorted_x = pl.pallas_call(
        _dispatch_kernel,
        out_shape=jax.ShapeDtypeStruct((n_ffn_tiles * FFN_TILE, D_MODEL), bf16),
        grid_spec=pltpu.PrefetchScalarGridSpec(
            num_scalar_prefetch=2, grid=(nt,),
            in_specs=[tile(D_MODEL), tile(LANE), rsf_spec, plan_spec, prev_spec],
            out_specs=any_spec,
            scratch_shapes=[pltpu.VMEM((2, xs_rows, D_MODEL), bf16), pltpu.VMEM((FFN_TILE, D_MODEL), bf16),
                            pltpu.SemaphoreType.DMA]),
        compiler_params=_cparams(("arbitrary",)),
        name="moe_dispatch",
    )(plan["gap"], plan["region"], m, idx, plan["runstart_f"], pieces, pieces)

    used = lambda i, te, nu: jnp.minimum(i, nu[0] - 1)
    sorted_z = pl.pallas_call(
        _ffn_kernel,
        out_shape=jax.ShapeDtypeStruct((n_ffn_tiles * FFN_TILE, D_MODEL), bf16),
        grid_spec=pltpu.PrefetchScalarGridSpec(
            num_scalar_prefetch=2, grid=(n_ffn_tiles,),
            in_specs=[pl.BlockSpec((FFN_TILE, D_MODEL), lambda i, te, nu: (used(i, te, nu), 0)),
                      pl.BlockSpec((None, None, D_MODEL, EXPERT_FF), lambda i, te, nu: (layer, te[i], 0, 0)),
                      pl.BlockSpec((None, None, D_MODEL, EXPERT_FF), lambda i, te, nu: (layer, te[i], 0, 0)),
                      pl.BlockSpec((None, None, EXPERT_FF, D_MODEL), lambda i, te, nu: (layer, te[i], 0, 0))],
            out_specs=pl.BlockSpec((FFN_TILE, D_MODEL), lambda i, te, nu: (used(i, te, nu), 0))),
        compiler_params=_cparams(("arbitrary",)),
        name="moe_ffn",
    )(plan["tile_expert"], plan["n_used"], sorted_x, w_gate, w_up, w_down)

    return pl.pallas_call(
        functools.partial(_combine_kernel, alpha=alpha),
        out_shape=jax.ShapeDtypeStruct((n, D_MODEL), f32),
        grid=(nt,),
        in_specs=[any_spec, plan_spec, tile(LANE), tile(LANE), rsf_spec, tile(D_MODEL), one(sgu), one(sd),
                  tile(D_MODEL), pl.BlockSpec((None, 1, D_MODEL), lambda i, *_: (seg(i), 0, 5)),
                  one(ln_g), one(ln_b)],
        out_specs=tile(D_MODEL),
        scratch_shapes=[pltpu.VMEM((xs_rows, D_MODEL), bf16), pltpu.SemaphoreType.DMA],
        compiler_params=_cparams(("arbitrary",)),
        name="moe_combine",
    )(sorted_z, pieces, idx, gates, plan["runstart_f"], m, sgu, sd, h, mods3, ln_g, ln_b)


def _pad_heads_cols(w, n_heads):
    lead = w.shape[:-1]
    w = w.reshape(*lead, n_heads, HEAD_DIM)
    w = jnp.concatenate([w, jnp.zeros_like(w)], axis=-1)
    return w.reshape(*lead, n_heads * HEAD_PAD)


def _pad_in_proj(w_in):
    hy, q, k, v, rgx, rgg = (w_in[..., :HY_END], w_in[..., HY_END:Q_END], w_in[..., Q_END:K_END],
                             w_in[..., K_END:V_END], w_in[..., V_END:RGX_END], w_in[..., RGX_END:])
    return jnp.concatenate([hy, _pad_heads_cols(q, N_HEADS), _pad_heads_cols(k, N_KV_HEADS),
                            _pad_heads_cols(v, N_KV_HEADS), rgx, rgg], axis=-1).astype(bf16)


def kernel(x, c, ctx, c_ctx, w_mod, b_mod, ln1_g, ln1_b, ln2_g, ln2_b, w_in, w_out,
           hy_short_w, hy_short_b, hy_f_w1, hy_f_b1, hy_f_freq, hy_f_w2, hy_f_b2, hy_f_w3, hy_skip,
           q_norm, k_norm, rg_conv_w, rg_conv_b, rg_lambda, rg_w_a, rg_b_a, rg_w_x, rg_b_x,
           w_router, b_router, w_gate, w_up, w_down, ws_gate, ws_up, ws_down):
    B, T, D = x.shape
    Lc = ctx.shape[1]
    depth = w_mod.shape[0]
    bt, bc = B * T, B * Lc
    n_tok = bt + bc
    tm = TOKEN_TILE
    assert D == D_MODEL and T % tm == 0 and bc % tm == 0 and B % 2 == 0
    assert T % ATT_Q_TILE == 0 and Lc == ATT_Q_TILE and T % GRID_W == 0
    alpha = (2 * depth) ** 0.25
    tiles_per_seq = T // tm
    seg = lambda i: jnp.minimum(i // tiles_per_seq, B)
    pos_block = lambda i: jnp.where(i < B * tiles_per_seq, i % tiles_per_seq, tiles_per_seq)
    moe_tm = DISPATCH_TILE
    assert T % moe_tm == 0 and bc % moe_tm == 0
    moe_seg = lambda i: jnp.minimum(i // (T // moe_tm), B)
    out_tm = max(t for t in ROUTER_TILES if T % t == 0 and bc % t == 0)
    out_seg = lambda i: jnp.minimum(i // (T // out_tm), B)

    rows = -(-(B + 1) // 8) * 8
    cond = jnp.zeros((rows, D), f32).at[:B].set(c).at[B].set(c_ctx)
    mods = _modulation(cond, w_mod, b_mod)

    cos, sin = _rope_tables(T, tm)
    w_in_pad = _pad_in_proj(w_in)
    w_out_hy = w_out[:, :HY_W].astype(bf16)
    w_out_att = _pad_heads_cols(w_out[:, HY_W:HY_W + ATT_W].transpose(0, 2, 1), N_HEADS).transpose(0, 2, 1).astype(bf16)
    w_out_rg = w_out[:, HY_W + ATT_W:].astype(bf16)
    wr_pad = jnp.zeros((depth, D, LANE), f32).at[:, :, :N_EXPERTS].set(w_router)
    br_pad = jnp.full((depth, 1, LANE), -jnp.inf, f32).at[:, 0, :N_EXPERTS].set(b_router)
    sgu = jnp.concatenate([ws_gate, ws_up], axis=-1).astype(bf16)
    sd = ws_down.astype(bf16)

    h = jnp.concatenate([x.reshape(bt, D), ctx.reshape(bc, D)], axis=0)
    for l in range(depth):
        mods3 = mods[l].reshape(rows, 1, 6 * D)
        hy, q, kv, rg = _inproj(h, mods3, w_in_pad[l], seg, tm)

        xz_l = _dwconv(hy, hy_short_w[l], hy_short_b[l], T, 0, B, 0, 3)
        xz_c = _dwconv(hy, hy_short_w[l], hy_short_b[l], Lc, bt // Lc, B, 0, 3)
        fargs = (hy_f_w1[l], hy_f_b1[l], hy_f_freq[l], hy_f_w2[l], hy_f_b2[l], hy_f_w3[l])
        hy_l = _hyena_long(xz_l, _hyena_filters(T, *fargs), hy_skip[l], B, T)
        hy_c = _hyena_short_seq(xz_c, _hyena_filters(Lc, *fargs), hy_skip[l], B, Lc)

        qp, kvp = _qkprep(q, kv, cos, sin, q_norm[l], k_norm[l], pos_block, tm)
        att = _attention(qp, kvp, B, T, Lc)

        rg_l, rg_c = _rglru(rg, B, T, Lc, rg_conv_w[l], rg_conv_b[l], rg_lambda[l],
                            rg_w_a[l], rg_b_a[l], rg_w_x[l], rg_b_x[l])

        wr_hi, wr_lo = _split(wr_pad[l])
        h, m, gates, chosen, counts = _outproj(
            hy_l, hy_c, att, rg_l, rg_c, h, mods3, ln1_g[l][None, :], ln1_b[l][None, :],
            w_out_hy[l], w_out_att[l], w_out_rg[l], wr_hi, wr_lo, br_pad[l], out_seg, out_tm, alpha)
        h = _moe(m, gates, chosen, counts, l, w_gate, w_up, w_down, sgu[l], sd[l], h, mods3,
                 ln2_g[l][None, :], ln2_b[l][None, :], moe_seg, alpha)
    return h[:bt].reshape(B, T, D)
```
